```python
import math
import jax, jax.numpy as jnp
from jax import lax
import numpy as np

D_MODEL = 1024
BATCH = 16
SEQ = 256
DEPTH = 2
DEC_BATCH = 4
DEC_SEQ = 4096
PAST_LEN = 512

GRID_W = 64
N_DIR = 2
CHUNK = 64
Q_BLOCK = 128
ROPE_BASE = 10000.0
EPS = 1e-6

A_HEADS = 4
A_QK_DIM = 64
A_V_DIM = 2 * A_QK_DIM
A_WIDTH = A_HEADS * A_V_DIM
A_QK_COLS = A_HEADS * 2 * A_QK_DIM
B_HEADS = 4
B_DIM = 64
B_WIDTH = B_HEADS * B_DIM
CONV_K = 5
C_HEADS = 4
C_DIM = 64
C_WIDTH = C_HEADS * C_DIM

MIX_WIDTH = A_WIDTH + B_WIDTH + C_WIDTH
IN_SIZES = (A_QK_COLS, A_QK_COLS, A_WIDTH,
            B_WIDTH, B_WIDTH, B_WIDTH, B_WIDTH, N_DIR * B_HEADS, N_DIR * B_HEADS,
            C_WIDTH, C_WIDTH, C_WIDTH, C_WIDTH, N_DIR * C_HEADS, N_DIR * C_HEADS)
IN_WIDTH = 2 * A_QK_COLS + A_WIDTH + 4 * B_WIDTH + 2 * N_DIR * B_HEADS + 4 * C_WIDTH + 2 * N_DIR * C_HEADS
FFN_HIDDEN = -((-8 * D_MODEL) // (3 * 256)) * 256

kernel_name = 'hybrid_diffusion_step'


def rms_norm(x, g):
    xf = x.astype(jnp.float32)
    y = xf * lax.rsqrt(jnp.mean(xf * xf, axis=-1, keepdims=True) + EPS)
    return (y * g.astype(jnp.float32)).astype(x.dtype)


def l2_norm(x):
    xf = x.astype(jnp.float32)
    return (xf * lax.rsqrt(jnp.sum(xf * xf, axis=-1, keepdims=True) + EPS)).astype(x.dtype)


def ada_mod(cvec, w, b):
    m = jnp.matmul(jax.nn.silu(cvec), w) + b
    return jnp.split(m, 6, axis=-1)


def swiglu(u, w_gu, w_dn):
    gate, up = jnp.split(jnp.einsum('bld,df->blf', u, w_gu), 2, axis=-1)
    return jnp.einsum('blf,fd->bld', jax.nn.silu(gate) * up, w_dn)


def axial_rope(x):
    L = x.shape[1]
    n_rows = L // GRID_W
    rows = jnp.repeat(jnp.arange(n_rows, dtype=jnp.float32), GRID_W)
    cols = jnp.tile(jnp.arange(GRID_W, dtype=jnp.float32), n_rows)
    n_freq = A_QK_DIM // 4
    inv_freq = ROPE_BASE ** (-jnp.arange(n_freq, dtype=jnp.float32) / n_freq)
    ang = jnp.concatenate([rows[:, None] * inv_freq, cols[:, None] * inv_freq], axis=-1)
    cos = jnp.cos(ang)[None, :, None, None, :].astype(x.dtype)
    sin = jnp.sin(ang)[None, :, None, None, :].astype(x.dtype)
    xp = x.reshape(x.shape[:-1] + (A_QK_DIM // 2, 2))
    x0, x1 = xp[..., 0], xp[..., 1]
    return jnp.stack([x0 * cos - x1 * sin, x0 * sin + x1 * cos], axis=-1).reshape(x.shape)


def centred_dwconv(x, w):
    return lax.conv_general_dilated(
        x, w[:, None, :].astype(x.dtype), window_strides=(1,),
        padding=[(CONV_K // 2, CONV_K // 2)], dimension_numbers=('NWC', 'WIO', 'NWC'),
        feature_group_count=x.shape[-1])


def diff_lambda(lqk, lam_init):
    lf = lqk.astype(jnp.float32)
    return jnp.exp(jnp.sum(lf[0] * lf[1])) - jnp.exp(jnp.sum(lf[2] * lf[3])) + lam_init


def diff_attention(q, k, v, lam):
    Bn, Lq = q.shape[:2]
    nb = Lq // Q_BLOCK
    qb = jnp.moveaxis(q.reshape((Bn, nb, Q_BLOCK) + q.shape[2:]), 1, 0)
    scale = A_QK_DIM ** -0.5

    def block(qi):
        s = jnp.einsum('bqhmd,bkhmd->bhmqk', qi, k).astype(jnp.float32) * scale
        p = jax.nn.softmax(s, axis=-1)
        w = p[:, :, 0] - lam * p[:, :, 1]
        return jnp.einsum('bhqk,bkhe->bqhe', w.astype(v.dtype), v)

    out = lax.map(block, qb)
    return jnp.moveaxis(out, 0, 1).reshape(Bn, Lq, A_HEADS, A_V_DIM)


def to_chunks(x):
    Bn, L, H = x.shape[:3]
    x = x.reshape((Bn, L // CHUNK, CHUNK, H) + x.shape[3:])
    return jnp.moveaxis(x, (1, 3), (0, 2))


def from_chunks(x):
    x = jnp.moveaxis(x, (0, 2), (1, 3))
    Bn, n, c, H, d = x.shape
    return x.reshape(Bn, n * c, H, d)


def chunk_masks():
    idx = jnp.arange(CHUNK)
    return idx[:, None] >= idx[None, :], idx[:, None] > idx[None, :]


def gated_delta_chunked(q, k, v, beta, g, s0):
    f32 = jnp.float32
    incl, strict = chunk_masks()
    q, k, v = to_chunks(q.astype(f32)), to_chunks(k.astype(f32)), to_chunks(v.astype(f32))
    beta, g = to_chunks(beta.astype(f32)), to_chunks(g.astype(f32))
    gc = jnp.cumsum(g, axis=-1)
    dd = gc[..., :, None] - gc[..., None, :]
    kb = k * beta[..., None]
    a_kk = jnp.einsum('nbhid,nbhjd->nbhij', kb, k) * jnp.exp(jnp.where(strict, dd, -jnp.inf))
    eye = jnp.eye(CHUNK, dtype=f32)
    t_mat = lax.linalg.triangular_solve(eye + a_kk, jnp.broadcast_to(eye, a_kk.shape),
                                        left_side=True, lower=True, unit_diagonal=True)
    u = jnp.matmul(t_mat, v * beta[..., None])
    w = jnp.matmul(t_mat, kb * jnp.exp(gc)[..., None])
    a_qk = jnp.einsum('nbhid,nbhjd->nbhij', q, k) * jnp.exp(jnp.where(incl, dd, -jnp.inf))
    q_dec = q * jnp.exp(gc)[..., None]
    k_dec = k * jnp.exp(gc[..., -1:] - gc)[..., None]
    g_last = jnp.exp(gc[..., -1])

    def step(S, inp):
        qd, kd, ui, wi, aqk, gl = inp
        v_new = ui - jnp.einsum('bhck,bhkv->bhcv', wi, S)
        o = jnp.einsum('bhck,bhkv->bhcv', qd, S) + jnp.einsum('bhcs,bhsv->bhcv', aqk, v_new)
        S = S * gl[..., None, None] + jnp.einsum('bhck,bhcv->bhkv', kd, v_new)
        return S, o

    S, o = lax.scan(step, s0.astype(f32), (q_dec, k_dec, u, w, a_qk, g_last))
    return from_chunks(o), S


def mlstm_chunked(q, k, v, ig, lf, c0, n0, m0):
    f32 = jnp.float32
    incl, _ = chunk_masks()
    q, k, v = to_chunks(q.astype(f32)), to_chunks(k.astype(f32)), to_chunks(v.astype(f32))
    ig, lf = to_chunks(ig.astype(f32)), to_chunks(lf.astype(f32))
    b = jnp.cumsum(lf, axis=-1)
    dmat = jnp.where(incl, b[..., :, None] - b[..., None, :] + ig[..., None, :], -jnp.inf)
    kv_log = b[..., -1:] - b + ig
    qk = jnp.einsum('nbhid,nbhjd->nbhij', q, k)

    def step(carry, inp):
        cs, ns, ms = carry
        qi, ki, vi, bi, di, kvl, qki = inp
        inter = bi + ms[..., None]
        m_t = jnp.maximum(inter, jnp.max(di, axis=-1))
        s = qki * jnp.exp(di - m_t[..., None])
        w_inter = jnp.exp(inter - m_t)
        num = w_inter[..., None] * jnp.einsum('bhcd,bhde->bhce', qi, cs) + jnp.einsum('bhcs,bhse->bhce', s, vi)
        den = w_inter * jnp.einsum('bhcd,bhd->bhc', qi, ns) + jnp.sum(s, axis=-1)
        h = num / jnp.maximum(jnp.abs(den), jnp.exp(-m_t))[..., None]
        bl = bi[..., -1]
        m_new = jnp.maximum(bl + ms, jnp.max(kvl, axis=-1))
        wk = jnp.exp(kvl - m_new[..., None])
        dec = jnp.exp(bl + ms - m_new)
        cs = dec[..., None, None] * cs + jnp.einsum('bhc,bhcd,bhce->bhde', wk, ki, vi)
        ns = dec[..., None] * ns + jnp.einsum('bhc,bhcd->bhd', wk, ki)
        return (cs, ns, m_new), h

    state, h = lax.scan(step, (c0.astype(f32), n0.astype(f32), m0.astype(f32)), (q, k, v, b, dmat, kv_log, qk))
    return from_chunks(h), state


def flip(x):
    return jnp.flip(x, axis=1)


def delta_bidir(q, k, v, beta, g, s0):
    o_f, s_f = gated_delta_chunked(q, k, v, beta[:, :, 0], g[:, :, 0], s0[:, 0])
    o_b, s_b = gated_delta_chunked(flip(q), flip(k), flip(v), flip(beta[:, :, 1]), flip(g[:, :, 1]), s0[:, 1])
    return (o_f + flip(o_b)).astype(v.dtype), jnp.stack([s_f, s_b], axis=1)


def mlstm_bidir(q, k, v, ig, lf, c0, n0, m0):
    h_f, st_f = mlstm_chunked(q, k, v, ig[:, :, 0], lf[:, :, 0], c0[:, 0], n0[:, 0], m0[:, 0])
    h_b, st_b = mlstm_chunked(flip(q), flip(k), flip(v), flip(ig[:, :, 1]), flip(lf[:, :, 1]),
                              c0[:, 1], n0[:, 1], m0[:, 1])
    states = (jnp.stack([st_f[0], st_b[0]], axis=1), jnp.stack([st_f[1], st_b[1]], axis=1),
              jnp.stack([st_f[2], st_b[2]], axis=1))
    return (h_f + flip(h_b)).astype(v.dtype), states


def prep_mixers(u, w_in_l, b_in_l, conv_w, a_log, dt_bias, f_bias):
    f32 = jnp.float32
    Bn, L, _ = u.shape
    split_at = [int(s) for s in np.cumsum(IN_SIZES)[:-1]]
    (aq, ak, av, bq, bk, bv, bg, ba, bb, cq, ck, cv, co, ci, cf) = jnp.split(
        jnp.einsum('bld,de->ble', u, w_in_l) + b_in_l, split_at, axis=-1)
    attn = (aq.reshape(Bn, L, A_HEADS, 2, A_QK_DIM), ak.reshape(Bn, L, A_HEADS, 2, A_QK_DIM),
            av.reshape(Bn, L, A_HEADS, A_V_DIM))
    qkv = jax.nn.silu(centred_dwconv(jnp.concatenate([bq, bk, bv], axis=-1), conv_w))
    dq, dk, dv = jnp.split(qkv, 3, axis=-1)
    beta = jax.nn.sigmoid(bb.astype(f32)).reshape(Bn, L, N_DIR, B_HEADS)
    g = -jnp.exp(a_log.astype(f32)) * jax.nn.softplus(ba.astype(f32).reshape(Bn, L, N_DIR, B_HEADS) + dt_bias.astype(f32))
    delta = (l2_norm(dq.reshape(Bn, L, B_HEADS, B_DIM)) * (B_DIM ** -0.5),
             l2_norm(dk.reshape(Bn, L, B_HEADS, B_DIM)), dv.reshape(Bn, L, B_HEADS, B_DIM), beta, g)
    ig = ci.astype(f32).reshape(Bn, L, N_DIR, C_HEADS)
    lf = jax.nn.log_sigmoid(cf.astype(f32).reshape(Bn, L, N_DIR, C_HEADS) + f_bias.astype(f32))
    mlstm = (cq.reshape(Bn, L, C_HEADS, C_DIM), ck.reshape(Bn, L, C_HEADS, C_DIM) * (C_DIM ** -0.5),
             cv.reshape(Bn, L, C_HEADS, C_DIM), ig, lf)
    return attn, delta, mlstm, bg, co


def merge_heads(a_out, o_delta, bg, h_m, co, attn_g, delta_g, mlstm_g, w_out_l, lam_init):
    Bn, L = a_out.shape[:2]
    a = rms_norm(a_out, attn_g) * (1.0 - lam_init)
    d = rms_norm(o_delta, delta_g) * jax.nn.silu(bg.reshape(Bn, L, B_HEADS, B_DIM))
    m = jax.nn.sigmoid(co.reshape(Bn, L, C_HEADS, C_DIM)) * rms_norm(h_m, mlstm_g)
    cat = jnp.concatenate([a.reshape(Bn, L, A_WIDTH), d.reshape(Bn, L, B_WIDTH), m.reshape(Bn, L, C_WIDTH)], axis=-1)
    return jnp.einsum('ble,ed->bld', cat, w_out_l)


def context_mixer(u, lp, lam_init):
    (w_in_l, b_in_l, w_out_l, lqk, attn_g, conv_w, a_log, dt_bias, delta_g, f_bias, mlstm_g) = lp
    (aq, ak, av), dl, ml, bg, co = prep_mixers(u, w_in_l, b_in_l, conv_w, a_log, dt_bias, f_bias)
    Bn = u.shape[0]
    f32 = jnp.float32
    a_out = diff_attention(aq, ak, av, diff_lambda(lqk, lam_init))
    o_d, s_d = delta_bidir(dl[0], dl[1], dl[2], dl[3], dl[4], jnp.zeros((Bn, N_DIR, B_HEADS, B_DIM, B_DIM), f32))
    h_m, (c_m, n_m, m_m) = mlstm_bidir(ml[0], ml[1], ml[2], ml[3], ml[4],
                                       jnp.zeros((Bn, N_DIR, C_HEADS, C_DIM, C_DIM), f32),
                                       jnp.zeros((Bn, N_DIR, C_HEADS, C_DIM), f32),
                                       jnp.zeros((Bn, N_DIR, C_HEADS), f32))
    out = merge_heads(a_out, o_d, bg, h_m, co, attn_g, delta_g, mlstm_g, w_out_l, lam_init)
    return out, ak, av, s_d, c_m, n_m, m_m


def latent_mixer(u, k_ctx, v_ctx, s_delta, c_st, n_st, m_st, lp, lam_init):
    (w_in_l, b_in_l, w_out_l, lqk, attn_g, conv_w, a_log, dt_bias, delta_g, f_bias, mlstm_g) = lp
    (aq, ak, av), dl, ml, bg, co = prep_mixers(u, w_in_l, b_in_l, conv_w, a_log, dt_bias, f_bias)
    k_all = jnp.concatenate([axial_rope(ak), k_ctx.astype(ak.dtype)], axis=1)
    v_all = jnp.concatenate([av, v_ctx.astype(av.dtype)], axis=1)
    a_out = diff_attention(axial_rope(aq), k_all, v_all, diff_lambda(lqk, lam_init))
    o_d, _ = delta_bidir(dl[0], dl[1], dl[2], dl[3], dl[4], s_delta)
    h_m, _ = mlstm_bidir(ml[0], ml[1], ml[2], ml[3], ml[4], c_st, n_st, m_st)
    return merge_heads(a_out, o_d, bg, h_m, co, attn_g, delta_g, mlstm_g, w_out_l, lam_init)


def setup_inputs(seed: int = 0) -> dict:
    key = jax.random.key(seed)
    ks = jax.random.split(key, 32)
    f32 = jnp.float32

    def nrm(k, shape, scale):
        return jax.random.normal(k, shape, f32) * scale

    dt = jnp.exp(jax.random.uniform(ks[21], (DEPTH, N_DIR, B_HEADS), f32, math.log(1e-3), math.log(1e-1)))
    return {
        'x_prompt': nrm(ks[0], (BATCH, SEQ, D_MODEL), 1.0),
        'x_sample': nrm(ks[1], (DEC_BATCH, DEC_SEQ, D_MODEL), 1.0),
        'cache_attn_k': nrm(ks[2], (DEC_BATCH, DEPTH, PAST_LEN, A_HEADS, 2, A_QK_DIM), 1.0),
        'cache_attn_v': nrm(ks[3], (DEC_BATCH, DEPTH, PAST_LEN, A_HEADS, A_V_DIM), 1.0),
        'state_delta': nrm(ks[4], (DEC_BATCH, DEPTH, N_DIR, B_HEADS, B_DIM, B_DIM), 0.3),
        'state_mlstm_C': nrm(ks[5], (DEC_BATCH, DEPTH, N_DIR, C_HEADS, C_DIM, C_DIM), 0.5),
        'state_mlstm_n': nrm(ks[6], (DEC_BATCH, DEPTH, N_DIR, C_HEADS, C_DIM), 0.5),
        'state_mlstm_m': nrm(ks[7], (DEC_BATCH, DEPTH, N_DIR, C_HEADS), 1.0),
        'c': nrm(ks[8], (DEC_BATCH, D_MODEL), 1.0),
        'c_ctx': nrm(ks[9], (D_MODEL,), 1.0),
        'norm1_g': 1.0 + nrm(ks[10], (DEPTH, D_MODEL), 0.05),
        'norm2_g': 1.0 + nrm(ks[11], (DEPTH, D_MODEL), 0.05),
        'w_mod': nrm(ks[12], (DEPTH, D_MODEL, 6 * D_MODEL), 0.5 * D_MODEL ** -0.5),
        'b_mod': nrm(ks[13], (DEPTH, 6 * D_MODEL), 0.02),
        'w_in': nrm(ks[14], (DEPTH, D_MODEL, IN_WIDTH), D_MODEL ** -0.5),
        'b_in': nrm(ks[15], (DEPTH, IN_WIDTH), 0.02),
        'w_out': nrm(ks[16], (DEPTH, MIX_WIDTH, D_MODEL), MIX_WIDTH ** -0.5),
        'lambda_qk': nrm(ks[17], (DEPTH, 4, A_QK_DIM), 0.1),
        'attn_subln_g': 1.0 + nrm(ks[18], (DEPTH, A_V_DIM), 0.05),
        'delta_conv_w': nrm(ks[19], (DEPTH, CONV_K, 3 * B_WIDTH), CONV_K ** -0.5),
        'delta_A_log': jnp.log(jax.random.uniform(ks[20], (DEPTH, N_DIR, B_HEADS), f32, 1.0, 16.0)),
        'delta_dt_bias': dt + jnp.log(-jnp.expm1(-dt)),
        'delta_norm_g': 1.0 + nrm(ks[22], (DEPTH, B_DIM), 0.05),
        'mlstm_f_bias': jax.random.uniform(ks[23], (DEPTH, N_DIR, C_HEADS), f32, 3.0, 6.0),
        'mlstm_norm_g': 1.0 + nrm(ks[24], (DEPTH, C_DIM), 0.05),
        'w_gate_up': nrm(ks[25], (DEPTH, D_MODEL, 2 * FFN_HIDDEN), D_MODEL ** -0.5),
        'w_down': nrm(ks[26], (DEPTH, FFN_HIDDEN, D_MODEL), FFN_HIDDEN ** -0.5),
        'final_norm_g': 1.0 + nrm(ks[27], (D_MODEL,), 0.05),
    }


def reference(x_prompt, x_sample, cache_attn_k, cache_attn_v, state_delta, state_mlstm_C, state_mlstm_n,
              state_mlstm_m, c, c_ctx, norm1_g, norm2_g, w_mod, b_mod, w_in, b_in, w_out, lambda_qk,
              attn_subln_g, delta_conv_w, delta_A_log, delta_dt_bias, delta_norm_g, mlstm_f_bias,
              mlstm_norm_g, w_gate_up, w_down, final_norm_g):
    xp, xs = x_prompt, x_sample
    sdt = x_prompt.dtype
    ks_l, vs_l, sd_l, cm_l, nm_l, mm_l = [], [], [], [], [], []
    for l in range(DEPTH):
        lp = (w_in[l], b_in[l], w_out[l], lambda_qk[l], attn_subln_g[l], delta_conv_w[l], delta_A_log[l],
              delta_dt_bias[l], delta_norm_g[l], mlstm_f_bias[l], mlstm_norm_g[l])
        lam_init = 0.8 - 0.6 * math.exp(-0.3 * l)
        sh1, sc1, g1, sh2, sc2, g2 = ada_mod(c_ctx, w_mod[l], b_mod[l])
        u = rms_norm(xp, norm1_g[l]) * (1.0 + sc1) + sh1
        mix, k_ctx, v_ctx, s_d, c_m, n_m, m_m = context_mixer(u, lp, lam_init)
        xp = xp + g1 * mix
        xp = xp + g2 * swiglu(rms_norm(xp, norm2_g[l]) * (1.0 + sc2) + sh2, w_gate_up[l], w_down[l])
        ks_l.append(k_ctx)
        vs_l.append(v_ctx)
        sd_l.append(s_d.astype(sdt))
        cm_l.append(c_m.astype(sdt))
        nm_l.append(n_m.astype(sdt))
        mm_l.append(m_m.astype(sdt))
        sh1, sc1, g1, sh2, sc2, g2 = ada_mod(c[:, None, :], w_mod[l], b_mod[l])
        u = rms_norm(xs, norm1_g[l]) * (1.0 + sc1) + sh1
        mix = latent_mixer(u, cache_attn_k[:, l], cache_attn_v[:, l], state_delta[:, l], state_mlstm_C[:, l],
                           state_mlstm_n[:, l], state_mlstm_m[:, l], lp, lam_init)
        xs = xs + g1 * mix
        xs = xs + g2 * swiglu(rms_norm(xs, norm2_g[l]) * (1.0 + sc2) + sh2, w_gate_up[l], w_down[l])
    y_prompt = rms_norm(xp, final_norm_g)
    y_sample = rms_norm(xs, final_norm_g)
    new_attn_k = jnp.stack(ks_l, axis=1)
    new_attn_v = jnp.stack(vs_l, axis=1)
    new_delta_S = jnp.stack(sd_l, axis=1)
    new_mlstm_C = jnp.stack(cm_l, axis=1)
    new_mlstm_n = jnp.stack(nm_l, axis=1)
    new_mlstm_m = jnp.stack(mm_l, axis=1)
    return (y_prompt, y_sample, new_attn_k, new_attn_v, new_delta_S, new_mlstm_C, new_mlstm_n, new_mlstm_m)
```

```python
import functools
import math

import jax
import jax.numpy as jnp
from jax import lax
from jax.experimental import pallas as pl
from jax.experimental.pallas import tpu as pltpu

F32 = jnp.float32
BF16 = jnp.bfloat16

D_MODEL = 1024
DEPTH = 2
GRID_W = 64
N_DIR = 2
CHUNK = 64
ROPE_BASE = 10000.0
EPS = 1e-6
HEADS = 4
A_QK_DIM = 64
A_V_DIM = 128
HEAD_DIM = 64
CONV_K = 5
FFN_HIDDEN = 2816
ROW_TILE = 256
CHUNKS_PER_TILE = ROW_TILE // CHUNK
HW = HEADS * HEAD_DIM
LANES = 128
HALO = 8

N_PROJ = 3712
VMEM_LIMIT = 56 * 1024 * 1024


def _cparams(sem):
    return pltpu.CompilerParams(dimension_semantics=sem, vmem_limit_bytes=VMEM_LIMIT)


def _bdot(a, b):
    return jnp.dot(a.astype(BF16), b.astype(BF16), preferred_element_type=F32)


def _bdot_nt(a, b):
    return lax.dot_general(a.astype(BF16), b.astype(BF16), (((1,), (1,)), ((), ())),
                           preferred_element_type=F32)


def _bdot_tn(a, b):
    return lax.dot_general(a.astype(BF16), b.astype(BF16), (((0,), (0,)), ((), ())),
                           preferred_element_type=F32)


def _dot3(a, b):
    ah = a.astype(BF16)
    al = (a - ah.astype(F32)).astype(BF16)
    bh = b.astype(BF16)
    bl = (b - bh.astype(F32)).astype(BF16)
    d = functools.partial(jnp.dot, preferred_element_type=F32)
    return d(ah, bh) + (d(ah, bl) + d(al, bh))


def _dot01(x, m01):
    hi = x.astype(BF16)
    r1 = x - hi.astype(F32)
    mid = r1.astype(BF16)
    lo = (r1 - mid.astype(F32)).astype(BF16)
    d = functools.partial(jnp.dot, preferred_element_type=F32)
    return d(hi, m01) + d(mid, m01) + d(lo, m01)


def _seg_ones(n, seg):
    r = lax.broadcasted_iota(jnp.int32, (n, n), 0) // seg
    c = lax.broadcasted_iota(jnp.int32, (n, n), 1) // seg
    return jnp.where(r == c, 1.0, 0.0).astype(BF16)


def _sigmoid(x):
    return 1.0 / (1.0 + jnp.exp(-x))


def _softplus(x):
    return jnp.maximum(x, 0.0) + jnp.log1p(jnp.exp(-jnp.abs(x)))


def _ada_kernel(c_ref, w_ref, b_ref, o_ref):
    c = c_ref[...]
    s = c * _sigmoid(c)
    o_ref[...] = _bdot(s, w_ref[...]) + b_ref[...]


def _ada_mod(cvecs, w_mod, b_mod):
    n_out = w_mod.shape[-1]
    tn = 1024
    return pl.pallas_call(
        _ada_kernel,
        grid=(DEPTH, n_out // tn),
        in_specs=[pl.BlockSpec((8, D_MODEL), lambda l, j: (0, 0)),
                  pl.BlockSpec((None, D_MODEL, tn), lambda l, j: (l, 0, j)),
                  pl.BlockSpec((None, 1, tn), lambda l, j: (l, 0, j))],
        out_specs=pl.BlockSpec((None, 8, tn), lambda l, j: (l, 0, j)),
        out_shape=jax.ShapeDtypeStruct((DEPTH, 8, n_out), F32),
        compiler_params=_cparams(("parallel", "parallel")),
        name="ada_mod",
    )(cvecs, w_mod, b_mod.reshape(DEPTH, 1, n_out))


def _proj_kernel(x_ref, mod_ref, g_ref, w_ref, b_ref, cos_ref, sin_ref,
                 qk_ref, va_ref, kv32_ref, bqkv_ref, bgc_ref, gates_ref):
    x = x_ref[...]
    y = x * lax.rsqrt(jnp.mean(x * x, axis=-1, keepdims=True) + EPS) * g_ref[...]
    u = y * (1.0 + mod_ref[:, D_MODEL:2 * D_MODEL]) + mod_ref[:, 0:D_MODEL]
    acc = _bdot(u, w_ref[...]) + b_ref[...]
    kv32_ref[...] = acc[:, 512:1536]
    va_ref[...] = acc[:, 1024:1536].astype(BF16)
    bqkv_ref[...] = acc[:, 1536:2304]
    bgc_ref[...] = acc[:, 2304:3584]
    gates_ref[...] = acc[:, 3584:3712]
    cos = cos_ref[...]
    sin = sin_ref[...]
    even = (lax.broadcasted_iota(jnp.int32, cos.shape, 1) % 2) == 0
    for j in range(8):
        xj = acc[:, j * LANES:(j + 1) * LANES]
        swapped = jnp.where(even, pltpu.roll(xj, LANES - 1, 1), pltpu.roll(xj, 1, 1))
        r = xj * cos + swapped * sin
        if j < 4:
            r = r * (A_QK_DIM ** -0.5)
        qk_ref[:, j * LANES:(j + 1) * LANES] = r.astype(BF16)


def _in_proj(x_all, mod_l, g1, w_p, b_p, cos_t, sin_t, n_ctx_tiles, tiles_per_seq):
    t_rows = x_all.shape[0]
    n_tiles = t_rows // ROW_TILE

    def mod_idx(i):
        return (jnp.where(i < n_ctx_tiles, 0, 1 + (i - n_ctx_tiles) // tiles_per_seq), 0, 0)

    def rope_idx(i):
        return (jnp.where(i < n_ctx_tiles, 0, 1 + (i - n_ctx_tiles) % tiles_per_seq), 0)

    row = lambda w: pl.BlockSpec((ROW_TILE, w), lambda i: (i, 0))
    outs = [(1024, BF16), (512, BF16), (1024, F32), (768, F32), (1280, F32), (LANES, F32)]
    return pl.pallas_call(
        _proj_kernel,
        grid=(n_tiles,),
        in_specs=[row(D_MODEL),
                  pl.BlockSpec((None, 1, 6 * D_MODEL), mod_idx),
                  pl.BlockSpec((1, D_MODEL), lambda i: (0, 0)),
                  pl.BlockSpec((D_MODEL, N_PROJ), lambda i: (0, 0)),
                  pl.BlockSpec((1, N_PROJ), lambda i: (0, 0)),
                  pl.BlockSpec((ROW_TILE, LANES), rope_idx),
                  pl.BlockSpec((ROW_TILE, LANES), rope_idx)],
        out_specs=[row(w) for w, _ in outs],
        out_shape=[jax.ShapeDtypeStruct((t_rows, w), dt) for w, dt in outs],
        compiler_params=_cparams(("parallel",)),
        name="in_proj",
    )(x_all, mod_l.reshape(8, 1, 6 * D_MODEL), g1.reshape(1, D_MODEL), w_p, b_p.reshape(1, N_PROJ),
      cos_t, sin_t)


def _gates_kernel(g_ref, par_ref, o_ref):
    x = g_ref[...]
    alog = par_ref[0:8, 0:1]
    dtb = par_ref[0:8, 1:2]
    fb = par_ref[24:32, 1:2]
    g = -jnp.exp(alog) * _softplus(x[0:8] + dtb)
    beta = _sigmoid(x[8:16])
    ig = x[16:24]
    lf = -_softplus(-(x[24:32] + fb))
    tl = x.shape[1]
    r = lax.broadcasted_iota(jnp.int32, (LANES, LANES), 0)
    c = lax.broadcasted_iota(jnp.int32, (LANES, LANES), 1)
    same = (r // CHUNK) == (c // CHUNK)
    pre = jnp.where(same & (r <= c), 1.0, 0.0).astype(BF16)
    suf = jnp.where(same & (r >= c), 1.0, 0.0).astype(BF16)
    tot = jnp.where(same, 1.0, 0.0).astype(BF16)
    m01 = jnp.concatenate([pre, suf, tot], axis=1)
    fwd = (lax.broadcasted_iota(jnp.int32, (16, LANES), 0) % 8) < 4
    o_ref[16:24, :] = beta
    o_ref[24:32, :] = ig
    for j in range(tl // LANES):
        sl = slice(j * LANES, (j + 1) * LANES)
        xs = jnp.concatenate([g[:, sl], lf[:, sl]], axis=0)
        cs = _dot01(xs, m01)
        cum = jnp.where(fwd, cs[:, 0:LANES], cs[:, LANES:2 * LANES])
        total = cs[:, 2 * LANES:3 * LANES]
        o_ref[0:8, sl] = cum[0:8]
        o_ref[8:16, sl] = total[0:8]
        o_ref[32:40, sl] = cum[8:16]
        o_ref[40:48, sl] = total[8:16]


def _gate_prep(gates_t, par):
    t_rows = gates_t.shape[1]
    tl = math.gcd(t_rows, 2048)
    return pl.pallas_call(
        _gates_kernel,
        grid=(t_rows // tl,),
        in_specs=[pl.BlockSpec((32, tl), lambda i: (0, i)),
                  pl.BlockSpec((32, LANES), lambda i: (0, 0))],
        out_specs=pl.BlockSpec((48, tl), lambda i: (0, i)),
        out_shape=jax.ShapeDtypeStruct((48, t_rows), F32),
        compiler_params=_cparams(("parallel",)),
        name="gate_prep",
    )(gates_t, par)


def _conv_kernel(x_ref, p_ref, n_ref, w_ref, o_ref, ext_s, *, n_ctx_tiles, tiles_per_seq):
    i = pl.program_id(0)
    j = (i - n_ctx_tiles) % tiles_per_seq
    is_ctx = i < n_ctx_tiles
    first = jnp.logical_or(is_ctx, j == 0)
    last = jnp.logical_or(is_ctx, j == tiles_per_seq - 1)
    ext_s[0:HALO, :] = jnp.where(first, 0.0, p_ref[...])
    ext_s[HALO:HALO + ROW_TILE, :] = x_ref[...]
    ext_s[HALO + ROW_TILE:, :] = jnp.where(last, 0.0, n_ref[...])
    w = w_ref[...]
    y = None
    for k in range(CONV_K):
        off = HALO - CONV_K // 2 + k
        term = ext_s[off:off + ROW_TILE, :] * w[k:k + 1, :]
        y = term if y is None else y + term
    y = y * _sigmoid(y)
    ones = _seg_ones(HW, HEAD_DIM)
    q = y[:, 0:HW]
    k_ = y[:, HW:2 * HW]
    o_ref[:, 0:HW] = q * lax.rsqrt(_dot01(q * q, ones) + EPS) * (HEAD_DIM ** -0.5)
    o_ref[:, HW:2 * HW] = k_ * lax.rsqrt(_dot01(k_ * k_, ones) + EPS)
    o_ref[:, 2 * HW:3 * HW] = y[:, 2 * HW:3 * HW]


def _delta_conv(bqkv, conv_w, n_ctx_tiles, tiles_per_seq):
    t_rows, width = bqkv.shape
    n_tiles = t_rows // ROW_TILE
    hb = ROW_TILE // HALO
    n_hb = t_rows // HALO
    return pl.pallas_call(
        functools.partial(_conv_kernel, n_ctx_tiles=n_ctx_tiles, tiles_per_seq=tiles_per_seq),
        grid=(n_tiles,),
        in_specs=[pl.BlockSpec((ROW_TILE, width), lambda i: (i, 0)),
                  pl.BlockSpec((HALO, width), lambda i: (jnp.maximum(i * hb - 1, 0), 0)),
                  pl.BlockSpec((HALO, width), lambda i: (jnp.minimum((i + 1) * hb, n_hb - 1), 0)),
                  pl.BlockSpec((CONV_K, width), lambda i: (0, 0))],
        out_specs=pl.BlockSpec((ROW_TILE, width), lambda i: (i, 0)),
        out_shape=jax.ShapeDtypeStruct((t_rows, width), F32),
        scratch_shapes=[pltpu.VMEM((ROW_TILE + 2 * HALO, width), F32)],
        compiler_params=_cparams(("parallel",)),
        name="delta_conv",
    )(bqkv, bqkv, bqkv, conv_w)


def _attn_kernel(lqk_ref, q_ref, k_ref, v_ref, o_ref, q2_s, m_s, l_s, acc_s, *, lam_init, tq):
    kt = pl.program_id(3)

    @pl.when(kt == 0)
    def _():
        q = q_ref[...]
        lane = lax.broadcasted_iota(jnp.int32, q.shape, 1)
        zero = jnp.zeros_like(q)
        q2_s[0:tq, :] = jnp.where(lane < A_QK_DIM, q, zero)
        q2_s[tq:2 * tq, :] = jnp.where(lane >= A_QK_DIM, q, zero)
        m_s[...] = jnp.full(m_s.shape, -jnp.inf, F32)
        l_s[...] = jnp.zeros(l_s.shape, F32)
        acc_s[...] = jnp.zeros(acc_s.shape, F32)

    s = lax.dot_general(q2_s[...], k_ref[...], (((1,), (1,)), ((), ())), preferred_element_type=F32)
    m_prev = m_s[...]
    m_new = jnp.maximum(m_prev, jnp.max(s, axis=-1, keepdims=True))
    alpha = jnp.exp(m_prev - m_new)
    p = jnp.exp(s - m_new)
    l_s[...] = alpha * l_s[...] + jnp.sum(p, axis=-1, keepdims=True)
    acc_s[...] = alpha * acc_s[...] + jnp.dot(p.astype(BF16), v_ref[...], preferred_element_type=F32)
    m_s[...] = m_new

    @pl.when(kt == pl.num_programs(3) - 1)
    def _():
        o = acc_s[...] / l_s[...]
        lq = lqk_ref[...]
        lam = (jnp.exp(jnp.sum(lq[0:1] * lq[1:2], axis=-1, keepdims=True))
               - jnp.exp(jnp.sum(lq[2:3] * lq[3:4], axis=-1, keepdims=True)) + lam_init)
        o_ref[...] = o[0:tq] - lam * o[tq:2 * tq]


def _diff_attention(lqk, qk, k_all, v_all, lam_init, q_row0, n_seq, lq, tq, tk):
    lk = k_all.shape[1]
    qb0 = q_row0 // tq
    nq = lq // tq
    return pl.pallas_call(
        functools.partial(_attn_kernel, lam_init=lam_init, tq=tq),
        grid=(n_seq, HEADS, nq, lk // tk),
        in_specs=[pl.BlockSpec((4, A_QK_DIM), lambda b, h, i, j: (0, 0)),
                  pl.BlockSpec((tq, LANES), lambda b, h, i, j: (qb0 + b * nq + i, h)),
                  pl.BlockSpec((None, tk, LANES), lambda b, h, i, j: (b, j, h)),
                  pl.BlockSpec((None, tk, LANES), lambda b, h, i, j: (b, j, h))],
        out_specs=pl.BlockSpec((tq, LANES), lambda b, h, i, j: (b * nq + i, h)),
        out_shape=jax.ShapeDtypeStruct((n_seq * lq, HEADS * A_V_DIM), F32),
        scratch_shapes=[pltpu.VMEM((2 * tq, LANES), BF16),
                        pltpu.VMEM((2 * tq, 1), F32),
                        pltpu.VMEM((2 * tq, 1), F32),
                        pltpu.VMEM((2 * tq, LANES), F32)],
        compiler_params=_cparams(("parallel", "parallel", "parallel", "arbitrary")),
        name="diff_attention",
    )(lqk, qk, k_all, v_all)


def _stack4(x):
    return jnp.concatenate([x, x, x, x], axis=0)


def _collapse(x):
    return (x[0:CHUNK] + x[CHUNK:2 * CHUNK]) + (x[2 * CHUNK:3 * CHUNK] + x[3 * CHUNK:4 * CHUNK])


def _block_masks(direction):
    r = lax.broadcasted_iota(jnp.int32, (HW, HW), 0)
    c = lax.broadcasted_iota(jnp.int32, (HW, HW), 1)
    same = (r // CHUNK) == (c // CHUNK)
    diff = ((r % CHUNK) - (c % CHUNK)) * (1 - 2 * direction)
    return same, jnp.logical_and(same, diff >= 0), jnp.logical_and(same, diff > 0)


def _delta_kernel(q_ref, k_ref, v_ref, col_ref, row_ref, s0_ref, o_ref, sout_ref, s_s):
    direction = pl.program_id(1)
    t = pl.program_id(2)

    @pl.when(t == 0)
    def _():
        s_s[...] = s0_ref[...]

    same, incl, strict = _block_masks(direction)
    eye = (lax.broadcasted_iota(jnp.int32, (HW, HW), 0) == lax.broadcasted_iota(jnp.int32, (HW, HW), 1))
    pre = []
    for c in range(CHUNKS_PER_TILE):
        rows = slice(c * CHUNK, (c + 1) * CHUNK)
        q4 = _stack4(q_ref[rows, :])
        k4 = _stack4(k_ref[rows, :])
        v4 = _stack4(v_ref[rows, :])
        col = col_ref[c]
        gc, gtot, beta = col[:, 0:1], col[:, 1:2], col[:, 2:3]
        gc_row = row_ref[c][0:1, :]
        kbm = jnp.where(same, k4 * beta, 0.0)
        qm = jnp.where(same, q4, 0.0)
        kk = _bdot_nt(jnp.concatenate([kbm, qm], axis=0), k4)
        e_incl = jnp.exp(jnp.where(incl, gc - gc_row, -jnp.inf))
        a = jnp.where(strict, kk[0:HW] * e_incl, 0.0)
        aqk = kk[HW:2 * HW] * e_incl
        tinv = jnp.where(eye, 1.0, 0.0) - a
        x = a
        for _ in range(5):
            x = _dot3(x, x)
            tinv = tinv + _dot3(tinv, x)
        egc = jnp.exp(gc)
        vbm = jnp.where(same, v4 * beta, 0.0)
        uw = _bdot(tinv, jnp.concatenate([vbm, kbm * egc], axis=1))
        u = _collapse(uw[:, 0:HW])
        w = _collapse(uw[:, HW:2 * HW])
        qd = _collapse(qm * egc)
        kd = _collapse(jnp.where(same, k4 * jnp.exp(gtot - gc), 0.0))
        pre.append((u, w, qd, kd, aqk, jnp.exp(gtot)))

    def scan(order):
        for c in order:
            u, w, qd, kd, aqk, gl = pre[c]
            s = s_s[...]
            wq = _bdot(jnp.concatenate([w, qd], axis=0), s)
            v_new = u - wq[0:CHUNK]
            vn4 = jnp.where(same, _stack4(v_new), 0.0)
            o_ref[c * CHUNK:(c + 1) * CHUNK, :] = wq[CHUNK:2 * CHUNK] + _collapse(_bdot(aqk, vn4))
            s_s[...] = s * gl + jnp.where(same, _bdot_tn(kd, v_new), 0.0)

    @pl.when(direction == 0)
    def _():
        scan(range(CHUNKS_PER_TILE))

    @pl.when(direction == 1)
    def _():
        scan(range(CHUNKS_PER_TILE - 1, -1, -1))

    @pl.when(t == pl.num_programs(2) - 1)
    def _():
        sout_ref[...] = s_s[...]


def _mlstm_kernel(q_ref, k_ref, v_ref, col_ref, row_ref, cn0_ref, m0_ref,
                  o_ref, cnout_ref, mout_ref, cn_s, m_s):
    direction = pl.program_id(1)
    t = pl.program_id(2)

    @pl.when(t == 0)
    def _():
        cn_s[...] = cn0_ref[...]
        m_s[...] = m0_ref[...]

    same, incl, _ = _block_masks(direction)
    pre = []
    for c in range(CHUNKS_PER_TILE):
        rows = slice(c * CHUNK, (c + 1) * CHUNK)
        q4 = _stack4(q_ref[rows, :])
        k4 = _stack4(k_ref[rows, :]) * (HEAD_DIM ** -0.5)
        v4 = _stack4(v_ref[rows, :])
        col = col_ref[c]
        ig, b, btot = col[:, 3:4], col[:, 4:5], col[:, 5:6]
        rw = row_ref[c]
        r_row = rw[3:4, :] - rw[4:5, :]
        qm = jnp.where(same, q4, 0.0)
        dmat = jnp.where(incl, b + r_row, -jnp.inf)
        dmax = jnp.max(dmat, axis=-1, keepdims=True)
        qk = _bdot_nt(qm, k4)
        kvl = btot - b + ig
        mkv = jnp.concatenate(
            [jnp.broadcast_to(jnp.max(kvl[h * CHUNK:(h + 1) * CHUNK], axis=0, keepdims=True), (CHUNK, 1))
             for h in range(HEADS)], axis=0)
        vaug = jnp.concatenate([jnp.where(same, v4, 0.0),
                                jnp.where(_head_cols(), 1.0, 0.0)], axis=1)
        pre.append((qm, k4, vaug, b, btot, dmat, dmax, qk, kvl, mkv))

    def scan(order):
        for c in order:
            qm, k4, vaug, b, btot, dmat, dmax, qk, kvl, mkv = pre[c]
            cn = cn_s[...]
            ms = m_s[...]
            inter = b + ms
            m_t = jnp.maximum(inter, dmax)
            s = qk * jnp.exp(dmat - m_t)
            w_inter = jnp.exp(inter - m_t)
            nd = w_inter * _bdot(qm, cn) + _bdot(s, vaug)
            den = jnp.sum(nd[:, HW:HW + LANES], axis=-1, keepdims=True)
            hval = nd[:, 0:HW] / jnp.maximum(jnp.abs(den), jnp.exp(-m_t))
            o_ref[c * CHUNK:(c + 1) * CHUNK, :] = _collapse(hval)
            m_new = jnp.maximum(btot + ms, mkv)
            kw = jnp.where(same, k4 * jnp.exp(kvl - m_new), 0.0)
            cn_s[...] = jnp.exp(btot + ms - m_new) * cn + _bdot_tn(kw, vaug)
            m_s[...] = m_new

    @pl.when(direction == 0)
    def _():
        scan(range(CHUNKS_PER_TILE))

    @pl.when(direction == 1)
    def _():
        scan(range(CHUNKS_PER_TILE - 1, -1, -1))

    @pl.when(t == pl.num_programs(2) - 1)
    def _():
        cnout_ref[...] = cn_s[...]
        mout_ref[...] = jnp.broadcast_to(m_s[...], mout_ref.shape)


def _head_cols():
    r = lax.broadcasted_iota(jnp.int32, (HW, LANES), 0) // CHUNK
    c = lax.broadcasted_iota(jnp.int32, (HW, LANES), 1)
    return r == c


def _scan_specs(tile0, n_tiles, col_blocks):
    def tile_idx(s, d, t):
        return tile0 + s * n_tiles + jnp.where(d == 0, t, n_tiles - 1 - t)

    qkv = [pl.BlockSpec((ROW_TILE, HW), functools.partial(lambda s, d, t, cb: (tile_idx(s, d, t), cb), cb=cb))
           for cb in col_blocks]
    col = pl.BlockSpec((None, CHUNKS_PER_TILE, HW, 8), lambda s, d, t: (d, tile_idx(s, d, t), 0, 0))
    row = pl.BlockSpec((None, CHUNKS_PER_TILE, 8, HW), lambda s, d, t: (d, tile_idx(s, d, t), 0, 0))
    out = pl.BlockSpec((None, ROW_TILE, HW), lambda s, d, t: (d, s * n_tiles + jnp.where(d == 0, t, n_tiles - 1 - t), 0))
    return qkv, col, row, out


def _delta_scan(dqkv, colg, rowg, s0, tile0, n_seq, n_tiles):
    qkv, col, row, out = _scan_specs(tile0, n_tiles, (0, 1, 2))
    st = pl.BlockSpec((None, None, HW, HW), lambda s, d, t: (s, d, 0, 0))
    return pl.pallas_call(
        _delta_kernel,
        grid=(n_seq, N_DIR, n_tiles),
        in_specs=qkv + [col, row, st],
        out_specs=[out, st],
        out_shape=[jax.ShapeDtypeStruct((N_DIR, n_seq * n_tiles * ROW_TILE, HW), F32),
                   jax.ShapeDtypeStruct((n_seq, N_DIR, HW, HW), F32)],
        scratch_shapes=[pltpu.VMEM((HW, HW), F32)],
        compiler_params=_cparams(("parallel", "parallel", "arbitrary")),
        name="delta_scan",
    )(dqkv, dqkv, dqkv, colg, rowg, s0)


def _mlstm_scan(bgc, colg, rowg, cn0, m0, tile0, n_seq, n_tiles):
    qkv, col, row, out = _scan_specs(tile0, n_tiles, (1, 2, 3))
    cn = pl.BlockSpec((None, None, HW, HW + LANES), lambda s, d, t: (s, d, 0, 0))
    mm = pl.BlockSpec((None, None, HW, LANES), lambda s, d, t: (s, d, 0, 0))
    m_in = pl.BlockSpec((None, None, HW, 1), lambda s, d, t: (s, d, 0, 0))
    return pl.pallas_call(
        _mlstm_kernel,
        grid=(n_seq, N_DIR, n_tiles),
        in_specs=qkv + [col, row, cn, m_in],
        out_specs=[out, cn, mm],
        out_shape=[jax.ShapeDtypeStruct((N_DIR, n_seq * n_tiles * ROW_TILE, HW), F32),
                   jax.ShapeDtypeStruct((n_seq, N_DIR, HW, HW + LANES), F32),
                   jax.ShapeDtypeStruct((n_seq, N_DIR, HW, LANES), F32)],
        scratch_shapes=[pltpu.VMEM((HW, HW + LANES), F32), pltpu.VMEM((HW, 1), F32)],
        compiler_params=_cparams(("parallel", "parallel", "arbitrary")),
        name="mlstm_scan",
    )(bgc, bgc, bgc, colg, rowg, cn0, m0)


def _merge_kernel(x_ref, mod_ref, a_ref, od_ref, hm_ref, bg_ref, co_ref, ag_ref, dg_ref, mg_ref, w_ref,
                  o_ref, *, lam_init):
    a = a_ref[...]
    a = a * lax.rsqrt(_dot01(a * a, _seg_ones(HEADS * A_V_DIM, A_V_DIM)) * (1.0 / A_V_DIM) + EPS)
    a = a * ag_ref[...] * (1.0 - lam_init)
    ones = _seg_ones(HW, HEAD_DIM)
    od = od_ref[0] + od_ref[1]
    bg = bg_ref[...]
    d = od * lax.rsqrt(_dot01(od * od, ones) * (1.0 / HEAD_DIM) + EPS) * dg_ref[...] * (bg * _sigmoid(bg))
    hm = hm_ref[0] + hm_ref[1]
    m = _sigmoid(co_ref[...]) * (hm * lax.rsqrt(_dot01(hm * hm, ones) * (1.0 / HEAD_DIM) + EPS) * mg_ref[...])
    cat = jnp.concatenate([a.astype(BF16), d.astype(BF16), m.astype(BF16)], axis=1)
    mix = jnp.dot(cat, w_ref[...], preferred_element_type=F32)
    o_ref[...] = x_ref[...] + mod_ref[:, 2 * D_MODEL:3 * D_MODEL] * mix


def _merge(x_all, mod_l, a_out, o_delta, h_m, bgc, attn_g, delta_g, mlstm_g, w_out, lam_init,
           n_ctx_tiles, tiles_per_seq):
    t_rows = x_all.shape[0]

    def mod_idx(i):
        return (jnp.where(i < n_ctx_tiles, 0, 1 + (i - n_ctx_tiles) // tiles_per_seq), 0, 0)

    full = lambda r, c: pl.BlockSpec((r, c), lambda i: (0, 0))
    return pl.pallas_call(
        functools.partial(_merge_kernel, lam_init=lam_init),
        grid=(t_rows // ROW_TILE,),
        in_specs=[pl.BlockSpec((ROW_TILE, D_MODEL), lambda i: (i, 0)),
                  pl.BlockSpec((None, 1, 6 * D_MODEL), mod_idx),
                  pl.BlockSpec((ROW_TILE, HEADS * A_V_DIM), lambda i: (i, 0)),
                  pl.BlockSpec((N_DIR, ROW_TILE, HW), lambda i: (0, i, 0)),
                  pl.BlockSpec((N_DIR, ROW_TILE, HW), lambda i: (0, i, 0)),
                  pl.BlockSpec((ROW_TILE, HW), lambda i: (i, 0)),
                  pl.BlockSpec((ROW_TILE, HW), lambda i: (i, 4)),
                  full(1, HEADS * A_V_DIM), full(1, HW), full(1, HW),
                  full(D_MODEL, D_MODEL)],
        out_specs=pl.BlockSpec((ROW_TILE, D_MODEL), lambda i: (i, 0)),
        out_shape=jax.ShapeDtypeStruct((t_rows, D_MODEL), F32),
        compiler_params=_cparams(("parallel",)),
        name="merge_out_proj",
    )(x_all, mod_l.reshape(8, 1, 6 * D_MODEL), a_out, o_delta, h_m, bgc, bgc,
      jnp.tile(attn_g, HEADS).reshape(1, -1), jnp.tile(delta_g, HEADS).reshape(1, -1),
      jnp.tile(mlstm_g, HEADS).reshape(1, -1), w_out)


def _ffn_kernel(x_ref, mod_ref, g_ref, wgu_ref, wdn_ref, fg_ref, o_ref, *, final):
    x = x_ref[...]
    y = x * lax.rsqrt(jnp.mean(x * x, axis=-1, keepdims=True) + EPS) * g_ref[...]
    u = y * (1.0 + mod_ref[:, 4 * D_MODEL:5 * D_MODEL]) + mod_ref[:, 3 * D_MODEL:4 * D_MODEL]
    h = jnp.dot(u.astype(BF16), wgu_ref[...], preferred_element_type=F32)
    gate = h[:, 0:FFN_HIDDEN]
    act = (gate * _sigmoid(gate)) * h[:, FFN_HIDDEN:2 * FFN_HIDDEN]
    out = x + mod_ref[:, 5 * D_MODEL:6 * D_MODEL] * jnp.dot(act.astype(BF16), wdn_ref[...],
                                                             preferred_element_type=F32)
    if final:
        out = out * lax.rsqrt(jnp.mean(out * out, axis=-1, keepdims=True) + EPS) * fg_ref[...]
    o_ref[...] = out


def _ffn(x_all, mod_l, g2, w_gu, w_dn, final_g, final, n_ctx_tiles, tiles_per_seq):
    t_rows = x_all.shape[0]

    def mod_idx(i):
        return (jnp.where(i < n_ctx_tiles, 0, 1 + (i - n_ctx_tiles) // tiles_per_seq), 0, 0)

    full = lambda r, c: pl.BlockSpec((r, c), lambda i: (0, 0))
    return pl.pallas_call(
        functools.partial(_ffn_kernel, final=final),
        grid=(t_rows // ROW_TILE,),
        in_specs=[pl.BlockSpec((ROW_TILE, D_MODEL), lambda i: (i, 0)),
                  pl.BlockSpec((None, 1, 6 * D_MODEL), mod_idx),
                  full(1, D_MODEL), full(D_MODEL, 2 * FFN_HIDDEN), full(FFN_HIDDEN, D_MODEL),
                  full(1, D_MODEL)],
        out_specs=pl.BlockSpec((ROW_TILE, D_MODEL), lambda i: (i, 0)),
        out_shape=jax.ShapeDtypeStruct((t_rows, D_MODEL), F32),
        compiler_params=_cparams(("parallel",)),
        name="ffn",
    )(x_all, mod_l.reshape(8, 1, 6 * D_MODEL), g2.reshape(1, D_MODEL), w_gu, w_dn,
      final_g.reshape(1, D_MODEL))


def _rope_tables(dec_seq):
    n_rows = dec_seq // GRID_W
    rows = jnp.repeat(jnp.arange(n_rows, dtype=F32), GRID_W)
    cols = jnp.tile(jnp.arange(GRID_W, dtype=F32), n_rows)
    n_freq = A_QK_DIM // 4
    inv_freq = ROPE_BASE ** (-jnp.arange(n_freq, dtype=F32) / n_freq)
    ang = jnp.concatenate([rows[:, None] * inv_freq, cols[:, None] * inv_freq], axis=-1)
    cos = jnp.repeat(jnp.cos(ang), 2, axis=-1)
    sin = jnp.repeat(jnp.sin(ang), 2, axis=-1) * jnp.tile(jnp.array([-1.0, 1.0], F32), A_QK_DIM // 2)
    cos = jnp.concatenate([jnp.ones((ROW_TILE, A_QK_DIM), F32), cos], axis=0)
    sin = jnp.concatenate([jnp.zeros((ROW_TILE, A_QK_DIM), F32), sin], axis=0)
    return jnp.tile(cos, (1, 2)), jnp.tile(sin, (1, 2))


def _permute_proj(w_in_l, b_in_l):
    def perm(a):
        head, ba_bb, tail, ci_cf = a[..., 0:2560], a[..., 2560:2576], a[..., 2576:3600], a[..., 3600:3616]
        pad = jnp.zeros(a.shape[:-1] + (N_PROJ - 3616,), a.dtype)
        return jnp.concatenate([head, tail, ba_bb, ci_cf, pad], axis=-1)
    return perm(w_in_l).astype(BF16), perm(b_in_l)


def _block_diag(s):
    eye = jnp.eye(HEADS, dtype=s.dtype)
    out = s[..., :, :, None, :] * eye[:, None, :, None]
    return out.reshape(s.shape[:-3] + (HW, HW))


def _block_diag_inv(s_bd):
    s6 = s_bd.reshape(s_bd.shape[:-2] + (HEADS, HEAD_DIM, HEADS, HEAD_DIM))
    return jnp.stack([s6[..., h, :, h, :] for h in range(HEADS)], axis=-3)


def _norm_cols(n):
    eye = jnp.eye(HEADS, LANES, dtype=n.dtype)
    return (n[..., None] * eye[:, None, :]).reshape(n.shape[:-2] + (HW, LANES))


def kernel(x_prompt, x_sample, cache_attn_k, cache_attn_v, state_delta, state_mlstm_C, state_mlstm_n,
           state_mlstm_m, c, c_ctx, norm1_g, norm2_g, w_mod, b_mod, w_in, b_in, w_out, lambda_qk,
           attn_subln_g, delta_conv_w, delta_A_log, delta_dt_bias, delta_norm_g, mlstm_f_bias,
           mlstm_norm_g, w_gate_up, w_down, final_norm_g):
    batch, seq, _ = x_prompt.shape
    dec_batch, dec_seq, _ = x_sample.shape
    past_len = cache_attn_k.shape[2]
    assert seq == ROW_TILE and dec_seq % ROW_TILE == 0 and dec_batch + 1 <= 8
    t_ctx = batch * seq
    t_lat = dec_batch * dec_seq
    t_rows = t_ctx + t_lat
    n_ctx_tiles = t_ctx // ROW_TILE
    tiles_per_seq = dec_seq // ROW_TILE
    n_chunks = t_rows // CHUNK

    x_all = jnp.concatenate([x_prompt.reshape(t_ctx, D_MODEL), x_sample.reshape(t_lat, D_MODEL)], axis=0)
    cvecs = jnp.zeros((8, D_MODEL), F32).at[0].set(c_ctx).at[1:1 + dec_batch].set(c)
    mod = _ada_mod(cvecs, w_mod, b_mod)
    cos_t, sin_t = _rope_tables(dec_seq)

    ks_l, vs_l, sd_l, cm_l, nm_l, mm_l = [], [], [], [], [], []
    for l in range(DEPTH):
        lam_init = 0.8 - 0.6 * math.exp(-0.3 * l)
        w_p, b_p = _permute_proj(w_in[l], b_in[l])
        qk, va, kv32, bqkv, bgc, gates = _in_proj(x_all, mod[l], norm1_g[l], w_p, b_p, cos_t, sin_t,
                                                 n_ctx_tiles, tiles_per_seq)
        ks_l.append(kv32[:t_ctx, 0:512].reshape(batch, seq, HEADS, 2, A_QK_DIM))
        vs_l.append(kv32[:t_ctx, 512:1024].reshape(batch, seq, HEADS, A_V_DIM))

        par = jnp.zeros((32, LANES), F32)
        par = par.at[0:8, 0].set(delta_A_log[l].reshape(-1)).at[0:8, 1].set(delta_dt_bias[l].reshape(-1))
        par = par.at[24:32, 1].set(mlstm_f_bias[l].reshape(-1))
        grow = _gate_prep(gates[:, 0:32].T, par)
        r6 = grow.reshape(6, N_DIR, HEADS, n_chunks, CHUNK)
        colg = jnp.pad(r6.transpose(1, 3, 2, 4, 0).reshape(N_DIR, n_chunks, HW, 6),
                       ((0, 0), (0, 0), (0, 0), (0, 2)))
        rowg = jnp.pad(r6.transpose(1, 3, 0, 2, 4).reshape(N_DIR, n_chunks, 6, HW),
                       ((0, 0), (0, 0), (0, 2), (0, 0)))

        k_ctx = qk[:t_ctx, 512:1024].reshape(batch, seq, 512)
        v_ctx = va[:t_ctx].reshape(batch, seq, 512)
        a_ctx = _diff_attention(lambda_qk[l], qk, k_ctx, v_ctx, lam_init, 0, batch, seq, seq, seq)
        k_lat = jnp.concatenate([qk[t_ctx:, 512:1024].reshape(dec_batch, dec_seq, 512),
                                 cache_attn_k[:, l].reshape(dec_batch, past_len, 512).astype(BF16)], axis=1)
        v_lat = jnp.concatenate([va[t_ctx:].reshape(dec_batch, dec_seq, 512),
                                 cache_attn_v[:, l].reshape(dec_batch, past_len, 512).astype(BF16)], axis=1)
        a_lat = _diff_attention(lambda_qk[l], qk, k_lat, v_lat, lam_init, t_ctx, dec_batch, dec_seq, 512, 512)
        a_out = jnp.concatenate([a_ctx, a_lat], axis=0)

        dqkv = _delta_conv(bqkv, delta_conv_w[l], n_ctx_tiles, tiles_per_seq)
        od_ctx, s_ctx = _delta_scan(dqkv, colg, rowg, jnp.zeros((batch, N_DIR, HW, HW), F32),
                                    0, batch, 1)
        od_lat, _ = _delta_scan(dqkv, colg, rowg, _block_diag(state_delta[:, l]),
                                n_ctx_tiles, dec_batch, tiles_per_seq)
        o_delta = jnp.concatenate([od_ctx, od_lat], axis=1)
        sd_l.append(_block_diag_inv(s_ctx))

        cn_lat = jnp.concatenate([_block_diag(state_mlstm_C[:, l]), _norm_cols(state_mlstm_n[:, l])], axis=-1)
        m_lat = jnp.repeat(state_mlstm_m[:, l], HEAD_DIM, axis=-1)[..., None]
        hm_ctx, cn_ctx, m_ctx = _mlstm_scan(bgc, colg, rowg, jnp.zeros((batch, N_DIR, HW, HW + LANES), F32),
                                            jnp.zeros((batch, N_DIR, HW, 1), F32), 0, batch, 1)
        hm_lat, _, _ = _mlstm_scan(bgc, colg, rowg, cn_lat, m_lat, n_ctx_tiles, dec_batch, tiles_per_seq)
        h_m = jnp.concatenate([hm_ctx, hm_lat], axis=1)
        cm_l.append(_block_diag_inv(cn_ctx[..., 0:HW]))
        n6 = cn_ctx[..., HW:HW + HEADS].reshape(batch, N_DIR, HEADS, HEAD_DIM, HEADS)
        nm_l.append(jnp.stack([n6[..., h, :, h] for h in range(HEADS)], axis=-2))
        mm_l.append(m_ctx[:, :, ::HEAD_DIM, 0])

        x_all = _merge(x_all, mod[l], a_out, o_delta, h_m, bgc, attn_subln_g[l], delta_norm_g[l],
                       mlstm_norm_g[l], w_out[l].astype(BF16), lam_init, n_ctx_tiles, tiles_per_seq)
        x_all = _ffn(x_all, mod[l], norm2_g[l], w_gate_up[l].astype(BF16), w_down[l].astype(BF16),
                     final_norm_g, l == DEPTH - 1, n_ctx_tiles, tiles_per_seq)

    y_prompt = x_all[:t_ctx].reshape(batch, seq, D_MODEL)
    y_sample = x_all[t_ctx:].reshape(dec_batch, dec_seq, D_MODEL)
    return (y_prompt, y_sample, jnp.stack(ks_l, axis=1), jnp.stack(vs_l, axis=1), jnp.stack(sd_l, axis=1),
            jnp.stack(cm_l, axis=1), jnp.stack(nm_l, axis=1), jnp.stack(mm_l, axis=1))
```

```python
import functools
import math

import jax
import jax.numpy as jnp
from jax import lax
from jax.experimental import pallas as pl
from jax.experimental.pallas import tpu as pltpu

F32 = jnp.float32
BF16 = jnp.bfloat16

D_MODEL = 1024
DEPTH = 2
GRID_W = 64
N_DIR = 2
CHUNK = 64
ROPE_BASE = 10000.0
EPS = 1e-6
HEADS = 4
A_QK_DIM = 64
A_V_DIM = 128
HEAD_DIM = 64
CONV_K = 5
FFN_HIDDEN = 2816
ROW_TILE = 256
CHUNKS_PER_TILE = ROW_TILE // CHUNK
HW = HEADS * HEAD_DIM
LANES = 128
HALO = 8

N_PROJ = 3712
VMEM_LIMIT = 56 * 1024 * 1024


def _cparams(sem):
    return pltpu.CompilerParams(dimension_semantics=sem, vmem_limit_bytes=VMEM_LIMIT)


def _bdot(a, b):
    return jnp.dot(a.astype(BF16), b.astype(BF16), preferred_element_type=F32)


def _bdot_nt(a, b):
    return lax.dot_general(a.astype(BF16), b.astype(BF16), (((1,), (1,)), ((), ())),
                           preferred_element_type=F32)


def _bdot_tn(a, b):
    return lax.dot_general(a.astype(BF16), b.astype(BF16), (((0,), (0,)), ((), ())),
                           preferred_element_type=F32)


def _dot3(a, b):
    ah = a.astype(BF16)
    al = (a - ah.astype(F32)).astype(BF16)
    bh = b.astype(BF16)
    bl = (b - bh.astype(F32)).astype(BF16)
    d = functools.partial(jnp.dot, preferred_element_type=F32)
    return d(ah, bh) + (d(ah, bl) + d(al, bh))


def _dot01(x, m01):
    hi = x.astype(BF16)
    r1 = x - hi.astype(F32)
    mid = r1.astype(BF16)
    lo = (r1 - mid.astype(F32)).astype(BF16)
    d = functools.partial(jnp.dot, preferred_element_type=F32)
    return d(hi, m01) + d(mid, m01) + d(lo, m01)


def _seg_ones(n, seg):
    r = lax.broadcasted_iota(jnp.int32, (n, n), 0) // seg
    c = lax.broadcasted_iota(jnp.int32, (n, n), 1) // seg
    return jnp.where(r == c, 1.0, 0.0).astype(BF16)


def _sigmoid(x):
    return 1.0 / (1.0 + jnp.exp(-x))


def _softplus(x):
    return jnp.maximum(x, 0.0) + jnp.log1p(jnp.exp(-jnp.abs(x)))


def _ada_kernel(c_ref, w_ref, b_ref, o_ref):
    c = c_ref[...]
    s = c * _sigmoid(c)
    o_ref[...] = _bdot(s, w_ref[...]) + b_ref[...]


def _ada_mod(cvecs, w_mod, b_mod):
    n_out = w_mod.shape[-1]
    tn = 1024
    return pl.pallas_call(
        _ada_kernel,
        grid=(DEPTH, n_out // tn),
        in_specs=[pl.BlockSpec((8, D_MODEL), lambda l, j: (0, 0)),
                  pl.BlockSpec((None, D_MODEL, tn), lambda l, j: (l, 0, j)),
                  pl.BlockSpec((None, 1, tn), lambda l, j: (l, 0, j))],
        out_specs=pl.BlockSpec((None, 8, tn), lambda l, j: (l, 0, j)),
        out_shape=jax.ShapeDtypeStruct((DEPTH, 8, n_out), F32),
        compiler_params=_cparams(("parallel", "parallel")),
        name="ada_mod",
    )(cvecs, w_mod, b_mod.reshape(DEPTH, 1, n_out))


def _proj_kernel(x_ref, mod_ref, g_ref, w_ref, b_ref, cos_ref, sin_ref,
                 qk_ref, va_ref, kv32_ref, bqkv_ref, bgc_ref, gates_ref):
    x = x_ref[...]
    y = x * lax.rsqrt(jnp.mean(x * x, axis=-1, keepdims=True) + EPS) * g_ref[...]
    u = y * (1.0 + mod_ref[:, D_MODEL:2 * D_MODEL]) + mod_ref[:, 0:D_MODEL]
    acc = _bdot(u, w_ref[...]) + b_ref[...]
    kv32_ref[...] = acc[:, 512:1536]
    va_ref[...] = acc[:, 1024:1536].astype(BF16)
    bqkv_ref[...] = acc[:, 1536:2304]
    bgc_ref[...] = acc[:, 2304:3584]
    gates_ref[...] = acc[:, 3584:3712]
    cos = cos_ref[...]
    sin = sin_ref[...]
    even = (lax.broadcasted_iota(jnp.int32, cos.shape, 1) % 2) == 0
    for j in range(8):
        xj = acc[:, j * LANES:(j + 1) * LANES]
        swapped = jnp.where(even, pltpu.roll(xj, LANES - 1, 1), pltpu.roll(xj, 1, 1))
        r = xj * cos + swapped * sin
        if j < 4:
            r = r * (A_QK_DIM ** -0.5)
        qk_ref[:, j * LANES:(j + 1) * LANES] = r.astype(BF16)


def _in_proj(x_all, mod_l, g1, w_p, b_p, cos_t, sin_t, n_ctx_tiles, tiles_per_seq):
    t_rows = x_all.shape[0]
    n_tiles = t_rows // ROW_TILE

    def mod_idx(i):
        return (jnp.where(i < n_ctx_tiles, 0, 1 + (i - n_ctx_tiles) // tiles_per_seq), 0, 0)

    def rope_idx(i):
        return (jnp.where(i < n_ctx_tiles, 0, 1 + (i - n_ctx_tiles) % tiles_per_seq), 0)

    row = lambda w: pl.BlockSpec((ROW_TILE, w), lambda i: (i, 0))
    outs = [(1024, BF16), (512, BF16), (1024, F32), (768, F32), (1280, F32), (LANES, F32)]
    return pl.pallas_call(
        _proj_kernel,
        grid=(n_tiles,),
        in_specs=[row(D_MODEL),
                  pl.BlockSpec((None, 1, 6 * D_MODEL), mod_idx),
                  pl.BlockSpec((1, D_MODEL), lambda i: (0, 0)),
                  pl.BlockSpec((D_MODEL, N_PROJ), lambda i: (0, 0)),
                  pl.BlockSpec((1, N_PROJ), lambda i: (0, 0)),
                  pl.BlockSpec((ROW_TILE, LANES), rope_idx),
                  pl.BlockSpec((ROW_TILE, LANES), rope_idx)],
        out_specs=[row(w) for w, _ in outs],
        out_shape=[jax.ShapeDtypeStruct((t_rows, w), dt) for w, dt in outs],
        compiler_params=_cparams(("parallel",)),
        name="in_proj",
    )(x_all, mod_l.reshape(8, 1, 6 * D_MODEL), g1.reshape(1, D_MODEL), w_p, b_p.reshape(1, N_PROJ),
      cos_t, sin_t)


def _gates_kernel(g_ref, par_ref, o_ref):
    x = g_ref[...]
    alog = par_ref[0:8, 0:1]
    dtb = par_ref[0:8, 1:2]
    fb = par_ref[24:32, 1:2]
    g = -jnp.exp(alog) * _softplus(x[0:8] + dtb)
    beta = _sigmoid(x[8:16])
    ig = x[16:24]
    lf = -_softplus(-(x[24:32] + fb))
    tl = x.shape[1]
    r = lax.broadcasted_iota(jnp.int32, (LANES, LANES), 0)
    c = lax.broadcasted_iota(jnp.int32, (LANES, LANES), 1)
    same = (r // CHUNK) == (c // CHUNK)
    pre = jnp.where(same & (r <= c), 1.0, 0.0).astype(BF16)
    suf = jnp.where(same & (r >= c), 1.0, 0.0).astype(BF16)
    tot = jnp.where(same, 1.0, 0.0).astype(BF16)
    m01 = jnp.concatenate([pre, suf, tot], axis=1)
    fwd = (lax.broadcasted_iota(jnp.int32, (16, LANES), 0) % 8) < 4
    o_ref[16:24, :] = beta
    o_ref[24:32, :] = ig
    for j in range(tl // LANES):
        sl = slice(j * LANES, (j + 1) * LANES)
        xs = jnp.concatenate([g[:, sl], lf[:, sl]], axis=0)
        cs = _dot01(xs, m01)
        cum = jnp.where(fwd, cs[:, 0:LANES], cs[:, LANES:2 * LANES])
        total = cs[:, 2 * LANES:3 * LANES]
        o_ref[0:8, sl] = cum[0:8]
        o_ref[8:16, sl] = total[0:8]
        o_ref[32:40, sl] = cum[8:16]
        o_ref[40:48, sl] = total[8:16]


def _gate_prep(gates_t, par):
    t_rows = gates_t.shape[1]
    tl = math.gcd(t_rows, 2048)
    return pl.pallas_call(
        _gates_kernel,
        grid=(t_rows // tl,),
        in_specs=[pl.BlockSpec((32, tl), lambda i: (0, i)),
                  pl.BlockSpec((32, LANES), lambda i: (0, 0))],
        out_specs=pl.BlockSpec((48, tl), lambda i: (0, i)),
        out_shape=jax.ShapeDtypeStruct((48, t_rows), F32),
        compiler_params=_cparams(("parallel",)),
        name="gate_prep",
    )(gates_t, par)


def _conv_kernel(x_ref, p_ref, n_ref, w_ref, o_ref, ext_s, *, n_ctx_tiles, tiles_per_seq):
    i = pl.program_id(0)
    j = (i - n_ctx_tiles) % tiles_per_seq
    is_ctx = i < n_ctx_tiles
    first = jnp.logical_or(is_ctx, j == 0)
    last = jnp.logical_or(is_ctx, j == tiles_per_seq - 1)
    ext_s[0:HALO, :] = jnp.where(first, 0.0, p_ref[...])
    ext_s[HALO:HALO + ROW_TILE, :] = x_ref[...]
    ext_s[HALO + ROW_TILE:, :] = jnp.where(last, 0.0, n_ref[...])
    w = w_ref[...]
    y = None
    for k in range(CONV_K):
        off = HALO - CONV_K // 2 + k
        term = ext_s[off:off + ROW_TILE, :] * w[k:k + 1, :]
        y = term if y is None else y + term
    y = y * _sigmoid(y)
    ones = _seg_ones(HW, HEAD_DIM)
    q = y[:, 0:HW]
    k_ = y[:, HW:2 * HW]
    o_ref[:, 0:HW] = q * lax.rsqrt(_dot01(q * q, ones) + EPS) * (HEAD_DIM ** -0.5)
    o_ref[:, HW:2 * HW] = k_ * lax.rsqrt(_dot01(k_ * k_, ones) + EPS)
    o_ref[:, 2 * HW:3 * HW] = y[:, 2 * HW:3 * HW]


def _delta_conv(bqkv, conv_w, n_ctx_tiles, tiles_per_seq):
    t_rows, width = bqkv.shape
    n_tiles = t_rows // ROW_TILE
    hb = ROW_TILE // HALO
    n_hb = t_rows // HALO
    return pl.pallas_call(
        functools.partial(_conv_kernel, n_ctx_tiles=n_ctx_tiles, tiles_per_seq=tiles_per_seq),
        grid=(n_tiles,),
        in_specs=[pl.BlockSpec((ROW_TILE, width), lambda i: (i, 0)),
                  pl.BlockSpec((HALO, width), lambda i: (jnp.maximum(i * hb - 1, 0), 0)),
                  pl.BlockSpec((HALO, width), lambda i: (jnp.minimum((i + 1) * hb, n_hb - 1), 0)),
                  pl.BlockSpec((CONV_K, width), lambda i: (0, 0))],
        out_specs=pl.BlockSpec((ROW_TILE, width), lambda i: (i, 0)),
        out_shape=jax.ShapeDtypeStruct((t_rows, width), F32),
        scratch_shapes=[pltpu.VMEM((ROW_TILE + 2 * HALO, width), F32)],
        compiler_params=_cparams(("parallel",)),
        name="delta_conv",
    )(bqkv, bqkv, bqkv, conv_w)


def _attn_kernel(lqk_ref, q_ref, k_ref, vt_ref, o_ref, *, lam_init, tq, tk):
    q = q_ref[...]
    lane = lax.broadcasted_iota(jnp.int32, q.shape, 1)
    zero = jnp.zeros_like(q)
    q2 = jnp.concatenate([jnp.where(lane < A_QK_DIM, q, zero), jnp.where(lane >= A_QK_DIM, q, zero)], axis=0)
    m = l = acc = None
    for j in range(k_ref.shape[0] // tk):
        st = lax.dot_general(k_ref[j * tk:(j + 1) * tk, :], q2, (((1,), (1,)), ((), ())),
                             preferred_element_type=F32)
        m_new = jnp.max(st, axis=0, keepdims=True)
        if j > 0:
            m_new = jnp.maximum(m, m_new)
        p = jnp.exp(st - m_new)
        pv = jnp.dot(vt_ref[:, j * tk:(j + 1) * tk], p.astype(BF16), preferred_element_type=F32)
        if j == 0:
            l, acc = jnp.sum(p, axis=0, keepdims=True), pv
        else:
            alpha = jnp.exp(m - m_new)
            l = alpha * l + jnp.sum(p, axis=0, keepdims=True)
            acc = alpha * acc + pv
        m = m_new
    o = acc / l
    lq = lqk_ref[...]
    lam = (jnp.exp(jnp.sum(lq[0:1] * lq[1:2], axis=-1, keepdims=True))
           - jnp.exp(jnp.sum(lq[2:3] * lq[3:4], axis=-1, keepdims=True)) + lam_init)
    o_ref[...] = (o[:, 0:tq] - lam * o[:, tq:2 * tq]).T


def _diff_attention(lqk, qk, k_all, vt_all, lam_init, q_row0, n_seq, lq, tq, tk):
    lk = k_all.shape[1]
    qb0 = q_row0 // tq
    nq = lq // tq
    return pl.pallas_call(
        functools.partial(_attn_kernel, lam_init=lam_init, tq=tq, tk=tk),
        grid=(n_seq, HEADS, nq),
        in_specs=[pl.BlockSpec((4, A_QK_DIM), lambda b, h, i: (0, 0)),
                  pl.BlockSpec((tq, LANES), lambda b, h, i: (qb0 + b * nq + i, h)),
                  pl.BlockSpec((None, lk, LANES), lambda b, h, i: (b, 0, h)),
                  pl.BlockSpec((None, A_V_DIM, lk), lambda b, h, i: (b, h, 0))],
        out_specs=pl.BlockSpec((tq, LANES), lambda b, h, i: (b * nq + i, h)),
        out_shape=jax.ShapeDtypeStruct((n_seq * lq, HEADS * A_V_DIM), F32),
        compiler_params=_cparams(("parallel", "parallel", "parallel")),
        name="diff_attention",
    )(lqk, qk, k_all, vt_all)


def _stack4(x):
    return jnp.concatenate([x, x, x, x], axis=0)


def _collapse(x):
    return (x[0:CHUNK] + x[CHUNK:2 * CHUNK]) + (x[2 * CHUNK:3 * CHUNK] + x[3 * CHUNK:4 * CHUNK])


def _block_masks(direction):
    r = lax.broadcasted_iota(jnp.int32, (HW, HW), 0)
    c = lax.broadcasted_iota(jnp.int32, (HW, HW), 1)
    same = (r // CHUNK) == (c // CHUNK)
    diff = ((r % CHUNK) - (c % CHUNK)) * (1 - 2 * direction)
    return same, jnp.logical_and(same, diff >= 0), jnp.logical_and(same, diff > 0)


def _delta_kernel(q_ref, k_ref, v_ref, col_ref, row_ref, s0_ref, o_ref, sout_ref, s_s):
    direction = pl.program_id(1)
    t = pl.program_id(2)

    @pl.when(t == 0)
    def _():
        s_s[...] = s0_ref[...]

    same, incl, strict = _block_masks(direction)
    eye = (lax.broadcasted_iota(jnp.int32, (HW, HW), 0) == lax.broadcasted_iota(jnp.int32, (HW, HW), 1))
    pre = []
    for c in range(CHUNKS_PER_TILE):
        rows = slice(c * CHUNK, (c + 1) * CHUNK)
        q4 = _stack4(q_ref[rows, :])
        k4 = _stack4(k_ref[rows, :])
        v4 = _stack4(v_ref[rows, :])
        col = col_ref[c]
        gc, gtot, beta = col[:, 0:1], col[:, 1:2], col[:, 2:3]
        gc_row = row_ref[c][0:1, :]
        kbm = jnp.where(same, k4 * beta, 0.0)
        qm = jnp.where(same, q4, 0.0)
        kk = _bdot_nt(jnp.concatenate([kbm, qm], axis=0), k4)
        e_incl = jnp.exp(jnp.where(incl, gc - gc_row, -jnp.inf))
        a = jnp.where(strict, kk[0:HW] * e_incl, 0.0)
        aqk = kk[HW:2 * HW] * e_incl
        tinv = jnp.where(eye, 1.0, 0.0) - a
        x = a
        for _ in range(5):
            x = _dot3(x, x)
            tinv = tinv + _dot3(tinv, x)
        egc = jnp.exp(gc)
        vbm = jnp.where(same, v4 * beta, 0.0)
        uw = _bdot(tinv, jnp.concatenate([vbm, kbm * egc], axis=1))
        u = _collapse(uw[:, 0:HW])
        w = _collapse(uw[:, HW:2 * HW])
        qd = _collapse(qm * egc)
        kd = _collapse(jnp.where(same, k4 * jnp.exp(gtot - gc), 0.0))
        pre.append((u, w, qd, kd, aqk, jnp.exp(gtot)))

    def scan(order):
        for c in order:
            u, w, qd, kd, aqk, gl = pre[c]
            s = s_s[...]
            wq = _bdot(jnp.concatenate([w, qd], axis=0), s)
            v_new = u - wq[0:CHUNK]
            vn4 = jnp.where(same, _stack4(v_new), 0.0)
            o_ref[c * CHUNK:(c + 1) * CHUNK, :] = wq[CHUNK:2 * CHUNK] + _collapse(_bdot(aqk, vn4))
            s_s[...] = s * gl + jnp.where(same, _bdot_tn(kd, v_new), 0.0)

    @pl.when(direction == 0)
    def _():
        scan(range(CHUNKS_PER_TILE))

    @pl.when(direction == 1)
    def _():
        scan(range(CHUNKS_PER_TILE - 1, -1, -1))

    @pl.when(t == pl.num_programs(2) - 1)
    def _():
        sout_ref[...] = s_s[...]


def _mlstm_kernel(q_ref, k_ref, v_ref, col_ref, row_ref, cn0_ref, m0_ref,
                  o_ref, cnout_ref, mout_ref, cn_s, m_s):
    direction = pl.program_id(1)
    t = pl.program_id(2)

    @pl.when(t == 0)
    def _():
        cn_s[...] = cn0_ref[...]
        m_s[...] = m0_ref[...]

    same, incl, _ = _block_masks(direction)
    pre = []
    for c in range(CHUNKS_PER_TILE):
        rows = slice(c * CHUNK, (c + 1) * CHUNK)
        q4 = _stack4(q_ref[rows, :])
        k4 = _stack4(k_ref[rows, :]) * (HEAD_DIM ** -0.5)
        v4 = _stack4(v_ref[rows, :])
        col = col_ref[c]
        ig, b, btot = col[:, 3:4], col[:, 4:5], col[:, 5:6]
        rw = row_ref[c]
        r_row = rw[3:4, :] - rw[4:5, :]
        qm = jnp.where(same, q4, 0.0)
        dmat = jnp.where(incl, b + r_row, -jnp.inf)
        dmax = jnp.max(dmat, axis=-1, keepdims=True)
        qk = _bdot_nt(qm, k4)
        kvl = btot - b + ig
        mkv = jnp.concatenate(
            [jnp.broadcast_to(jnp.max(kvl[h * CHUNK:(h + 1) * CHUNK], axis=0, keepdims=True), (CHUNK, 1))
             for h in range(HEADS)], axis=0)
        vaug = jnp.concatenate([jnp.where(same, v4, 0.0),
                                jnp.where(_head_cols(), 1.0, 0.0)], axis=1)
        pre.append((qm, k4, vaug, b, btot, dmat, dmax, qk, kvl, mkv))

    def scan(order):
        for c in order:
            qm, k4, vaug, b, btot, dmat, dmax, qk, kvl, mkv = pre[c]
            cn = cn_s[...]
            ms = m_s[...]
            inter = b + ms
            m_t = jnp.maximum(inter, dmax)
            s = qk * jnp.exp(dmat - m_t)
            w_inter = jnp.exp(inter - m_t)
            nd = w_inter * _bdot(qm, cn) + _bdot(s, vaug)
            den = jnp.sum(nd[:, HW:HW + LANES], axis=-1, keepdims=True)
            hval = nd[:, 0:HW] / jnp.maximum(jnp.abs(den), jnp.exp(-m_t))
            o_ref[c * CHUNK:(c + 1) * CHUNK, :] = _collapse(hval)
            m_new = jnp.maximum(btot + ms, mkv)
            kw = jnp.where(same, k4 * jnp.exp(kvl - m_new), 0.0)
            cn_s[...] = jnp.exp(btot + ms - m_new) * cn + _bdot_tn(kw, vaug)
            m_s[...] = m_new

    @pl.when(direction == 0)
    def _():
        scan(range(CHUNKS_PER_TILE))

    @pl.when(direction == 1)
    def _():
        scan(range(CHUNKS_PER_TILE - 1, -1, -1))

    @pl.when(t == pl.num_programs(2) - 1)
    def _():
        cnout_ref[...] = cn_s[...]
        mout_ref[...] = jnp.broadcast_to(m_s[...], mout_ref.shape)


def _head_cols():
    r = lax.broadcasted_iota(jnp.int32, (HW, LANES), 0) // CHUNK
    c = lax.broadcasted_iota(jnp.int32, (HW, LANES), 1)
    return r == c


def _scan_specs(tile0, n_tiles, col_blocks):
    def tile_idx(s, d, t):
        return tile0 + s * n_tiles + jnp.where(d == 0, t, n_tiles - 1 - t)

    qkv = [pl.BlockSpec((ROW_TILE, HW), functools.partial(lambda s, d, t, cb: (tile_idx(s, d, t), cb), cb=cb))
           for cb in col_blocks]
    col = pl.BlockSpec((None, CHUNKS_PER_TILE, HW, 8), lambda s, d, t: (d, tile_idx(s, d, t), 0, 0))
    row = pl.BlockSpec((None, CHUNKS_PER_TILE, 8, HW), lambda s, d, t: (d, tile_idx(s, d, t), 0, 0))
    out = pl.BlockSpec((None, ROW_TILE, HW), lambda s, d, t: (d, s * n_tiles + jnp.where(d == 0, t, n_tiles - 1 - t), 0))
    return qkv, col, row, out


def _delta_scan(dqkv, colg, rowg, s0, tile0, n_seq, n_tiles):
    qkv, col, row, out = _scan_specs(tile0, n_tiles, (0, 1, 2))
    st = pl.BlockSpec((None, None, HW, HW), lambda s, d, t: (s, d, 0, 0))
    return pl.pallas_call(
        _delta_kernel,
        grid=(n_seq, N_DIR, n_tiles),
        in_specs=qkv + [col, row, st],
        out_specs=[out, st],
        out_shape=[jax.ShapeDtypeStruct((N_DIR, n_seq * n_tiles * ROW_TILE, HW), F32),
                   jax.ShapeDtypeStruct((n_seq, N_DIR, HW, HW), F32)],
        scratch_shapes=[pltpu.VMEM((HW, HW), F32)],
        compiler_params=_cparams(("parallel", "parallel", "arbitrary")),
        name="delta_scan",
    )(dqkv, dqkv, dqkv, colg, rowg, s0)


def _mlstm_scan(bgc, colg, rowg, cn0, m0, tile0, n_seq, n_tiles):
    qkv, col, row, out = _scan_specs(tile0, n_tiles, (1, 2, 3))
    cn = pl.BlockSpec((None, None, HW, HW + LANES), lambda s, d, t: (s, d, 0, 0))
    mm = pl.BlockSpec((None, None, HW, LANES), lambda s, d, t: (s, d, 0, 0))
    m_in = pl.BlockSpec((None, None, HW, 1), lambda s, d, t: (s, d, 0, 0))
    return pl.pallas_call(
        _mlstm_kernel,
        grid=(n_seq, N_DIR, n_tiles),
        in_specs=qkv + [col, row, cn, m_in],
        out_specs=[out, cn, mm],
        out_shape=[jax.ShapeDtypeStruct((N_DIR, n_seq * n_tiles * ROW_TILE, HW), F32),
                   jax.ShapeDtypeStruct((n_seq, N_DIR, HW, HW + LANES), F32),
                   jax.ShapeDtypeStruct((n_seq, N_DIR, HW, LANES), F32)],
        scratch_shapes=[pltpu.VMEM((HW, HW + LANES), F32), pltpu.VMEM((HW, 1), F32)],
        compiler_params=_cparams(("parallel", "parallel", "arbitrary")),
        name="mlstm_scan",
    )(bgc, bgc, bgc, colg, rowg, cn0, m0)


def _merge_kernel(x_ref, mod_ref, a_ref, od_ref, hm_ref, bg_ref, co_ref, ag_ref, dg_ref, mg_ref, w_ref,
                  o_ref, *, lam_init):
    a = a_ref[...]
    a = a * lax.rsqrt(_dot01(a * a, _seg_ones(HEADS * A_V_DIM, A_V_DIM)) * (1.0 / A_V_DIM) + EPS)
    a = a * ag_ref[...] * (1.0 - lam_init)
    ones = _seg_ones(HW, HEAD_DIM)
    od = od_ref[0] + od_ref[1]
    bg = bg_ref[...]
    d = od * lax.rsqrt(_dot01(od * od, ones) * (1.0 / HEAD_DIM) + EPS) * dg_ref[...] * (bg * _sigmoid(bg))
    hm = hm_ref[0] + hm_ref[1]
    m = _sigmoid(co_ref[...]) * (hm * lax.rsqrt(_dot01(hm * hm, ones) * (1.0 / HEAD_DIM) + EPS) * mg_ref[...])
    cat = jnp.concatenate([a.astype(BF16), d.astype(BF16), m.astype(BF16)], axis=1)
    mix = jnp.dot(cat, w_ref[...], preferred_element_type=F32)
    o_ref[...] = x_ref[...] + mod_ref[:, 2 * D_MODEL:3 * D_MODEL] * mix


def _merge(x_all, mod_l, a_out, o_delta, h_m, bgc, attn_g, delta_g, mlstm_g, w_out, lam_init,
           n_ctx_tiles, tiles_per_seq):
    t_rows = x_all.shape[0]

    def mod_idx(i):
        return (jnp.where(i < n_ctx_tiles, 0, 1 + (i - n_ctx_tiles) // tiles_per_seq), 0, 0)

    full = lambda r, c: pl.BlockSpec((r, c), lambda i: (0, 0))
    return pl.pallas_call(
        functools.partial(_merge_kernel, lam_init=lam_init),
        grid=(t_rows // ROW_TILE,),
        in_specs=[pl.BlockSpec((ROW_TILE, D_MODEL), lambda i: (i, 0)),
                  pl.BlockSpec((None, 1, 6 * D_MODEL), mod_idx),
                  pl.BlockSpec((ROW_TILE, HEADS * A_V_DIM), lambda i: (i, 0)),
                  pl.BlockSpec((N_DIR, ROW_TILE, HW), lambda i: (0, i, 0)),
                  pl.BlockSpec((N_DIR, ROW_TILE, HW), lambda i: (0, i, 0)),
                  pl.BlockSpec((ROW_TILE, HW), lambda i: (i, 0)),
                  pl.BlockSpec((ROW_TILE, HW), lambda i: (i, 4)),
                  full(1, HEADS * A_V_DIM), full(1, HW), full(1, HW),
                  full(D_MODEL, D_MODEL)],
        out_specs=pl.BlockSpec((ROW_TILE, D_MODEL), lambda i: (i, 0)),
        out_shape=jax.ShapeDtypeStruct((t_rows, D_MODEL), F32),
        compiler_params=_cparams(("parallel",)),
        name="merge_out_proj",
    )(x_all, mod_l.reshape(8, 1, 6 * D_MODEL), a_out, o_delta, h_m, bgc, bgc,
      jnp.tile(attn_g, HEADS).reshape(1, -1), jnp.tile(delta_g, HEADS).reshape(1, -1),
      jnp.tile(mlstm_g, HEADS).reshape(1, -1), w_out)


def _ffn_kernel(x_ref, mod_ref, g_ref, wgu_ref, wdn_ref, fg_ref, o_ref, *, final):
    x = x_ref[...]
    y = x * lax.rsqrt(jnp.mean(x * x, axis=-1, keepdims=True) + EPS) * g_ref[...]
    u = y * (1.0 + mod_ref[:, 4 * D_MODEL:5 * D_MODEL]) + mod_ref[:, 3 * D_MODEL:4 * D_MODEL]
    h = jnp.dot(u.astype(BF16), wgu_ref[...], preferred_element_type=F32)
    gate = h[:, 0:FFN_HIDDEN]
    act = (gate * _sigmoid(gate)) * h[:, FFN_HIDDEN:2 * FFN_HIDDEN]
    out = x + mod_ref[:, 5 * D_MODEL:6 * D_MODEL] * jnp.dot(act.astype(BF16), wdn_ref[...],
                                                             preferred_element_type=F32)
    if final:
        out = out * lax.rsqrt(jnp.mean(out * out, axis=-1, keepdims=True) + EPS) * fg_ref[...]
    o_ref[...] = out


def _ffn(x_all, mod_l, g2, w_gu, w_dn, final_g, final, n_ctx_tiles, tiles_per_seq):
    t_rows = x_all.shape[0]

    def mod_idx(i):
        return (jnp.where(i < n_ctx_tiles, 0, 1 + (i - n_ctx_tiles) // tiles_per_seq), 0, 0)

    full = lambda r, c: pl.BlockSpec((r, c), lambda i: (0, 0))
    return pl.pallas_call(
        functools.partial(_ffn_kernel, final=final),
        grid=(t_rows // ROW_TILE,),
        in_specs=[pl.BlockSpec((ROW_TILE, D_MODEL), lambda i: (i, 0)),
                  pl.BlockSpec((None, 1, 6 * D_MODEL), mod_idx),
                  full(1, D_MODEL), full(D_MODEL, 2 * FFN_HIDDEN), full(FFN_HIDDEN, D_MODEL),
                  full(1, D_MODEL)],
        out_specs=pl.BlockSpec((ROW_TILE, D_MODEL), lambda i: (i, 0)),
        out_shape=jax.ShapeDtypeStruct((t_rows, D_MODEL), F32),
        compiler_params=_cparams(("parallel",)),
        name="ffn",
    )(x_all, mod_l.reshape(8, 1, 6 * D_MODEL), g2.reshape(1, D_MODEL), w_gu, w_dn,
      final_g.reshape(1, D_MODEL))


def _rope_tables(dec_seq):
    n_rows = dec_seq // GRID_W
    rows = jnp.repeat(jnp.arange(n_rows, dtype=F32), GRID_W)
    cols = jnp.tile(jnp.arange(GRID_W, dtype=F32), n_rows)
    n_freq = A_QK_DIM // 4
    inv_freq = ROPE_BASE ** (-jnp.arange(n_freq, dtype=F32) / n_freq)
    ang = jnp.concatenate([rows[:, None] * inv_freq, cols[:, None] * inv_freq], axis=-1)
    cos = jnp.repeat(jnp.cos(ang), 2, axis=-1)
    sin = jnp.repeat(jnp.sin(ang), 2, axis=-1) * jnp.tile(jnp.array([-1.0, 1.0], F32), A_QK_DIM // 2)
    cos = jnp.concatenate([jnp.ones((ROW_TILE, A_QK_DIM), F32), cos], axis=0)
    sin = jnp.concatenate([jnp.zeros((ROW_TILE, A_QK_DIM), F32), sin], axis=0)
    return jnp.tile(cos, (1, 2)), jnp.tile(sin, (1, 2))


def _permute_proj(w_in_l, b_in_l):
    def perm(a):
        head, ba_bb, tail, ci_cf = a[..., 0:2560], a[..., 2560:2576], a[..., 2576:3600], a[..., 3600:3616]
        pad = jnp.zeros(a.shape[:-1] + (N_PROJ - 3616,), a.dtype)
        return jnp.concatenate([head, tail, ba_bb, ci_cf, pad], axis=-1)
    return perm(w_in_l).astype(BF16), perm(b_in_l)


def _block_diag(s):
    eye = jnp.eye(HEADS, dtype=s.dtype)
    out = s[..., :, :, None, :] * eye[:, None, :, None]
    return out.reshape(s.shape[:-3] + (HW, HW))


def _block_diag_inv(s_bd):
    s6 = s_bd.reshape(s_bd.shape[:-2] + (HEADS, HEAD_DIM, HEADS, HEAD_DIM))
    return jnp.stack([s6[..., h, :, h, :] for h in range(HEADS)], axis=-3)


def _norm_cols(n):
    eye = jnp.eye(HEADS, LANES, dtype=n.dtype)
    return (n[..., None] * eye[:, None, :]).reshape(n.shape[:-2] + (HW, LANES))


def kernel(x_prompt, x_sample, cache_attn_k, cache_attn_v, state_delta, state_mlstm_C, state_mlstm_n,
           state_mlstm_m, c, c_ctx, norm1_g, norm2_g, w_mod, b_mod, w_in, b_in, w_out, lambda_qk,
           attn_subln_g, delta_conv_w, delta_A_log, delta_dt_bias, delta_norm_g, mlstm_f_bias,
           mlstm_norm_g, w_gate_up, w_down, final_norm_g):
    batch, seq, _ = x_prompt.shape
    dec_batch, dec_seq, _ = x_sample.shape
    past_len = cache_attn_k.shape[2]
    assert seq == ROW_TILE and dec_seq % ROW_TILE == 0 and dec_batch + 1 <= 8
    t_ctx = batch * seq
    t_lat = dec_batch * dec_seq
    t_rows = t_ctx + t_lat
    n_ctx_tiles = t_ctx // ROW_TILE
    tiles_per_seq = dec_seq // ROW_TILE
    n_chunks = t_rows // CHUNK

    x_all = jnp.concatenate([x_prompt.reshape(t_ctx, D_MODEL), x_sample.reshape(t_lat, D_MODEL)], axis=0)
    cvecs = jnp.zeros((8, D_MODEL), F32).at[0].set(c_ctx).at[1:1 + dec_batch].set(c)
    mod = _ada_mod(cvecs, w_mod, b_mod)
    cos_t, sin_t = _rope_tables(dec_seq)

    ks_l, vs_l, sd_l, cm_l, nm_l, mm_l = [], [], [], [], [], []
    for l in range(DEPTH):
        lam_init = 0.8 - 0.6 * math.exp(-0.3 * l)
        w_p, b_p = _permute_proj(w_in[l], b_in[l])
        qk, va, kv32, bqkv, bgc, gates = _in_proj(x_all, mod[l], norm1_g[l], w_p, b_p, cos_t, sin_t,
                                                 n_ctx_tiles, tiles_per_seq)
        ks_l.append(kv32[:t_ctx, 0:512].reshape(batch, seq, HEADS, 2, A_QK_DIM))
        vs_l.append(kv32[:t_ctx, 512:1024].reshape(batch, seq, HEADS, A_V_DIM))

        par = jnp.zeros((32, LANES), F32)
        par = par.at[0:8, 0].set(delta_A_log[l].reshape(-1)).at[0:8, 1].set(delta_dt_bias[l].reshape(-1))
        par = par.at[24:32, 1].set(mlstm_f_bias[l].reshape(-1))
        grow = _gate_prep(gates[:, 0:32].T, par)
        r6 = grow.reshape(6, N_DIR, HEADS, n_chunks, CHUNK)
        colg = jnp.pad(r6.transpose(1, 3, 2, 4, 0).reshape(N_DIR, n_chunks, HW, 6),
                       ((0, 0), (0, 0), (0, 0), (0, 2)))
        rowg = jnp.pad(r6.transpose(1, 3, 0, 2, 4).reshape(N_DIR, n_chunks, 6, HW),
                       ((0, 0), (0, 0), (0, 2), (0, 0)))

        k_ctx = qk[:t_ctx, 512:1024].reshape(batch, seq, 512)
        v_ctx = va[:t_ctx].reshape(batch, seq, 512).transpose(0, 2, 1)
        a_ctx = _diff_attention(lambda_qk[l], qk, k_ctx, v_ctx, lam_init, 0, batch, seq, seq, seq)
        k_lat = jnp.concatenate([qk[t_ctx:, 512:1024].reshape(dec_batch, dec_seq, 512),
                                 cache_attn_k[:, l].reshape(dec_batch, past_len, 512).astype(BF16)], axis=1)
        v_lat = jnp.concatenate([va[t_ctx:].reshape(dec_batch, dec_seq, 512),
                                 cache_attn_v[:, l].reshape(dec_batch, past_len, 512).astype(BF16)],
                                axis=1).transpose(0, 2, 1)
        a_lat = _diff_attention(lambda_qk[l], qk, k_lat, v_lat, lam_init, t_ctx, dec_batch, dec_seq, 512, 512)
        a_out = jnp.concatenate([a_ctx, a_lat], axis=0)

        dqkv = _delta_conv(bqkv, delta_conv_w[l], n_ctx_tiles, tiles_per_seq)
        od_ctx, s_ctx = _delta_scan(dqkv, colg, rowg, jnp.zeros((batch, N_DIR, HW, HW), F32),
                                    0, batch, 1)
        od_lat, _ = _delta_scan(dqkv, colg, rowg, _block_diag(state_delta[:, l]),
                                n_ctx_tiles, dec_batch, tiles_per_seq)
        o_delta = jnp.concatenate([od_ctx, od_lat], axis=1)
        sd_l.append(_block_diag_inv(s_ctx))

        cn_lat = jnp.concatenate([_block_diag(state_mlstm_C[:, l]), _norm_cols(state_mlstm_n[:, l])], axis=-1)
        m_lat = jnp.repeat(state_mlstm_m[:, l], HEAD_DIM, axis=-1)[..., None]
        hm_ctx, cn_ctx, m_ctx = _mlstm_scan(bgc, colg, rowg, jnp.zeros((batch, N_DIR, HW, HW + LANES), F32),
                                            jnp.zeros((batch, N_DIR, HW, 1), F32), 0, batch, 1)
        hm_lat, _, _ = _mlstm_scan(bgc, colg, rowg, cn_lat, m_lat, n_ctx_tiles, dec_batch, tiles_per_seq)
        h_m = jnp.concatenate([hm_ctx, hm_lat], axis=1)
        cm_l.append(_block_diag_inv(cn_ctx[..., 0:HW]))
        n6 = cn_ctx[..., HW:HW + HEADS].reshape(batch, N_DIR, HEADS, HEAD_DIM, HEADS)
        nm_l.append(jnp.stack([n6[..., h, :, h] for h in range(HEADS)], axis=-2))
        mm_l.append(m_ctx[:, :, ::HEAD_DIM, 0])

        x_all = _merge(x_all, mod[l], a_out, o_delta, h_m, bgc, attn_subln_g[l], delta_norm_g[l],
                       mlstm_norm_g[l], w_out[l].astype(BF16), lam_init, n_ctx_tiles, tiles_per_seq)
        x_all = _ffn(x_all, mod[l], norm2_g[l], w_gate_up[l].astype(BF16), w_down[l].astype(BF16),
                     final_norm_g, l == DEPTH - 1, n_ctx_tiles, tiles_per_seq)

    y_prompt = x_all[:t_ctx].reshape(batch, seq, D_MODEL)
    y_sample = x_all[t_ctx:].reshape(dec_batch, dec_seq, D_MODEL)
    return (y_prompt, y_sample, jnp.stack(ks_l, axis=1), jnp.stack(vs_l, axis=1), jnp.stack(sd_l, axis=1),
            jnp.stack(cm_l, axis=1), jnp.stack(nm_l, axis=1), jnp.stack(mm_l, axis=1))
```

```python
import functools
import math

import jax
import jax.numpy as jnp
from jax import lax
from jax.experimental import pallas as pl
from jax.experimental.pallas import tpu as pltpu

F32 = jnp.float32
BF16 = jnp.bfloat16

D_MODEL = 1024
DEPTH = 2
GRID_W = 64
N_DIR = 2
CHUNK = 64
ROPE_BASE = 10000.0
EPS = 1e-6
HEADS = 4
A_QK_DIM = 64
A_V_DIM = 128
HEAD_DIM = 64
CONV_K = 5
FFN_HIDDEN = 2816
ROW_TILE = 256
CHUNKS_PER_TILE = ROW_TILE // CHUNK
HW = HEADS * HEAD_DIM
LANES = 128
HALO = 8

N_PROJ = 3712
VMEM_LIMIT = 56 * 1024 * 1024


def _cparams(sem):
    return pltpu.CompilerParams(dimension_semantics=sem, vmem_limit_bytes=VMEM_LIMIT)


def _bdot(a, b):
    return jnp.dot(a.astype(BF16), b.astype(BF16), preferred_element_type=F32)


def _bdot_nt(a, b):
    return lax.dot_general(a.astype(BF16), b.astype(BF16), (((1,), (1,)), ((), ())),
                           preferred_element_type=F32)


def _dot01(x, m01):
    hi = x.astype(BF16)
    r1 = x - hi.astype(F32)
    mid = r1.astype(BF16)
    lo = (r1 - mid.astype(F32)).astype(BF16)
    d = functools.partial(jnp.dot, preferred_element_type=F32)
    return d(hi, m01) + d(mid, m01) + d(lo, m01)


def _seg_ones(n, seg):
    r = lax.broadcasted_iota(jnp.int32, (n, n), 0) // seg
    c = lax.broadcasted_iota(jnp.int32, (n, n), 1) // seg
    return jnp.where(r == c, 1.0, 0.0).astype(BF16)


def _sigmoid(x):
    return 1.0 / (1.0 + jnp.exp(-x))


def _softplus(x):
    return jnp.maximum(x, 0.0) + jnp.log1p(jnp.exp(-jnp.abs(x)))


def _ada_kernel(c_ref, w_ref, b_ref, o_ref):
    c = c_ref[...]
    s = c * _sigmoid(c)
    o_ref[...] = _bdot(s, w_ref[...]) + b_ref[...]


def _ada_mod(cvecs, w_mod, b_mod):
    n_out = w_mod.shape[-1]
    tn = 1024
    return pl.pallas_call(
        _ada_kernel,
        grid=(DEPTH, n_out // tn),
        in_specs=[pl.BlockSpec((8, D_MODEL), lambda l, j: (0, 0)),
                  pl.BlockSpec((None, D_MODEL, tn), lambda l, j: (l, 0, j)),
                  pl.BlockSpec((None, 1, tn), lambda l, j: (l, 0, j))],
        out_specs=pl.BlockSpec((None, 8, tn), lambda l, j: (l, 0, j)),
        out_shape=jax.ShapeDtypeStruct((DEPTH, 8, n_out), F32),
        compiler_params=_cparams(("parallel", "parallel")),
        name="ada_mod",
    )(cvecs, w_mod, b_mod.reshape(DEPTH, 1, n_out))


def _proj_kernel(x_ref, mod_ref, g_ref, w_ref, b_ref, cos_ref, sin_ref,
                 qk_ref, va_ref, kv32_ref, bqkv_ref, bgc_ref, gates_ref):
    x = x_ref[...]
    y = x * lax.rsqrt(jnp.mean(x * x, axis=-1, keepdims=True) + EPS) * g_ref[...]
    u = y * (1.0 + mod_ref[:, D_MODEL:2 * D_MODEL]) + mod_ref[:, 0:D_MODEL]
    acc = _bdot(u, w_ref[...]) + b_ref[...]
    kv32_ref[...] = acc[:, 512:1536]
    va_ref[...] = acc[:, 1024:1536].astype(BF16)
    bqkv_ref[...] = acc[:, 1536:2304]
    bgc_ref[...] = acc[:, 2304:3584]
    gates_ref[...] = acc[:, 3584:3712]
    cos = cos_ref[...]
    sin = sin_ref[...]
    even = (lax.broadcasted_iota(jnp.int32, cos.shape, 1) % 2) == 0
    for j in range(8):
        xj = acc[:, j * LANES:(j + 1) * LANES]
        swapped = jnp.where(even, pltpu.roll(xj, LANES - 1, 1), pltpu.roll(xj, 1, 1))
        r = xj * cos + swapped * sin
        if j < 4:
            r = r * (A_QK_DIM ** -0.5)
        qk_ref[:, j * LANES:(j + 1) * LANES] = r.astype(BF16)


def _in_proj(x_all, mod_l, g1, w_p, b_p, cos_t, sin_t, n_ctx_tiles, tiles_per_seq):
    t_rows = x_all.shape[0]
    n_tiles = t_rows // ROW_TILE

    def mod_idx(i):
        return (jnp.where(i < n_ctx_tiles, 0, 1 + (i - n_ctx_tiles) // tiles_per_seq), 0, 0)

    def rope_idx(i):
        return (jnp.where(i < n_ctx_tiles, 0, 1 + (i - n_ctx_tiles) % tiles_per_seq), 0)

    row = lambda w: pl.BlockSpec((ROW_TILE, w), lambda i: (i, 0))
    outs = [(1024, BF16), (512, BF16), (1024, F32), (768, F32), (1280, F32), (LANES, F32)]
    return pl.pallas_call(
        _proj_kernel,
        grid=(n_tiles,),
        in_specs=[row(D_MODEL),
                  pl.BlockSpec((None, 1, 6 * D_MODEL), mod_idx),
                  pl.BlockSpec((1, D_MODEL), lambda i: (0, 0)),
                  pl.BlockSpec((D_MODEL, N_PROJ), lambda i: (0, 0)),
                  pl.BlockSpec((1, N_PROJ), lambda i: (0, 0)),
                  pl.BlockSpec((ROW_TILE, LANES), rope_idx),
                  pl.BlockSpec((ROW_TILE, LANES), rope_idx)],
        out_specs=[row(w) for w, _ in outs],
        out_shape=[jax.ShapeDtypeStruct((t_rows, w), dt) for w, dt in outs],
        compiler_params=_cparams(("parallel",)),
        name="in_proj",
    )(x_all, mod_l.reshape(8, 1, 6 * D_MODEL), g1.reshape(1, D_MODEL), w_p, b_p.reshape(1, N_PROJ),
      cos_t, sin_t)


def _gates_kernel(g_ref, par_ref, o_ref):
    x = g_ref[...]
    alog = par_ref[0:8, 0:1]
    dtb = par_ref[0:8, 1:2]
    fb = par_ref[24:32, 1:2]
    g = -jnp.exp(alog) * _softplus(x[0:8] + dtb)
    beta = _sigmoid(x[8:16])
    ig = x[16:24]
    lf = -_softplus(-(x[24:32] + fb))
    tl = x.shape[1]
    r = lax.broadcasted_iota(jnp.int32, (LANES, LANES), 0)
    c = lax.broadcasted_iota(jnp.int32, (LANES, LANES), 1)
    same = (r // CHUNK) == (c // CHUNK)
    pre = jnp.where(same & (r <= c), 1.0, 0.0).astype(BF16)
    suf = jnp.where(same & (r >= c), 1.0, 0.0).astype(BF16)
    tot = jnp.where(same, 1.0, 0.0).astype(BF16)
    m01 = jnp.concatenate([pre, suf, tot], axis=1)
    fwd = (lax.broadcasted_iota(jnp.int32, (16, LANES), 0) % 8) < 4
    o_ref[16:24, :] = beta
    o_ref[24:32, :] = ig
    for j in range(tl // LANES):
        sl = slice(j * LANES, (j + 1) * LANES)
        xs = jnp.concatenate([g[:, sl], lf[:, sl]], axis=0)
        cs = _dot01(xs, m01)
        cum = jnp.where(fwd, cs[:, 0:LANES], cs[:, LANES:2 * LANES])
        total = cs[:, 2 * LANES:3 * LANES]
        o_ref[0:8, sl] = cum[0:8]
        o_ref[8:16, sl] = total[0:8]
        o_ref[32:40, sl] = cum[8:16]
        o_ref[40:48, sl] = total[8:16]


def _gate_prep(gates_t, par):
    t_rows = gates_t.shape[1]
    tl = math.gcd(t_rows, 2048)
    return pl.pallas_call(
        _gates_kernel,
        grid=(t_rows // tl,),
        in_specs=[pl.BlockSpec((32, tl), lambda i: (0, i)),
                  pl.BlockSpec((32, LANES), lambda i: (0, 0))],
        out_specs=pl.BlockSpec((48, tl), lambda i: (0, i)),
        out_shape=jax.ShapeDtypeStruct((48, t_rows), F32),
        compiler_params=_cparams(("parallel",)),
        name="gate_prep",
    )(gates_t, par)


def _conv_kernel(x_ref, p_ref, n_ref, w_ref, o_ref, ext_s, *, n_ctx_tiles, tiles_per_seq):
    i = pl.program_id(0)
    j = (i - n_ctx_tiles) % tiles_per_seq
    is_ctx = i < n_ctx_tiles
    first = jnp.logical_or(is_ctx, j == 0)
    last = jnp.logical_or(is_ctx, j == tiles_per_seq - 1)
    ext_s[0:HALO, :] = jnp.where(first, 0.0, p_ref[...])
    ext_s[HALO:HALO + ROW_TILE, :] = x_ref[...]
    ext_s[HALO + ROW_TILE:, :] = jnp.where(last, 0.0, n_ref[...])
    w = w_ref[...]
    y = None
    for k in range(CONV_K):
        off = HALO - CONV_K // 2 + k
        term = ext_s[off:off + ROW_TILE, :] * w[k:k + 1, :]
        y = term if y is None else y + term
    y = y * _sigmoid(y)
    ones = _seg_ones(HW, HEAD_DIM)
    q = y[:, 0:HW]
    k_ = y[:, HW:2 * HW]
    o_ref[:, 0:HW] = q * lax.rsqrt(_dot01(q * q, ones) + EPS) * (HEAD_DIM ** -0.5)
    o_ref[:, HW:2 * HW] = k_ * lax.rsqrt(_dot01(k_ * k_, ones) + EPS)
    o_ref[:, 2 * HW:3 * HW] = y[:, 2 * HW:3 * HW]


def _delta_conv(bqkv, conv_w, n_ctx_tiles, tiles_per_seq):
    t_rows, width = bqkv.shape
    n_tiles = t_rows // ROW_TILE
    hb = ROW_TILE // HALO
    n_hb = t_rows // HALO
    return pl.pallas_call(
        functools.partial(_conv_kernel, n_ctx_tiles=n_ctx_tiles, tiles_per_seq=tiles_per_seq),
        grid=(n_tiles,),
        in_specs=[pl.BlockSpec((ROW_TILE, width), lambda i: (i, 0)),
                  pl.BlockSpec((HALO, width), lambda i: (jnp.maximum(i * hb - 1, 0), 0)),
                  pl.BlockSpec((HALO, width), lambda i: (jnp.minimum((i + 1) * hb, n_hb - 1), 0)),
                  pl.BlockSpec((CONV_K, width), lambda i: (0, 0))],
        out_specs=pl.BlockSpec((ROW_TILE, width), lambda i: (i, 0)),
        out_shape=jax.ShapeDtypeStruct((t_rows, width), F32),
        scratch_shapes=[pltpu.VMEM((ROW_TILE + 2 * HALO, width), F32)],
        compiler_params=_cparams(("parallel",)),
        name="delta_conv",
    )(bqkv, bqkv, bqkv, conv_w)


def _attn_kernel(lqk_ref, q_ref, k_ref, vt_ref, o_ref, *, lam_init, tq, tk):
    q = q_ref[...]
    lane = lax.broadcasted_iota(jnp.int32, q.shape, 1)
    zero = jnp.zeros_like(q)
    q2 = jnp.concatenate([jnp.where(lane < A_QK_DIM, q, zero), jnp.where(lane >= A_QK_DIM, q, zero)], axis=0)
    m = l = acc = None
    for j in range(k_ref.shape[0] // tk):
        st = lax.dot_general(k_ref[j * tk:(j + 1) * tk, :], q2, (((1,), (1,)), ((), ())),
                             preferred_element_type=F32)
        m_new = jnp.max(st, axis=0, keepdims=True)
        if j > 0:
            m_new = jnp.maximum(m, m_new)
        p = jnp.exp(st - m_new)
        pv = jnp.dot(vt_ref[:, j * tk:(j + 1) * tk], p.astype(BF16), preferred_element_type=F32)
        if j == 0:
            l, acc = jnp.sum(p, axis=0, keepdims=True), pv
        else:
            alpha = jnp.exp(m - m_new)
            l = alpha * l + jnp.sum(p, axis=0, keepdims=True)
            acc = alpha * acc + pv
        m = m_new
    o = acc / l
    lq = lqk_ref[...]
    lam = (jnp.exp(jnp.sum(lq[0:1] * lq[1:2], axis=-1, keepdims=True))
           - jnp.exp(jnp.sum(lq[2:3] * lq[3:4], axis=-1, keepdims=True)) + lam_init)
    o_ref[...] = (o[:, 0:tq] - lam * o[:, tq:2 * tq]).T


def _diff_attention(lqk, qk, k_all, vt_all, lam_init, q_row0, n_seq, lq, tq, tk):
    lk = k_all.shape[1]
    qb0 = q_row0 // tq
    nq = lq // tq
    return pl.pallas_call(
        functools.partial(_attn_kernel, lam_init=lam_init, tq=tq, tk=tk),
        grid=(n_seq, HEADS, nq),
        in_specs=[pl.BlockSpec((4, A_QK_DIM), lambda b, h, i: (0, 0)),
                  pl.BlockSpec((tq, LANES), lambda b, h, i: (qb0 + b * nq + i, h)),
                  pl.BlockSpec((None, lk, LANES), lambda b, h, i: (b, 0, h)),
                  pl.BlockSpec((None, A_V_DIM, lk), lambda b, h, i: (b, h, 0))],
        out_specs=pl.BlockSpec((tq, LANES), lambda b, h, i: (b * nq + i, h)),
        out_shape=jax.ShapeDtypeStruct((n_seq * lq, HEADS * A_V_DIM), F32),
        compiler_params=_cparams(("parallel", "parallel", "parallel")),
        name="diff_attention",
    )(lqk, qk, k_all, vt_all)


def _stack4(x):
    return jnp.concatenate([x, x, x, x], axis=0)


def _collapse(x):
    return (x[0:CHUNK] + x[CHUNK:2 * CHUNK]) + (x[2 * CHUNK:3 * CHUNK] + x[3 * CHUNK:4 * CHUNK])


def _block_masks(direction):
    r = lax.broadcasted_iota(jnp.int32, (HW, HW), 0)
    c = lax.broadcasted_iota(jnp.int32, (HW, HW), 1)
    same = (r // CHUNK) == (c // CHUNK)
    diff = ((r % CHUNK) - (c % CHUNK)) * (1 - 2 * direction)
    return same, jnp.logical_and(same, diff >= 0), jnp.logical_and(same, diff > 0)


def _unit_tri_inverse(mats):
    r = lax.broadcasted_iota(jnp.int32, (HW, HW), 0)
    c = lax.broadcasted_iota(jnp.int32, (HW, HW), 1)
    b16 = (r // 16) == (c // 16)
    b32 = (r // 32) == (c // 32)
    eye = jnp.where(r == c, 1.0, 0.0)
    a16 = [jnp.where(b16, a, 0.0) for a in mats]
    ts = [eye - a for a in a16]
    xs = [a.astype(BF16) for a in a16]
    for _ in range(3):
        xs = [jnp.dot(x, x, preferred_element_type=F32).astype(BF16) for x in xs]
        ts = [t + _bdot(t, x) for t, x in zip(ts, xs)]
    for lo_blk, hi_blk in ((b16, b32), (b32, None)):
        inside = jnp.logical_not(lo_blk) if hi_blk is None else jnp.logical_and(hi_blk, jnp.logical_not(lo_blk))
        offs = [jnp.where(inside, a, 0.0) for a in mats]
        tb = [t.astype(BF16) for t in ts]
        mids = [_bdot(t, o) for t, o in zip(tb, offs)]
        ts = [t - _bdot(m, t16) for t, m, t16 in zip(ts, mids, tb)]
    return ts


def _scan_job(n_ctx_tiles, tiles_per_seq):
    j = pl.program_id(0)
    t = (j - n_ctx_tiles) % tiles_per_seq
    is_ctx = j < n_ctx_tiles
    return jnp.logical_or(is_ctx, t == 0), jnp.logical_or(is_ctx, t == tiles_per_seq - 1)


def _delta_prep(chunks):
    masks = {d: _block_masks(d) for d in (0, 1)}
    n = len(chunks)
    same = masks[0][0]
    dirs = [ch[5] for ch in chunks]
    cols = [ch[3] for ch in chunks]
    rows = [ch[4] for ch in chunks]
    km = [jnp.where(same, _stack4(ch[1]), 0.0) for ch in chunks]
    kbm = [km[i] * cols[i][:, 2:3] for i in range(n)]
    qm = [jnp.where(same, _stack4(ch[0]), 0.0) for ch in chunks]
    kk = [_bdot_nt(jnp.concatenate([kbm[i], qm[i]], axis=0), _stack4(chunks[i][1])) for i in range(n)]
    e_incl = [jnp.exp(jnp.where(masks[dirs[i]][1], cols[i][:, 0:1] - rows[i][0:1, :], -jnp.inf)) for i in range(n)]
    a = [jnp.where(masks[dirs[i]][2], kk[i][0:HW] * e_incl[i], 0.0) for i in range(n)]
    aqk = [kk[i][HW:2 * HW] * e_incl[i] for i in range(n)]
    tinv = _unit_tri_inverse(a)
    egc = [jnp.exp(cols[i][:, 0:1]) for i in range(n)]
    vbm = [jnp.where(same, _stack4(chunks[i][2]) * cols[i][:, 2:3], 0.0) for i in range(n)]
    uw = [_bdot(tinv[i], jnp.concatenate([vbm[i], kbm[i] * egc[i]], axis=1)) for i in range(n)]
    kd_t = [km[i].T * jnp.exp(rows[i][1:2, :] - rows[i][0:1, :]) for i in range(n)]
    return [(uw[i][:, 0:HW], uw[i][:, HW:2 * HW], qm[i] * egc[i], kd_t[i], aqk[i], jnp.exp(cols[i][:, 1:2]))
            for i in range(n)]


def _delta_step(pre, s_ref, o_ref, c):
    u, w, qd, kd_t, aqk, gl = pre
    s = s_ref[...]
    wq = _bdot(jnp.concatenate([w, qd], axis=0), s)
    v_new = u - wq[0:HW]
    o_ref[c * CHUNK:(c + 1) * CHUNK, :] = _collapse(wq[HW:2 * HW] + _bdot(aqk, v_new))
    s_ref[...] = s * gl + _bdot(kd_t, v_new)


def _delta_kernel(qf, kf, vf, qb, kb, vb, colf, rowf, colb, rowb, s0_ref, of_ref, ob_ref, sout_ref,
                  sf_s, sb_s, *, n_ctx_tiles, tiles_per_seq):
    first, last = _scan_job(n_ctx_tiles, tiles_per_seq)

    @pl.when(first)
    def _():
        sf_s[...] = s0_ref[0]
        sb_s[...] = s0_ref[1]

    def chunk(q, k, v, col, row, c, direction):
        rows = slice(c * CHUNK, (c + 1) * CHUNK)
        return (q[rows, :], k[rows, :], v[rows, :], col[c], row[c], direction)

    pre = _delta_prep([chunk(qf, kf, vf, colf, rowf, c, 0) for c in range(CHUNKS_PER_TILE)]
                      + [chunk(qb, kb, vb, colb, rowb, c, 1) for c in range(CHUNKS_PER_TILE)])
    for i in range(CHUNKS_PER_TILE):
        cb = CHUNKS_PER_TILE - 1 - i
        _delta_step(pre[i], sf_s, of_ref, i)
        _delta_step(pre[CHUNKS_PER_TILE + cb], sb_s, ob_ref, cb)

    @pl.when(last)
    def _():
        sout_ref[0] = sf_s[...]
        sout_ref[1] = sb_s[...]


def _head_cols():
    r = lax.broadcasted_iota(jnp.int32, (HW, LANES), 0) // CHUNK
    c = lax.broadcasted_iota(jnp.int32, (HW, LANES), 1)
    return r == c


def _mlstm_prep(chunks):
    masks = {d: _block_masks(d) for d in (0, 1)}
    same = masks[0][0]
    n = len(chunks)
    ig = [ch[3][:, 3:4] for ch in chunks]
    b = [ch[3][:, 4:5] for ch in chunks]
    btot = [ch[3][:, 5:6] for ch in chunks]
    ig_row = [ch[4][3:4, :] for ch in chunks]
    b_row = [ch[4][4:5, :] for ch in chunks]
    btot_row = [ch[4][5:6, :] for ch in chunks]
    k4 = [_stack4(ch[1]) * (HEAD_DIM ** -0.5) for ch in chunks]
    qm = [jnp.where(same, _stack4(ch[0]), 0.0) for ch in chunks]
    qk = [_bdot_nt(qm[i], k4[i]) for i in range(n)]
    dmat = [jnp.where(masks[chunks[i][5]][1], b[i] + (ig_row[i] - b_row[i]), -jnp.inf) for i in range(n)]
    dmax = [jnp.max(d, axis=-1, keepdims=True) for d in dmat]
    k_t = [jnp.where(same, k, 0.0).T for k in k4]
    kvl_row = [btot_row[i] - b_row[i] + ig_row[i] for i in range(n)]
    kvl_col = [btot[i] - b[i] + ig[i] for i in range(n)]
    mkv_row = [jnp.max(jnp.where(same, kv, -jnp.inf), axis=0, keepdims=True) for kv in kvl_col]
    mkv_col = [jnp.max(jnp.where(same, kv, -jnp.inf), axis=1, keepdims=True) for kv in kvl_row]
    ones_cols = jnp.where(_head_cols(), 1.0, 0.0)
    vaug = [jnp.concatenate([jnp.where(same, _stack4(ch[2]), 0.0), ones_cols], axis=1) for ch in chunks]
    return [(qm[i], k_t[i], vaug[i], b[i], btot[i], btot_row[i], dmat[i], dmax[i], qk[i], kvl_row[i],
             mkv_col[i], mkv_row[i]) for i in range(n)]


def _mlstm_step(pre, cn_ref, mcol_ref, mrow_ref, o_ref, c):
    qm, k_t, vaug, b, btot, btot_row, dmat, dmax, qk, kvl_row, mkv_col, mkv_row = pre
    cn = cn_ref[...]
    ms = mcol_ref[...]
    inter = b + ms
    m_t = jnp.maximum(inter, dmax)
    s = qk * jnp.exp(dmat - m_t)
    nd = jnp.exp(inter - m_t) * _bdot(qm, cn) + _bdot(s, vaug)
    den = jnp.sum(nd[:, HW:HW + LANES], axis=-1, keepdims=True)
    hval = nd[:, 0:HW] / jnp.maximum(jnp.abs(den), jnp.exp(-m_t))
    o_ref[c * CHUNK:(c + 1) * CHUNK, :] = _collapse(hval)
    m_new = jnp.maximum(btot + ms, mkv_col)
    m_new_row = jnp.maximum(btot_row + mrow_ref[...], mkv_row)
    cn_ref[...] = jnp.exp(btot + ms - m_new) * cn + _bdot(k_t * jnp.exp(kvl_row - m_new_row), vaug)
    mcol_ref[...] = m_new
    mrow_ref[...] = m_new_row


def _mlstm_kernel(qf, kf, vf, qb, kb, vb, colf, rowf, colb, rowb, cn0_ref, mc0_ref, mr0_ref,
                  of_ref, ob_ref, cnout_ref, mout_ref, cnf_s, cnb_s, mcf_s, mcb_s, mrf_s, mrb_s,
                  *, n_ctx_tiles, tiles_per_seq):
    first, last = _scan_job(n_ctx_tiles, tiles_per_seq)

    @pl.when(first)
    def _():
        cnf_s[...] = cn0_ref[0]
        cnb_s[...] = cn0_ref[1]
        mcf_s[...] = mc0_ref[0]
        mcb_s[...] = mc0_ref[1]
        mrf_s[...] = mr0_ref[0]
        mrb_s[...] = mr0_ref[1]

    def chunk(q, k, v, col, row, c, direction):
        rows = slice(c * CHUNK, (c + 1) * CHUNK)
        return (q[rows, :], k[rows, :], v[rows, :], col[c], row[c], direction)

    pre = _mlstm_prep([chunk(qf, kf, vf, colf, rowf, c, 0) for c in range(CHUNKS_PER_TILE)]
                      + [chunk(qb, kb, vb, colb, rowb, c, 1) for c in range(CHUNKS_PER_TILE)])
    for i in range(CHUNKS_PER_TILE):
        cb = CHUNKS_PER_TILE - 1 - i
        _mlstm_step(pre[i], cnf_s, mcf_s, mrf_s, of_ref, i)
        _mlstm_step(pre[CHUNKS_PER_TILE + cb], cnb_s, mcb_s, mrb_s, ob_ref, cb)

    @pl.when(last)
    def _():
        cnout_ref[0] = cnf_s[...]
        cnout_ref[1] = cnb_s[...]
        mout_ref[0] = jnp.broadcast_to(mcf_s[...], (HW, LANES))
        mout_ref[1] = jnp.broadcast_to(mcb_s[...], (HW, LANES))


def _scan_call(kernel_fn, name, src, col_blocks, colg, rowg, states, state_out_shapes, scratch,
               n_ctx_tiles, tiles_per_seq):
    t_rows = src.shape[0]
    n_tiles = t_rows // ROW_TILE

    def seq_of(j):
        return jnp.where(j < n_ctx_tiles, j, n_ctx_tiles + (j - n_ctx_tiles) // tiles_per_seq)

    def back(j):
        jj = j - n_ctx_tiles
        mirrored = n_ctx_tiles + (jj // tiles_per_seq) * tiles_per_seq + tiles_per_seq - 1 - jj % tiles_per_seq
        return jnp.where(j < n_ctx_tiles, j, mirrored)

    def tile_spec(cb, bwd):
        return pl.BlockSpec((ROW_TILE, HW), (lambda j: (back(j), cb)) if bwd else (lambda j: (j, cb)))

    def gate_spec(shape, d):
        return pl.BlockSpec((None, CHUNKS_PER_TILE) + shape,
                            (lambda j: (1, back(j), 0, 0)) if d else (lambda j: (0, j, 0, 0)))

    def state_spec(shape):
        return pl.BlockSpec((None,) + tuple(shape), lambda j: (seq_of(j),) + (0,) * len(shape))

    in_specs = ([tile_spec(cb, False) for cb in col_blocks] + [tile_spec(cb, True) for cb in col_blocks]
                + [gate_spec((HW, 8), 0), gate_spec((8, HW), 0), gate_spec((HW, 8), 1), gate_spec((8, HW), 1)]
                + [state_spec(s.shape[1:]) for s in states])
    out_specs = ([pl.BlockSpec((ROW_TILE, HW), lambda j: (j, 0)),
                  pl.BlockSpec((ROW_TILE, HW), lambda j: (back(j), 0))]
                 + [state_spec(s[1:]) for s in state_out_shapes])
    out_shape = ([jax.ShapeDtypeStruct((t_rows, HW), F32)] * 2
                 + [jax.ShapeDtypeStruct(tuple(s), F32) for s in state_out_shapes])
    return pl.pallas_call(
        functools.partial(kernel_fn, n_ctx_tiles=n_ctx_tiles, tiles_per_seq=tiles_per_seq),
        grid=(n_tiles,),
        in_specs=in_specs, out_specs=out_specs, out_shape=out_shape,
        scratch_shapes=scratch,
        compiler_params=_cparams(("arbitrary",)),
        name=name,
    )(*([src] * (2 * len(col_blocks))), colg, rowg, colg, rowg, *states)


def _delta_scan(dqkv, colg, rowg, s0, n_ctx_tiles, tiles_per_seq):
    return _scan_call(_delta_kernel, "delta_scan", dqkv, (0, 1, 2), colg, rowg, [s0], [s0.shape],
                      [pltpu.VMEM((HW, HW), F32)] * 2, n_ctx_tiles, tiles_per_seq)


def _mlstm_scan(bgc, colg, rowg, cn0, mc0, mr0, n_ctx_tiles, tiles_per_seq):
    n_seq = cn0.shape[0]
    return _scan_call(_mlstm_kernel, "mlstm_scan", bgc, (1, 2, 3), colg, rowg, [cn0, mc0, mr0],
                      [cn0.shape, (n_seq, N_DIR, HW, LANES)],
                      [pltpu.VMEM((HW, HW + LANES), F32)] * 2 + [pltpu.VMEM((HW, 1), F32)] * 2
                      + [pltpu.VMEM((1, HW), F32)] * 2, n_ctx_tiles, tiles_per_seq)


def _merge_kernel(x_ref, mod_ref, a_ref, odf_ref, odb_ref, hmf_ref, hmb_ref, bg_ref, co_ref,
                  ag_ref, dg_ref, mg_ref, w_ref, o_ref, *, lam_init):
    a = a_ref[...]
    a = a * lax.rsqrt(_dot01(a * a, _seg_ones(HEADS * A_V_DIM, A_V_DIM)) * (1.0 / A_V_DIM) + EPS)
    a = a * ag_ref[...] * (1.0 - lam_init)
    ones = _seg_ones(HW, HEAD_DIM)
    od = odf_ref[...] + odb_ref[...]
    bg = bg_ref[...]
    d = od * lax.rsqrt(_dot01(od * od, ones) * (1.0 / HEAD_DIM) + EPS) * dg_ref[...] * (bg * _sigmoid(bg))
    hm = hmf_ref[...] + hmb_ref[...]
    m = _sigmoid(co_ref[...]) * (hm * lax.rsqrt(_dot01(hm * hm, ones) * (1.0 / HEAD_DIM) + EPS) * mg_ref[...])
    cat = jnp.concatenate([a.astype(BF16), d.astype(BF16), m.astype(BF16)], axis=1)
    mix = jnp.dot(cat, w_ref[...], preferred_element_type=F32)
    o_ref[...] = x_ref[...] + mod_ref[:, 2 * D_MODEL:3 * D_MODEL] * mix


def _merge(x_all, mod_l, a_out, od, hm, bgc, attn_g, delta_g, mlstm_g, w_out, lam_init,
           n_ctx_tiles, tiles_per_seq):
    t_rows = x_all.shape[0]

    def mod_idx(i):
        return (jnp.where(i < n_ctx_tiles, 0, 1 + (i - n_ctx_tiles) // tiles_per_seq), 0, 0)

    full = lambda r, c: pl.BlockSpec((r, c), lambda i: (0, 0))
    row = lambda w, cb=0: pl.BlockSpec((ROW_TILE, w), lambda i: (i, cb))
    return pl.pallas_call(
        functools.partial(_merge_kernel, lam_init=lam_init),
        grid=(t_rows // ROW_TILE,),
        in_specs=[row(D_MODEL),
                  pl.BlockSpec((None, 1, 6 * D_MODEL), mod_idx),
                  row(HEADS * A_V_DIM), row(HW), row(HW), row(HW), row(HW), row(HW), row(HW, 4),
                  full(1, HEADS * A_V_DIM), full(1, HW), full(1, HW),
                  full(D_MODEL, D_MODEL)],
        out_specs=row(D_MODEL),
        out_shape=jax.ShapeDtypeStruct((t_rows, D_MODEL), F32),
        compiler_params=_cparams(("parallel",)),
        name="merge_out_proj",
    )(x_all, mod_l.reshape(8, 1, 6 * D_MODEL), a_out, od[0], od[1], hm[0], hm[1], bgc, bgc,
      jnp.tile(attn_g, HEADS).reshape(1, -1), jnp.tile(delta_g, HEADS).reshape(1, -1),
      jnp.tile(mlstm_g, HEADS).reshape(1, -1), w_out)


def _ffn_kernel(x_ref, mod_ref, g_ref, wgu_ref, wdn_ref, fg_ref, o_ref, *, final):
    x = x_ref[...]
    y = x * lax.rsqrt(jnp.mean(x * x, axis=-1, keepdims=True) + EPS) * g_ref[...]
    u = y * (1.0 + mod_ref[:, 4 * D_MODEL:5 * D_MODEL]) + mod_ref[:, 3 * D_MODEL:4 * D_MODEL]
    h = jnp.dot(u.astype(BF16), wgu_ref[...], preferred_element_type=F32)
    gate = h[:, 0:FFN_HIDDEN]
    act = (gate * _sigmoid(gate)) * h[:, FFN_HIDDEN:2 * FFN_HIDDEN]
    out = x + mod_ref[:, 5 * D_MODEL:6 * D_MODEL] * jnp.dot(act.astype(BF16), wdn_ref[...],
                                                             preferred_element_type=F32)
    if final:
        out = out * lax.rsqrt(jnp.mean(out * out, axis=-1, keepdims=True) + EPS) * fg_ref[...]
    o_ref[...] = out


def _ffn(x_all, mod_l, g2, w_gu, w_dn, final_g, final, n_ctx_tiles, tiles_per_seq):
    t_rows = x_all.shape[0]

    def mod_idx(i):
        return (jnp.where(i < n_ctx_tiles, 0, 1 + (i - n_ctx_tiles) // tiles_per_seq), 0, 0)

    full = lambda r, c: pl.BlockSpec((r, c), lambda i: (0, 0))
    return pl.pallas_call(
        functools.partial(_ffn_kernel, final=final),
        grid=(t_rows // ROW_TILE,),
        in_specs=[pl.BlockSpec((ROW_TILE, D_MODEL), lambda i: (i, 0)),
                  pl.BlockSpec((None, 1, 6 * D_MODEL), mod_idx),
                  full(1, D_MODEL), full(D_MODEL, 2 * FFN_HIDDEN), full(FFN_HIDDEN, D_MODEL),
                  full(1, D_MODEL)],
        out_specs=pl.BlockSpec((ROW_TILE, D_MODEL), lambda i: (i, 0)),
        out_shape=jax.ShapeDtypeStruct((t_rows, D_MODEL), F32),
        compiler_params=_cparams(("parallel",)),
        name="ffn",
    )(x_all, mod_l.reshape(8, 1, 6 * D_MODEL), g2.reshape(1, D_MODEL), w_gu, w_dn,
      final_g.reshape(1, D_MODEL))


def _rope_tables(dec_seq):
    n_rows = dec_seq // GRID_W
    rows = jnp.repeat(jnp.arange(n_rows, dtype=F32), GRID_W)
    cols = jnp.tile(jnp.arange(GRID_W, dtype=F32), n_rows)
    n_freq = A_QK_DIM // 4
    inv_freq = ROPE_BASE ** (-jnp.arange(n_freq, dtype=F32) / n_freq)
    ang = jnp.concatenate([rows[:, None] * inv_freq, cols[:, None] * inv_freq], axis=-1)
    cos = jnp.repeat(jnp.cos(ang), 2, axis=-1)
    sin = jnp.repeat(jnp.sin(ang), 2, axis=-1) * jnp.tile(jnp.array([-1.0, 1.0], F32), A_QK_DIM // 2)
    cos = jnp.concatenate([jnp.ones((ROW_TILE, A_QK_DIM), F32), cos], axis=0)
    sin = jnp.concatenate([jnp.zeros((ROW_TILE, A_QK_DIM), F32), sin], axis=0)
    return jnp.tile(cos, (1, 2)), jnp.tile(sin, (1, 2))


def _permute_proj(w_in_l, b_in_l):
    def perm(a):
        head, ba_bb, tail, ci_cf = a[..., 0:2560], a[..., 2560:2576], a[..., 2576:3600], a[..., 3600:3616]
        pad = jnp.zeros(a.shape[:-1] + (N_PROJ - 3616,), a.dtype)
        return jnp.concatenate([head, tail, ba_bb, ci_cf, pad], axis=-1)
    return perm(w_in_l).astype(BF16), perm(b_in_l)


def _block_diag(s):
    eye = jnp.eye(HEADS, dtype=s.dtype)
    out = s[..., :, :, None, :] * eye[:, None, :, None]
    return out.reshape(s.shape[:-3] + (HW, HW))


def _block_diag_inv(s_bd):
    s6 = s_bd.reshape(s_bd.shape[:-2] + (HEADS, HEAD_DIM, HEADS, HEAD_DIM))
    return jnp.stack([s6[..., h, :, h, :] for h in range(HEADS)], axis=-3)


def _norm_cols(n):
    eye = jnp.eye(HEADS, LANES, dtype=n.dtype)
    return (n[..., None] * eye[:, None, :]).reshape(n.shape[:-2] + (HW, LANES))


def _with_zero_ctx(batch, lat_state):
    return jnp.concatenate([jnp.zeros((batch,) + lat_state.shape[1:], lat_state.dtype), lat_state], axis=0)


def kernel(x_prompt, x_sample, cache_attn_k, cache_attn_v, state_delta, state_mlstm_C, state_mlstm_n,
           state_mlstm_m, c, c_ctx, norm1_g, norm2_g, w_mod, b_mod, w_in, b_in, w_out, lambda_qk,
           attn_subln_g, delta_conv_w, delta_A_log, delta_dt_bias, delta_norm_g, mlstm_f_bias,
           mlstm_norm_g, w_gate_up, w_down, final_norm_g):
    batch, seq, _ = x_prompt.shape
    dec_batch, dec_seq, _ = x_sample.shape
    past_len = cache_attn_k.shape[2]
    assert seq == ROW_TILE and dec_seq % ROW_TILE == 0 and dec_batch + 1 <= 8
    t_ctx = batch * seq
    t_lat = dec_batch * dec_seq
    t_rows = t_ctx + t_lat
    n_ctx_tiles = t_ctx // ROW_TILE
    tiles_per_seq = dec_seq // ROW_TILE
    n_chunks = t_rows // CHUNK

    x_all = jnp.concatenate([x_prompt.reshape(t_ctx, D_MODEL), x_sample.reshape(t_lat, D_MODEL)], axis=0)
    cvecs = jnp.zeros((8, D_MODEL), F32).at[0].set(c_ctx).at[1:1 + dec_batch].set(c)
    mod = _ada_mod(cvecs, w_mod, b_mod)
    cos_t, sin_t = _rope_tables(dec_seq)

    ks_l, vs_l, sd_l, cm_l, nm_l, mm_l = [], [], [], [], [], []
    for l in range(DEPTH):
        lam_init = 0.8 - 0.6 * math.exp(-0.3 * l)
        w_p, b_p = _permute_proj(w_in[l], b_in[l])
        qk, va, kv32, bqkv, bgc, gates = _in_proj(x_all, mod[l], norm1_g[l], w_p, b_p, cos_t, sin_t,
                                                 n_ctx_tiles, tiles_per_seq)
        ks_l.append(kv32[:t_ctx, 0:512].reshape(batch, seq, HEADS, 2, A_QK_DIM))
        vs_l.append(kv32[:t_ctx, 512:1024].reshape(batch, seq, HEADS, A_V_DIM))

        par = jnp.zeros((32, LANES), F32)
        par = par.at[0:8, 0].set(delta_A_log[l].reshape(-1)).at[0:8, 1].set(delta_dt_bias[l].reshape(-1))
        par = par.at[24:32, 1].set(mlstm_f_bias[l].reshape(-1))
        grow = _gate_prep(gates[:, 0:32].T, par)
        r6 = grow.reshape(6, N_DIR, HEADS, n_chunks, CHUNK)
        colg = jnp.pad(r6.transpose(1, 3, 2, 4, 0).reshape(N_DIR, n_chunks, HW, 6),
                       ((0, 0), (0, 0), (0, 0), (0, 2)))
        rowg = jnp.pad(r6.transpose(1, 3, 0, 2, 4).reshape(N_DIR, n_chunks, 6, HW),
                       ((0, 0), (0, 0), (0, 2), (0, 0)))

        k_ctx = qk[:t_ctx, 512:1024].reshape(batch, seq, 512)
        v_ctx = va[:t_ctx].reshape(batch, seq, 512).transpose(0, 2, 1)
        a_ctx = _diff_attention(lambda_qk[l], qk, k_ctx, v_ctx, lam_init, 0, batch, seq, seq, seq)
        k_lat = jnp.concatenate([qk[t_ctx:, 512:1024].reshape(dec_batch, dec_seq, 512),
                                 cache_attn_k[:, l].reshape(dec_batch, past_len, 512).astype(BF16)], axis=1)
        v_lat = jnp.concatenate([va[t_ctx:].reshape(dec_batch, dec_seq, 512),
                                 cache_attn_v[:, l].reshape(dec_batch, past_len, 512).astype(BF16)],
                                axis=1).transpose(0, 2, 1)
        a_lat = _diff_attention(lambda_qk[l], qk, k_lat, v_lat, lam_init, t_ctx, dec_batch, dec_seq, 512, 512)
        a_out = jnp.concatenate([a_ctx, a_lat], axis=0)

        dqkv = _delta_conv(bqkv, delta_conv_w[l], n_ctx_tiles, tiles_per_seq)
        od_f, od_b, s_fin = _delta_scan(dqkv, colg, rowg, _with_zero_ctx(batch, _block_diag(state_delta[:, l])),
                                        n_ctx_tiles, tiles_per_seq)
        sd_l.append(_block_diag_inv(s_fin[:batch]))

        cn_lat = jnp.concatenate([_block_diag(state_mlstm_C[:, l]), _norm_cols(state_mlstm_n[:, l])], axis=-1)
        m_lat = jnp.repeat(state_mlstm_m[:, l], HEAD_DIM, axis=-1)
        hm_f, hm_b, cn_fin, m_fin = _mlstm_scan(
            bgc, colg, rowg, _with_zero_ctx(batch, cn_lat), _with_zero_ctx(batch, m_lat[..., None]),
            _with_zero_ctx(batch, m_lat[:, :, None, :]), n_ctx_tiles, tiles_per_seq)
        cm_l.append(_block_diag_inv(cn_fin[:batch, :, :, 0:HW]))
        n6 = cn_fin[:batch, :, :, HW:HW + HEADS].reshape(batch, N_DIR, HEADS, HEAD_DIM, HEADS)
        nm_l.append(jnp.stack([n6[..., h, :, h] for h in range(HEADS)], axis=-2))
        mm_l.append(m_fin[:batch, :, ::HEAD_DIM, 0])

        x_all = _merge(x_all, mod[l], a_out, (od_f, od_b), (hm_f, hm_b), bgc, attn_subln_g[l], delta_norm_g[l],
                       mlstm_norm_g[l], w_out[l].astype(BF16), lam_init, n_ctx_tiles, tiles_per_seq)
        x_all = _ffn(x_all, mod[l], norm2_g[l], w_gate_up[l].astype(BF16), w_down[l].astype(BF16),
                     final_norm_g, l == DEPTH - 1, n_ctx_tiles, tiles_per_seq)

    y_prompt = x_all[:t_ctx].reshape(batch, seq, D_MODEL)
    y_sample = x_all[t_ctx:].reshape(dec_batch, dec_seq, D_MODEL)
    return (y_prompt, y_sample, jnp.stack(ks_l, axis=1), jnp.stack(vs_l, axis=1), jnp.stack(sd_l, axis=1),
            jnp.stack(cm_l, axis=1), jnp.stack(nm_l, axis=1), jnp.stack(mm_l, axis=1))
```

```python
import functools
import math

import jax
import jax.numpy as jnp
from jax import lax
from jax.experimental import pallas as pl
from jax.experimental.pallas import tpu as pltpu

F32 = jnp.float32
BF16 = jnp.bfloat16

D_MODEL = 1024
DEPTH = 2
GRID_W = 64
N_DIR = 2
CHUNK = 64
ROPE_BASE = 10000.0
EPS = 1e-6
HEADS = 4
A_QK_DIM = 64
A_V_DIM = 128
HEAD_DIM = 64
CONV_K = 5
FFN_HIDDEN = 2816
ROW_TILE = 256
CHUNKS_PER_TILE = ROW_TILE // CHUNK
HW = HEADS * HEAD_DIM
LANES = 128
HALO = 8
BF16_ROWS = 16

N_PROJ = 3712
VMEM_LIMIT = 56 * 1024 * 1024


def _cparams(sem):
    return pltpu.CompilerParams(dimension_semantics=sem, vmem_limit_bytes=VMEM_LIMIT)


def _bdot(a, b):
    return jnp.dot(a.astype(BF16), b.astype(BF16), preferred_element_type=F32)


def _bdot_nt(a, b):
    return lax.dot_general(a.astype(BF16), b.astype(BF16), (((1,), (1,)), ((), ())),
                           preferred_element_type=F32)


def _dot01(x, m01):
    hi = x.astype(BF16)
    r1 = x - hi.astype(F32)
    mid = r1.astype(BF16)
    lo = (r1 - mid.astype(F32)).astype(BF16)
    d = functools.partial(jnp.dot, preferred_element_type=F32)
    return d(hi, m01) + d(mid, m01) + d(lo, m01)


def _seg_ones(n, seg):
    r = lax.broadcasted_iota(jnp.int32, (n, n), 0) // seg
    c = lax.broadcasted_iota(jnp.int32, (n, n), 1) // seg
    return jnp.where(r == c, 1.0, 0.0).astype(BF16)


def _sigmoid(x):
    return 1.0 / (1.0 + jnp.exp(-x))


def _softplus(x):
    return jnp.maximum(x, 0.0) + jnp.log1p(jnp.exp(-jnp.abs(x)))


def _ada_kernel(c_ref, w_ref, b_ref, o_ref):
    c = c_ref[...]
    s = c * _sigmoid(c)
    o_ref[...] = _bdot(s, w_ref[...]) + b_ref[...]


def _ada_mod(cvecs, w_mod, b_mod):
    n_out = w_mod.shape[-1]
    tn = 1024
    return pl.pallas_call(
        _ada_kernel,
        grid=(DEPTH, n_out // tn),
        in_specs=[pl.BlockSpec((8, D_MODEL), lambda l, j: (0, 0)),
                  pl.BlockSpec((None, D_MODEL, tn), lambda l, j: (l, 0, j)),
                  pl.BlockSpec((None, 1, tn), lambda l, j: (l, 0, j))],
        out_specs=pl.BlockSpec((None, 8, tn), lambda l, j: (l, 0, j)),
        out_shape=jax.ShapeDtypeStruct((DEPTH, 8, n_out), F32),
        compiler_params=_cparams(("parallel", "parallel")),
        name="ada_mod",
    )(cvecs, w_mod, b_mod.reshape(DEPTH, 1, n_out))


def _proj_kernel(x_ref, mod_ref, g_ref, w_ref, b_ref, cos_ref, sin_ref,
                 qk_ref, va_ref, kv32_ref, bqkv_ref, bgc_ref, gates_ref, *, n_ctx_tiles):
    x = x_ref[...]
    y = x * lax.rsqrt(jnp.mean(x * x, axis=-1, keepdims=True) + EPS) * g_ref[...]
    u = y * (1.0 + mod_ref[:, D_MODEL:2 * D_MODEL]) + mod_ref[:, 0:D_MODEL]
    acc = _bdot(u, w_ref[...]) + b_ref[...]

    @pl.when(pl.program_id(0) < n_ctx_tiles)
    def _():
        kv32_ref[...] = acc[:, 512:1536]

    va_ref[...] = acc[:, 1024:1536].astype(BF16)
    bqkv_ref[...] = acc[:, 1536:2304]
    bgc_ref[...] = acc[:, 2304:3584]
    gates_ref[...] = acc[:, 3584:3712]
    cos = cos_ref[...]
    sin = sin_ref[...]
    even = (lax.broadcasted_iota(jnp.int32, cos.shape, 1) % 2) == 0
    for j in range(8):
        xj = acc[:, j * LANES:(j + 1) * LANES]
        swapped = jnp.where(even, pltpu.roll(xj, LANES - 1, 1), pltpu.roll(xj, 1, 1))
        r = xj * cos + swapped * sin
        if j < 4:
            r = r * (A_QK_DIM ** -0.5)
        qk_ref[:, j * LANES:(j + 1) * LANES] = r.astype(BF16)


def _in_proj(x_all, mod_l, g1, w_p, b_p, cos_t, sin_t, n_ctx_tiles, tiles_per_seq):
    t_rows = x_all.shape[0]
    n_tiles = t_rows // ROW_TILE

    def mod_idx(i):
        return (jnp.where(i < n_ctx_tiles, 0, 1 + (i - n_ctx_tiles) // tiles_per_seq), 0, 0)

    def rope_idx(i):
        return (jnp.where(i < n_ctx_tiles, 0, 1 + (i - n_ctx_tiles) % tiles_per_seq), 0)

    row = lambda w: pl.BlockSpec((ROW_TILE, w), lambda i: (i, 0))
    outs = [(1024, BF16), (512, BF16), (1024, F32), (768, F32), (1280, F32), (LANES, F32)]
    out_specs = [row(w) for w, _ in outs]
    out_shape = [jax.ShapeDtypeStruct((t_rows, w), dt) for w, dt in outs]
    out_specs[2] = pl.BlockSpec((ROW_TILE, 1024), lambda i: (jnp.minimum(i, n_ctx_tiles - 1), 0))
    out_shape[2] = jax.ShapeDtypeStruct((n_ctx_tiles * ROW_TILE, 1024), F32)
    return pl.pallas_call(
        functools.partial(_proj_kernel, n_ctx_tiles=n_ctx_tiles),
        grid=(n_tiles,),
        in_specs=[row(D_MODEL),
                  pl.BlockSpec((None, 1, 6 * D_MODEL), mod_idx),
                  pl.BlockSpec((1, D_MODEL), lambda i: (0, 0)),
                  pl.BlockSpec((D_MODEL, N_PROJ), lambda i: (0, 0)),
                  pl.BlockSpec((1, N_PROJ), lambda i: (0, 0)),
                  pl.BlockSpec((ROW_TILE, LANES), rope_idx),
                  pl.BlockSpec((ROW_TILE, LANES), rope_idx)],
        out_specs=out_specs,
        out_shape=out_shape,
        compiler_params=_cparams(("arbitrary",)),
        name="in_proj",
    )(x_all, mod_l.reshape(8, 1, 6 * D_MODEL), g1.reshape(1, D_MODEL), w_p, b_p.reshape(1, N_PROJ),
      cos_t, sin_t)


def _gates_kernel(g_ref, par_ref, o_ref):
    x = g_ref[...]
    alog = par_ref[0:8, 0:1]
    dtb = par_ref[0:8, 1:2]
    fb = par_ref[24:32, 1:2]
    g = -jnp.exp(alog) * _softplus(x[0:8] + dtb)
    beta = _sigmoid(x[8:16])
    ig = x[16:24]
    lf = -_softplus(-(x[24:32] + fb))
    tl = x.shape[1]
    r = lax.broadcasted_iota(jnp.int32, (LANES, LANES), 0)
    c = lax.broadcasted_iota(jnp.int32, (LANES, LANES), 1)
    same = (r // CHUNK) == (c // CHUNK)
    pre = jnp.where(same & (r <= c), 1.0, 0.0).astype(BF16)
    suf = jnp.where(same & (r >= c), 1.0, 0.0).astype(BF16)
    tot = jnp.where(same, 1.0, 0.0).astype(BF16)
    m01 = jnp.concatenate([pre, suf, tot], axis=1)
    fwd = (lax.broadcasted_iota(jnp.int32, (16, LANES), 0) % 8) < 4
    o_ref[16:24, :] = beta
    o_ref[24:32, :] = ig
    for j in range(tl // LANES):
        sl = slice(j * LANES, (j + 1) * LANES)
        xs = jnp.concatenate([g[:, sl], lf[:, sl]], axis=0)
        cs = _dot01(xs, m01)
        cum = jnp.where(fwd, cs[:, 0:LANES], cs[:, LANES:2 * LANES])
        total = cs[:, 2 * LANES:3 * LANES]
        o_ref[0:8, sl] = cum[0:8]
        o_ref[8:16, sl] = total[0:8]
        o_ref[32:40, sl] = cum[8:16]
        o_ref[40:48, sl] = total[8:16]


def _gate_prep(gates_t, par):
    t_rows = gates_t.shape[1]
    tl = math.gcd(t_rows, 2048)
    return pl.pallas_call(
        _gates_kernel,
        grid=(t_rows // tl,),
        in_specs=[pl.BlockSpec((32, tl), lambda i: (0, i)),
                  pl.BlockSpec((32, LANES), lambda i: (0, 0))],
        out_specs=pl.BlockSpec((48, tl), lambda i: (0, i)),
        out_shape=jax.ShapeDtypeStruct((48, t_rows), F32),
        compiler_params=_cparams(("parallel",)),
        name="gate_prep",
    )(gates_t, par)


def _conv_kernel(x_ref, p_ref, n_ref, w_ref, o_ref, ext_s, *, n_ctx_tiles, tiles_per_seq):
    i = pl.program_id(0)
    j = (i - n_ctx_tiles) % tiles_per_seq
    is_ctx = i < n_ctx_tiles
    first = jnp.logical_or(is_ctx, j == 0)
    last = jnp.logical_or(is_ctx, j == tiles_per_seq - 1)
    ext_s[0:HALO, :] = jnp.where(first, 0.0, p_ref[...])
    ext_s[HALO:HALO + ROW_TILE, :] = x_ref[...]
    ext_s[HALO + ROW_TILE:, :] = jnp.where(last, 0.0, n_ref[...])
    w = w_ref[...]
    y = None
    for k in range(CONV_K):
        off = HALO - CONV_K // 2 + k
        term = ext_s[off:off + ROW_TILE, :] * w[k:k + 1, :]
        y = term if y is None else y + term
    y = y * _sigmoid(y)
    ones = _seg_ones(HW, HEAD_DIM)
    q = y[:, 0:HW]
    k_ = y[:, HW:2 * HW]
    o_ref[:, 0:HW] = q * lax.rsqrt(_dot01(q * q, ones) + EPS) * (HEAD_DIM ** -0.5)
    o_ref[:, HW:2 * HW] = k_ * lax.rsqrt(_dot01(k_ * k_, ones) + EPS)
    o_ref[:, 2 * HW:3 * HW] = y[:, 2 * HW:3 * HW]


def _delta_conv(bqkv, conv_w, n_ctx_tiles, tiles_per_seq):
    t_rows, width = bqkv.shape
    n_tiles = t_rows // ROW_TILE
    hb = ROW_TILE // HALO
    n_hb = t_rows // HALO
    return pl.pallas_call(
        functools.partial(_conv_kernel, n_ctx_tiles=n_ctx_tiles, tiles_per_seq=tiles_per_seq),
        grid=(n_tiles,),
        in_specs=[pl.BlockSpec((ROW_TILE, width), lambda i: (i, 0)),
                  pl.BlockSpec((HALO, width), lambda i: (jnp.maximum(i * hb - 1, 0), 0)),
                  pl.BlockSpec((HALO, width), lambda i: (jnp.minimum((i + 1) * hb, n_hb - 1), 0)),
                  pl.BlockSpec((CONV_K, width), lambda i: (0, 0))],
        out_specs=pl.BlockSpec((ROW_TILE, width), lambda i: (i, 0)),
        out_shape=jax.ShapeDtypeStruct((t_rows, width), F32),
        scratch_shapes=[pltpu.VMEM((ROW_TILE + 2 * HALO, width), F32)],
        compiler_params=_cparams(("parallel",)),
        name="delta_conv",
    )(bqkv, bqkv, bqkv, conv_w)


def _attn_kernel(lqk_ref, q_ref, k_ref, vt_ref, o_ref, *, lam_init, tq, tk):
    q = q_ref[...]
    lane = lax.broadcasted_iota(jnp.int32, q.shape, 1)
    zero = jnp.zeros_like(q)
    q2 = jnp.concatenate([jnp.where(lane < A_QK_DIM, q, zero), jnp.where(lane >= A_QK_DIM, q, zero)], axis=0)
    def scores(j):
        return lax.dot_general(k_ref[j * tk:(j + 1) * tk, :], q2, (((1,), (1,)), ((), ())),
                               preferred_element_type=F32)

    m = acc = None
    n_kv = k_ref.shape[0] // tk
    st_next = scores(0)
    for j in range(n_kv):
        st = st_next
        if j + 1 < n_kv:
            st_next = scores(j + 1)
        m_new = jnp.max(st, axis=0, keepdims=True)
        if j > 0:
            m_new = jnp.maximum(m, m_new)
        p = jnp.exp(st - m_new)
        pv = jnp.dot(vt_ref[:, j * tk:(j + 1) * tk], p.astype(BF16), preferred_element_type=F32)
        acc = pv if j == 0 else jnp.exp(m - m_new) * acc + pv
        m = m_new
    o = acc[0:A_V_DIM] / acc[A_V_DIM:A_V_DIM + 1]
    lq = lqk_ref[...]
    lam = (jnp.exp(jnp.sum(lq[0:1] * lq[1:2], axis=-1, keepdims=True))
           - jnp.exp(jnp.sum(lq[2:3] * lq[3:4], axis=-1, keepdims=True)) + lam_init)
    o_ref[...] = (o[:, 0:tq] - lam * o[:, tq:2 * tq]).T


def _diff_attention(lqk, qk, k_all, v_all, lam_init, q_row0, n_seq, lq, tq, tk):
    lk = k_all.shape[1]
    qb0 = q_row0 // tq
    nq = lq // tq
    vt_all = jnp.concatenate([v_all.reshape(n_seq, lk, HEADS, A_V_DIM).transpose(0, 2, 3, 1),
                              jnp.ones((n_seq, HEADS, BF16_ROWS, lk), BF16)], axis=2)
    return pl.pallas_call(
        functools.partial(_attn_kernel, lam_init=lam_init, tq=tq, tk=tk),
        grid=(n_seq, HEADS, nq),
        in_specs=[pl.BlockSpec((4, A_QK_DIM), lambda b, h, i: (0, 0)),
                  pl.BlockSpec((tq, LANES), lambda b, h, i: (qb0 + b * nq + i, h)),
                  pl.BlockSpec((None, lk, LANES), lambda b, h, i: (b, 0, h)),
                  pl.BlockSpec((None, None, A_V_DIM + BF16_ROWS, lk), lambda b, h, i: (b, h, 0, 0))],
        out_specs=pl.BlockSpec((tq, LANES), lambda b, h, i: (b * nq + i, h)),
        out_shape=jax.ShapeDtypeStruct((n_seq * lq, HEADS * A_V_DIM), F32),
        compiler_params=_cparams(("parallel", "parallel", "parallel")),
        name="diff_attention",
    )(lqk, qk, k_all, vt_all)


def _stack4(x):
    return jnp.concatenate([x, x, x, x], axis=0)


def _collapse(x):
    return (x[0:CHUNK] + x[CHUNK:2 * CHUNK]) + (x[2 * CHUNK:3 * CHUNK] + x[3 * CHUNK:4 * CHUNK])


def _block_masks(direction):
    r = lax.broadcasted_iota(jnp.int32, (HW, HW), 0)
    c = lax.broadcasted_iota(jnp.int32, (HW, HW), 1)
    same = (r // CHUNK) == (c // CHUNK)
    diff = ((r % CHUNK) - (c % CHUNK)) * (1 - 2 * direction)
    return same, jnp.logical_and(same, diff >= 0), jnp.logical_and(same, diff > 0)


def _unit_tri_inverse(mats):
    r = lax.broadcasted_iota(jnp.int32, (HW, HW), 0)
    c = lax.broadcasted_iota(jnp.int32, (HW, HW), 1)
    b16 = (r // 16) == (c // 16)
    b32 = (r // 32) == (c // 32)
    eye = jnp.where(r == c, 1.0, 0.0)
    a16 = [jnp.where(b16, a, 0.0) for a in mats]
    ts = [eye - a for a in a16]
    xs = [a.astype(BF16) for a in a16]
    for _ in range(3):
        xs = [jnp.dot(x, x, preferred_element_type=F32).astype(BF16) for x in xs]
        ts = [t + _bdot(t, x) for t, x in zip(ts, xs)]
    for lo_blk, hi_blk in ((b16, b32), (b32, None)):
        inside = jnp.logical_not(lo_blk) if hi_blk is None else jnp.logical_and(hi_blk, jnp.logical_not(lo_blk))
        offs = [jnp.where(inside, a, 0.0) for a in mats]
        tb = [t.astype(BF16) for t in ts]
        mids = [_bdot(t, o) for t, o in zip(tb, offs)]
        ts = [t - _bdot(m, t16) for t, m, t16 in zip(ts, mids, tb)]
    return ts


def _scan_job(n_ctx_tiles, tiles_per_seq):
    j = pl.program_id(0)
    t = (j - n_ctx_tiles) % tiles_per_seq
    is_ctx = j < n_ctx_tiles
    return jnp.logical_or(is_ctx, t == 0), jnp.logical_or(is_ctx, t == tiles_per_seq - 1)


def _delta_prep(chunks):
    masks = {d: _block_masks(d) for d in (0, 1)}
    n = len(chunks)
    same = masks[0][0]
    dirs = [ch[5] for ch in chunks]
    cols = [ch[3] for ch in chunks]
    rows = [ch[4] for ch in chunks]
    km = [jnp.where(same, _stack4(ch[1]), 0.0) for ch in chunks]
    kbm = [km[i] * cols[i][:, 2:3] for i in range(n)]
    qm = [jnp.where(same, _stack4(ch[0]), 0.0) for ch in chunks]
    kk = [_bdot_nt(jnp.concatenate([kbm[i], qm[i]], axis=0), _stack4(chunks[i][1])) for i in range(n)]
    e_incl = [jnp.exp(jnp.where(masks[dirs[i]][1], cols[i][:, 0:1] - rows[i][0:1, :], -jnp.inf)) for i in range(n)]
    a = [jnp.where(masks[dirs[i]][2], kk[i][0:HW] * e_incl[i], 0.0) for i in range(n)]
    aqk = [kk[i][HW:2 * HW] * e_incl[i] for i in range(n)]
    tinv = _unit_tri_inverse(a)
    egc = [jnp.exp(cols[i][:, 0:1]) for i in range(n)]
    vbm = [jnp.where(same, _stack4(chunks[i][2]) * cols[i][:, 2:3], 0.0) for i in range(n)]
    uw = [_bdot(tinv[i], jnp.concatenate([vbm[i], kbm[i] * egc[i]], axis=1)) for i in range(n)]
    kd_t = [km[i].T * jnp.exp(rows[i][1:2, :] - rows[i][0:1, :]) for i in range(n)]
    return [(uw[i][:, 0:HW], uw[i][:, HW:2 * HW], qm[i] * egc[i], kd_t[i], aqk[i], jnp.exp(cols[i][:, 1:2]))
            for i in range(n)]


def _delta_step(pre, s_ref, o_ref, c):
    u, w, qd, kd_t, aqk, gl = pre
    s = s_ref[...]
    wq = _bdot(jnp.concatenate([w, qd], axis=0), s)
    v_new = u - wq[0:HW]
    o_ref[c * CHUNK:(c + 1) * CHUNK, :] = _collapse(wq[HW:2 * HW] + _bdot(aqk, v_new))
    s_ref[...] = s * gl + _bdot(kd_t, v_new)


def _delta_kernel(qf, kf, vf, qb, kb, vb, colf, rowf, colb, rowb, s0_ref, of_ref, ob_ref, sout_ref,
                  sf_s, sb_s, *, n_ctx_tiles, tiles_per_seq):
    first, last = _scan_job(n_ctx_tiles, tiles_per_seq)

    @pl.when(first)
    def _():
        sf_s[...] = s0_ref[0]
        sb_s[...] = s0_ref[1]

    def chunk(q, k, v, col, row, c, direction):
        rows = slice(c * CHUNK, (c + 1) * CHUNK)
        return (q[rows, :], k[rows, :], v[rows, :], col[c], row[c], direction)

    pre = _delta_prep([chunk(qf, kf, vf, colf, rowf, c, 0) for c in range(CHUNKS_PER_TILE)]
                      + [chunk(qb, kb, vb, colb, rowb, c, 1) for c in range(CHUNKS_PER_TILE)])
    for i in range(CHUNKS_PER_TILE):
        cb = CHUNKS_PER_TILE - 1 - i
        _delta_step(pre[i], sf_s, of_ref, i)
        _delta_step(pre[CHUNKS_PER_TILE + cb], sb_s, ob_ref, cb)

    @pl.when(last)
    def _():
        sout_ref[0] = sf_s[...]
        sout_ref[1] = sb_s[...]


def _head_cols():
    r = lax.broadcasted_iota(jnp.int32, (HW, LANES), 0) // CHUNK
    c = lax.broadcasted_iota(jnp.int32, (HW, LANES), 1)
    return r == c


def _mlstm_prep(chunks):
    masks = {d: _block_masks(d) for d in (0, 1)}
    same = masks[0][0]
    n = len(chunks)
    ig = [ch[3][:, 3:4] for ch in chunks]
    b = [ch[3][:, 4:5] for ch in chunks]
    btot = [ch[3][:, 5:6] for ch in chunks]
    ig_row = [ch[4][3:4, :] for ch in chunks]
    b_row = [ch[4][4:5, :] for ch in chunks]
    btot_row = [ch[4][5:6, :] for ch in chunks]
    k4 = [_stack4(ch[1]) * (HEAD_DIM ** -0.5) for ch in chunks]
    qm = [jnp.where(same, _stack4(ch[0]), 0.0) for ch in chunks]
    qk = [_bdot_nt(qm[i], k4[i]) for i in range(n)]
    dmat = [jnp.where(masks[chunks[i][5]][1], b[i] + (ig_row[i] - b_row[i]), -jnp.inf) for i in range(n)]
    dmax = [jnp.max(d, axis=-1, keepdims=True) for d in dmat]
    k_t = [jnp.where(same, k, 0.0).T for k in k4]
    kvl_row = [btot_row[i] - b_row[i] + ig_row[i] for i in range(n)]
    kvl_col = [btot[i] - b[i] + ig[i] for i in range(n)]
    mkv_row = [jnp.max(jnp.where(same, kv, -jnp.inf), axis=0, keepdims=True) for kv in kvl_col]
    mkv_col = [jnp.max(jnp.where(same, kv, -jnp.inf), axis=1, keepdims=True) for kv in kvl_row]
    ones_cols = jnp.where(_head_cols(), 1.0, 0.0)
    vaug = [jnp.concatenate([jnp.where(same, _stack4(ch[2]), 0.0), ones_cols], axis=1) for ch in chunks]
    return [(qm[i], k_t[i], vaug[i], b[i], btot[i], btot_row[i], dmat[i], dmax[i], qk[i], kvl_row[i],
             mkv_col[i], mkv_row[i]) for i in range(n)]


def _mlstm_step(pre, cn_ref, mcol_ref, mrow_ref, o_ref, c):
    qm, k_t, vaug, b, btot, btot_row, dmat, dmax, qk, kvl_row, mkv_col, mkv_row = pre
    cn = cn_ref[...]
    ms = mcol_ref[...]
    inter = b + ms
    m_t = jnp.maximum(inter, dmax)
    s = qk * jnp.exp(dmat - m_t)
    nd = jnp.exp(inter - m_t) * _bdot(qm, cn) + _bdot(s, vaug)
    den = jnp.sum(nd[:, HW:HW + LANES], axis=-1, keepdims=True)
    hval = nd[:, 0:HW] / jnp.maximum(jnp.abs(den), jnp.exp(-m_t))
    o_ref[c * CHUNK:(c + 1) * CHUNK, :] = _collapse(hval)
    m_new = jnp.maximum(btot + ms, mkv_col)
    m_new_row = jnp.maximum(btot_row + mrow_ref[...], mkv_row)
    cn_ref[...] = jnp.exp(btot + ms - m_new) * cn + _bdot(k_t * jnp.exp(kvl_row - m_new_row), vaug)
    mcol_ref[...] = m_new
    mrow_ref[...] = m_new_row


def _mlstm_kernel(qf, kf, vf, qb, kb, vb, colf, rowf, colb, rowb, cn0_ref, mc0_ref, mr0_ref,
                  of_ref, ob_ref, cnout_ref, mout_ref, cnf_s, cnb_s, mcf_s, mcb_s, mrf_s, mrb_s,
                  *, n_ctx_tiles, tiles_per_seq):
    first, last = _scan_job(n_ctx_tiles, tiles_per_seq)

    @pl.when(first)
    def _():
        cnf_s[...] = cn0_ref[0]
        cnb_s[...] = cn0_ref[1]
        mcf_s[...] = mc0_ref[0]
        mcb_s[...] = mc0_ref[1]
        mrf_s[...] = mr0_ref[0]
        mrb_s[...] = mr0_ref[1]

    def chunk(q, k, v, col, row, c, direction):
        rows = slice(c * CHUNK, (c + 1) * CHUNK)
        return (q[rows, :], k[rows, :], v[rows, :], col[c], row[c], direction)

    pre = _mlstm_prep([chunk(qf, kf, vf, colf, rowf, c, 0) for c in range(CHUNKS_PER_TILE)]
                      + [chunk(qb, kb, vb, colb, rowb, c, 1) for c in range(CHUNKS_PER_TILE)])
    for i in range(CHUNKS_PER_TILE):
        cb = CHUNKS_PER_TILE - 1 - i
        _mlstm_step(pre[i], cnf_s, mcf_s, mrf_s, of_ref, i)
        _mlstm_step(pre[CHUNKS_PER_TILE + cb], cnb_s, mcb_s, mrb_s, ob_ref, cb)

    @pl.when(last)
    def _():
        cnout_ref[0] = cnf_s[...]
        cnout_ref[1] = cnb_s[...]
        mout_ref[0] = jnp.broadcast_to(mcf_s[...], (HW, LANES))
        mout_ref[1] = jnp.broadcast_to(mcb_s[...], (HW, LANES))


def _scan_call(kernel_fn, name, src, col_blocks, colg, rowg, states, state_out_shapes, scratch,
               n_ctx_tiles, tiles_per_seq):
    t_rows = src.shape[0]
    n_tiles = t_rows // ROW_TILE

    def seq_of(j):
        return jnp.where(j < n_ctx_tiles, j, n_ctx_tiles + (j - n_ctx_tiles) // tiles_per_seq)

    def back(j):
        jj = j - n_ctx_tiles
        mirrored = n_ctx_tiles + (jj // tiles_per_seq) * tiles_per_seq + tiles_per_seq - 1 - jj % tiles_per_seq
        return jnp.where(j < n_ctx_tiles, j, mirrored)

    def tile_spec(cb, bwd):
        return pl.BlockSpec((ROW_TILE, HW), (lambda j: (back(j), cb)) if bwd else (lambda j: (j, cb)))

    def gate_spec(shape, d):
        return pl.BlockSpec((None, CHUNKS_PER_TILE) + shape,
                            (lambda j: (1, back(j), 0, 0)) if d else (lambda j: (0, j, 0, 0)))

    def state_spec(shape):
        return pl.BlockSpec((None,) + tuple(shape), lambda j: (seq_of(j),) + (0,) * len(shape))

    in_specs = ([tile_spec(cb, False) for cb in col_blocks] + [tile_spec(cb, True) for cb in col_blocks]
                + [gate_spec((HW, 8), 0), gate_spec((8, HW), 0), gate_spec((HW, 8), 1), gate_spec((8, HW), 1)]
                + [state_spec(s.shape[1:]) for s in states])
    out_specs = ([pl.BlockSpec((ROW_TILE, HW), lambda j: (j, 0)),
                  pl.BlockSpec((ROW_TILE, HW), lambda j: (back(j), 0))]
                 + [state_spec(s[1:]) for s in state_out_shapes])
    out_shape = ([jax.ShapeDtypeStruct((t_rows, HW), F32)] * 2
                 + [jax.ShapeDtypeStruct(tuple(s), F32) for s in state_out_shapes])
    return pl.pallas_call(
        functools.partial(kernel_fn, n_ctx_tiles=n_ctx_tiles, tiles_per_seq=tiles_per_seq),
        grid=(n_tiles,),
        in_specs=in_specs, out_specs=out_specs, out_shape=out_shape,
        scratch_shapes=scratch,
        compiler_params=_cparams(("arbitrary",)),
        name=name,
    )(*([src] * (2 * len(col_blocks))), colg, rowg, colg, rowg, *states)


def _delta_scan(dqkv, colg, rowg, s0, n_ctx_tiles, tiles_per_seq):
    return _scan_call(_delta_kernel, "delta_scan", dqkv, (0, 1, 2), colg, rowg, [s0], [s0.shape],
                      [pltpu.VMEM((HW, HW), F32)] * 2, n_ctx_tiles, tiles_per_seq)


def _mlstm_scan(bgc, colg, rowg, cn0, mc0, mr0, n_ctx_tiles, tiles_per_seq):
    n_seq = cn0.shape[0]
    return _scan_call(_mlstm_kernel, "mlstm_scan", bgc, (1, 2, 3), colg, rowg, [cn0, mc0, mr0],
                      [cn0.shape, (n_seq, N_DIR, HW, LANES)],
                      [pltpu.VMEM((HW, HW + LANES), F32)] * 2 + [pltpu.VMEM((HW, 1), F32)] * 2
                      + [pltpu.VMEM((1, HW), F32)] * 2, n_ctx_tiles, tiles_per_seq)


def _merge_ffn_kernel(x_ref, mod_ref, actx_ref, alat_ref, odf_ref, odb_ref, hmf_ref, hmb_ref, bg_ref, co_ref,
                      ag_ref, dg_ref, mg_ref, w_ref, g2_ref, wgu_ref, wdn_ref, fg_ref, o_ref,
                      *, lam_init, n_ctx_tiles, final):
    a = jnp.where(pl.program_id(0) < n_ctx_tiles, actx_ref[...], alat_ref[...])
    a = a * lax.rsqrt(_dot01(a * a, _seg_ones(HEADS * A_V_DIM, A_V_DIM)) * (1.0 / A_V_DIM) + EPS)
    a = a * ag_ref[...] * (1.0 - lam_init)
    ones = _seg_ones(HW, HEAD_DIM)
    od = odf_ref[...] + odb_ref[...]
    bg = bg_ref[...]
    d = od * lax.rsqrt(_dot01(od * od, ones) * (1.0 / HEAD_DIM) + EPS) * dg_ref[...] * (bg * _sigmoid(bg))
    hm = hmf_ref[...] + hmb_ref[...]
    m = _sigmoid(co_ref[...]) * (hm * lax.rsqrt(_dot01(hm * hm, ones) * (1.0 / HEAD_DIM) + EPS) * mg_ref[...])
    cat = jnp.concatenate([a.astype(BF16), d.astype(BF16), m.astype(BF16)], axis=1)
    mix = jnp.dot(cat, w_ref[...], preferred_element_type=F32)
    x = x_ref[...] + mod_ref[:, 2 * D_MODEL:3 * D_MODEL] * mix
    y = x * lax.rsqrt(jnp.mean(x * x, axis=-1, keepdims=True) + EPS) * g2_ref[...]
    u = y * (1.0 + mod_ref[:, 4 * D_MODEL:5 * D_MODEL]) + mod_ref[:, 3 * D_MODEL:4 * D_MODEL]
    h = jnp.dot(u.astype(BF16), wgu_ref[...], preferred_element_type=F32)
    gate = h[:, 0:FFN_HIDDEN]
    act = (gate * _sigmoid(gate)) * h[:, FFN_HIDDEN:2 * FFN_HIDDEN]
    out = x + mod_ref[:, 5 * D_MODEL:6 * D_MODEL] * jnp.dot(act.astype(BF16), wdn_ref[...],
                                                             preferred_element_type=F32)
    if final:
        out = out * lax.rsqrt(jnp.mean(out * out, axis=-1, keepdims=True) + EPS) * fg_ref[...]
    o_ref[...] = out


def _merge_ffn(x_all, mod_l, a_ctx, a_lat, od, hm, bgc, attn_g, delta_g, mlstm_g, w_out, lam_init,
               g2, w_gu, w_dn, final_g, final, n_ctx_tiles, tiles_per_seq):
    t_rows = x_all.shape[0]

    def mod_idx(i):
        return (jnp.where(i < n_ctx_tiles, 0, 1 + (i - n_ctx_tiles) // tiles_per_seq), 0, 0)

    full = lambda r, c: pl.BlockSpec((r, c), lambda i: (0, 0), pipeline_mode=pl.Buffered(1))
    row = lambda w, cb=0: pl.BlockSpec((ROW_TILE, w), lambda i: (i, cb))
    a_w = HEADS * A_V_DIM
    return pl.pallas_call(
        functools.partial(_merge_ffn_kernel, lam_init=lam_init, n_ctx_tiles=n_ctx_tiles, final=final),
        grid=(t_rows // ROW_TILE,),
        in_specs=[row(D_MODEL),
                  pl.BlockSpec((None, 1, 6 * D_MODEL), mod_idx),
                  pl.BlockSpec((ROW_TILE, a_w), lambda i: (jnp.minimum(i, n_ctx_tiles - 1), 0)),
                  pl.BlockSpec((ROW_TILE, a_w), lambda i: (jnp.maximum(i - n_ctx_tiles, 0), 0)),
                  row(HW), row(HW), row(HW), row(HW), row(HW), row(HW, 4),
                  full(1, a_w), full(1, HW), full(1, HW), full(D_MODEL, D_MODEL),
                  full(1, D_MODEL), full(D_MODEL, 2 * FFN_HIDDEN), full(FFN_HIDDEN, D_MODEL),
                  full(1, D_MODEL)],
        out_specs=row(D_MODEL),
        out_shape=jax.ShapeDtypeStruct((t_rows, D_MODEL), F32),
        compiler_params=_cparams(("parallel",)),
        name="merge_ffn",
    )(x_all, mod_l.reshape(8, 1, 6 * D_MODEL), a_ctx, a_lat, od[0], od[1], hm[0], hm[1], bgc, bgc,
      jnp.tile(attn_g, HEADS).reshape(1, -1), jnp.tile(delta_g, HEADS).reshape(1, -1),
      jnp.tile(mlstm_g, HEADS).reshape(1, -1), w_out, g2.reshape(1, D_MODEL), w_gu, w_dn,
      final_g.reshape(1, D_MODEL))


def _rope_tables(dec_seq):
    n_rows = dec_seq // GRID_W
    rows = jnp.repeat(jnp.arange(n_rows, dtype=F32), GRID_W)
    cols = jnp.tile(jnp.arange(GRID_W, dtype=F32), n_rows)
    n_freq = A_QK_DIM // 4
    inv_freq = ROPE_BASE ** (-jnp.arange(n_freq, dtype=F32) / n_freq)
    ang = jnp.concatenate([rows[:, None] * inv_freq, cols[:, None] * inv_freq], axis=-1)
    cos = jnp.repeat(jnp.cos(ang), 2, axis=-1)
    sin = jnp.repeat(jnp.sin(ang), 2, axis=-1) * jnp.tile(jnp.array([-1.0, 1.0], F32), A_QK_DIM // 2)
    cos = jnp.concatenate([jnp.ones((ROW_TILE, A_QK_DIM), F32), cos], axis=0)
    sin = jnp.concatenate([jnp.zeros((ROW_TILE, A_QK_DIM), F32), sin], axis=0)
    return jnp.tile(cos, (1, 2)), jnp.tile(sin, (1, 2))


def _permute_proj(w_in_l, b_in_l):
    def perm(a):
        head, ba_bb, tail, ci_cf = a[..., 0:2560], a[..., 2560:2576], a[..., 2576:3600], a[..., 3600:3616]
        pad = jnp.zeros(a.shape[:-1] + (N_PROJ - 3616,), a.dtype)
        return jnp.concatenate([head, tail, ba_bb, ci_cf, pad], axis=-1)
    return perm(w_in_l).astype(BF16), perm(b_in_l)


def _block_diag(s):
    eye = jnp.eye(HEADS, dtype=s.dtype)
    out = s[..., :, :, None, :] * eye[:, None, :, None]
    return out.reshape(s.shape[:-3] + (HW, HW))


def _block_diag_inv(s_bd):
    s6 = s_bd.reshape(s_bd.shape[:-2] + (HEADS, HEAD_DIM, HEADS, HEAD_DIM))
    return jnp.stack([s6[..., h, :, h, :] for h in range(HEADS)], axis=-3)


def _norm_cols(n):
    eye = jnp.eye(HEADS, LANES, dtype=n.dtype)
    return (n[..., None] * eye[:, None, :]).reshape(n.shape[:-2] + (HW, LANES))


def _with_zero_ctx(batch, lat_state):
    return jnp.concatenate([jnp.zeros((batch,) + lat_state.shape[1:], lat_state.dtype), lat_state], axis=0)


def kernel(x_prompt, x_sample, cache_attn_k, cache_attn_v, state_delta, state_mlstm_C, state_mlstm_n,
           state_mlstm_m, c, c_ctx, norm1_g, norm2_g, w_mod, b_mod, w_in, b_in, w_out, lambda_qk,
           attn_subln_g, delta_conv_w, delta_A_log, delta_dt_bias, delta_norm_g, mlstm_f_bias,
           mlstm_norm_g, w_gate_up, w_down, final_norm_g):
    batch, seq, _ = x_prompt.shape
    dec_batch, dec_seq, _ = x_sample.shape
    past_len = cache_attn_k.shape[2]
    assert seq == ROW_TILE and dec_seq % ROW_TILE == 0 and dec_batch + 1 <= 8
    t_ctx = batch * seq
    t_lat = dec_batch * dec_seq
    t_rows = t_ctx + t_lat
    n_ctx_tiles = t_ctx // ROW_TILE
    tiles_per_seq = dec_seq // ROW_TILE
    n_chunks = t_rows // CHUNK

    x_all = jnp.concatenate([x_prompt.reshape(t_ctx, D_MODEL), x_sample.reshape(t_lat, D_MODEL)], axis=0)
    cvecs = jnp.zeros((8, D_MODEL), F32).at[0].set(c_ctx).at[1:1 + dec_batch].set(c)
    mod = _ada_mod(cvecs, w_mod, b_mod)
    cos_t, sin_t = _rope_tables(dec_seq)

    ks_l, vs_l, sd_l, cm_l, nm_l, mm_l = [], [], [], [], [], []
    for l in range(DEPTH):
        lam_init = 0.8 - 0.6 * math.exp(-0.3 * l)
        w_p, b_p = _permute_proj(w_in[l], b_in[l])
        qk, va, kv32, bqkv, bgc, gates = _in_proj(x_all, mod[l], norm1_g[l], w_p, b_p, cos_t, sin_t,
                                                 n_ctx_tiles, tiles_per_seq)
        ks_l.append(kv32[:t_ctx, 0:512].reshape(batch, seq, HEADS, 2, A_QK_DIM))
        vs_l.append(kv32[:t_ctx, 512:1024].reshape(batch, seq, HEADS, A_V_DIM))

        par = jnp.zeros((32, LANES), F32)
        par = par.at[0:8, 0].set(delta_A_log[l].reshape(-1)).at[0:8, 1].set(delta_dt_bias[l].reshape(-1))
        par = par.at[24:32, 1].set(mlstm_f_bias[l].reshape(-1))
        grow = _gate_prep(gates[:, 0:32].T, par)
        r6 = grow.reshape(6, N_DIR, HEADS, n_chunks, CHUNK)
        colg = jnp.pad(r6.transpose(1, 3, 2, 4, 0).reshape(N_DIR, n_chunks, HW, 6),
                       ((0, 0), (0, 0), (0, 0), (0, 2)))
        rowg = jnp.pad(r6.transpose(1, 3, 0, 2, 4).reshape(N_DIR, n_chunks, 6, HW),
                       ((0, 0), (0, 0), (0, 2), (0, 0)))

        k_ctx = qk[:t_ctx, 512:1024].reshape(batch, seq, 512)
        v_ctx = va[:t_ctx].reshape(batch, seq, 512)
        a_ctx = _diff_attention(lambda_qk[l], qk, k_ctx, v_ctx, lam_init, 0, batch, seq, seq, seq)
        k_lat = jnp.concatenate([qk[t_ctx:, 512:1024].reshape(dec_batch, dec_seq, 512),
                                 cache_attn_k[:, l].reshape(dec_batch, past_len, 512).astype(BF16)], axis=1)
        v_lat = jnp.concatenate([va[t_ctx:].reshape(dec_batch, dec_seq, 512),
                                 cache_attn_v[:, l].reshape(dec_batch, past_len, 512).astype(BF16)], axis=1)
        a_lat = _diff_attention(lambda_qk[l], qk, k_lat, v_lat, lam_init, t_ctx, dec_batch, dec_seq, 512, 512)

        dqkv = _delta_conv(bqkv, delta_conv_w[l], n_ctx_tiles, tiles_per_seq)
        od_f, od_b, s_fin = _delta_scan(dqkv, colg, rowg, _with_zero_ctx(batch, _block_diag(state_delta[:, l])),
                                        n_ctx_tiles, tiles_per_seq)
        sd_l.append(_block_diag_inv(s_fin[:batch]))

        cn_lat = jnp.concatenate([_block_diag(state_mlstm_C[:, l]), _norm_cols(state_mlstm_n[:, l])], axis=-1)
        m_lat = jnp.repeat(state_mlstm_m[:, l], HEAD_DIM, axis=-1)
        hm_f, hm_b, cn_fin, m_fin = _mlstm_scan(
            bgc, colg, rowg, _with_zero_ctx(batch, cn_lat), _with_zero_ctx(batch, m_lat[..., None]),
            _with_zero_ctx(batch, m_lat[:, :, None, :]), n_ctx_tiles, tiles_per_seq)
        cm_l.append(_block_diag_inv(cn_fin[:batch, :, :, 0:HW]))
        n6 = cn_fin[:batch, :, :, HW:HW + HEADS].reshape(batch, N_DIR, HEADS, HEAD_DIM, HEADS)
        nm_l.append(jnp.stack([n6[..., h, :, h] for h in range(HEADS)], axis=-2))
        mm_l.append(m_fin[:batch, :, ::HEAD_DIM, 0])

        x_all = _merge_ffn(x_all, mod[l], a_ctx, a_lat, (od_f, od_b), (hm_f, hm_b), bgc, attn_subln_g[l],
                           delta_norm_g[l], mlstm_norm_g[l], w_out[l].astype(BF16), lam_init, norm2_g[l],
                           w_gate_up[l].astype(BF16), w_down[l].astype(BF16), final_norm_g, l == DEPTH - 1,
                           n_ctx_tiles, tiles_per_seq)

    y_prompt = x_all[:t_ctx].reshape(batch, seq, D_MODEL)
    y_sample = x_all[t_ctx:].reshape(dec_batch, dec_seq, D_MODEL)
    return (y_prompt, y_sample, jnp.stack(ks_l, axis=1), jnp.stack(vs_l, axis=1), jnp.stack(sd_l, axis=1),
            jnp.stack(cm_l, axis=1), jnp.stack(nm_l, axis=1), jnp.stack(mm_l, axis=1))
```

```python
import functools
import math

import jax
import jax.numpy as jnp
from jax import lax
from jax.experimental import pallas as pl
from jax.experimental.pallas import tpu as pltpu

F32 = jnp.float32
BF16 = jnp.bfloat16

D_MODEL = 1024
DEPTH = 2
GRID_W = 64
N_DIR = 2
CHUNK = 64
ROPE_BASE = 10000.0
EPS = 1e-6
HEADS = 4
A_QK_DIM = 64
A_V_DIM = 128
HEAD_DIM = 64
CONV_K = 5
FFN_HIDDEN = 2816
ROW_TILE = 256
CHUNKS_PER_TILE = ROW_TILE // CHUNK
HW = HEADS * HEAD_DIM
LANES = 128
HALO = 8
BF16_ROWS = 16
GATE_GROUP = 32

N_PROJ = 3712
VMEM_LIMIT = 56 * 1024 * 1024


def _cparams(sem):
    return pltpu.CompilerParams(dimension_semantics=sem, vmem_limit_bytes=VMEM_LIMIT)


def _bdot(a, b):
    return jnp.dot(a.astype(BF16), b.astype(BF16), preferred_element_type=F32)


def _bdot_nt(a, b):
    return lax.dot_general(a.astype(BF16), b.astype(BF16), (((1,), (1,)), ((), ())),
                           preferred_element_type=F32)


def _split3(x):
    hi = x.astype(BF16).astype(F32)
    r1 = x - hi
    mid = r1.astype(BF16).astype(F32)
    lo = (r1 - mid).astype(BF16).astype(F32)
    return hi, mid, lo


def _dot01(x, m01):
    d = functools.partial(jnp.dot, preferred_element_type=F32)
    hi, mid, lo = _split3(x)
    return d(hi.astype(BF16), m01) + d(mid.astype(BF16), m01) + d(lo.astype(BF16), m01)


def _seg_ones(n, seg):
    r = lax.broadcasted_iota(jnp.int32, (n, n), 0) // seg
    c = lax.broadcasted_iota(jnp.int32, (n, n), 1) // seg
    return jnp.where(r == c, 1.0, 0.0).astype(BF16)


def _sigmoid(x):
    return 1.0 / (1.0 + jnp.exp(-x))


def _softplus(x):
    return jnp.maximum(x, 0.0) + jnp.log1p(jnp.exp(-jnp.abs(x)))


def _ada_kernel(c_ref, w_ref, b_ref, o_ref):
    c = c_ref[...]
    s = c * _sigmoid(c)
    o_ref[...] = _bdot(s, w_ref[...]) + b_ref[...]


def _ada_mod(cvecs, w_mod, b_mod):
    n_out = w_mod.shape[-1]
    tn = 1024
    return pl.pallas_call(
        _ada_kernel,
        grid=(DEPTH, n_out // tn),
        in_specs=[pl.BlockSpec((8, D_MODEL), lambda l, j: (0, 0)),
                  pl.BlockSpec((None, D_MODEL, tn), lambda l, j: (l, 0, j)),
                  pl.BlockSpec((None, 1, tn), lambda l, j: (l, 0, j))],
        out_specs=pl.BlockSpec((None, 8, tn), lambda l, j: (l, 0, j)),
        out_shape=jax.ShapeDtypeStruct((DEPTH, 8, n_out), F32),
        compiler_params=_cparams(("parallel", "parallel")),
        name="ada_mod",
    )(cvecs, w_mod, b_mod.reshape(DEPTH, 1, n_out))


def _proj_kernel(x_ref, mod_ref, g_ref, w_ref, b_ref, cos_ref, sin_ref,
                 qk_ref, va_ref, kv32_ref, bqkv_ref, bgc_ref, gates_ref):
    x = x_ref[...]
    y = x * lax.rsqrt(jnp.mean(x * x, axis=-1, keepdims=True) + EPS) * g_ref[...]
    u = y * (1.0 + mod_ref[:, D_MODEL:2 * D_MODEL]) + mod_ref[:, 0:D_MODEL]
    acc = _bdot(u, w_ref[...]) + b_ref[...]
    kv32_ref[...] = acc[:, 512:1536]
    va_ref[...] = acc[:, 1024:1536].astype(BF16)
    bqkv_ref[...] = acc[:, 1536:2304]
    bgc_ref[...] = acc[:, 2304:3584]
    gates_ref[...] = acc[:, 3584:3712]
    cos = cos_ref[...]
    sin = sin_ref[...]
    even = (lax.broadcasted_iota(jnp.int32, cos.shape, 1) % 2) == 0
    for j in range(8):
        xj = acc[:, j * LANES:(j + 1) * LANES]
        swapped = jnp.where(even, pltpu.roll(xj, LANES - 1, 1), pltpu.roll(xj, 1, 1))
        r = xj * cos + swapped * sin
        if j < 4:
            r = r * (A_QK_DIM ** -0.5)
        qk_ref[:, j * LANES:(j + 1) * LANES] = r.astype(BF16)


def _in_proj(x_all, mod_l, g1, w_p, b_p, cos_t, sin_t, n_ctx_tiles, tiles_per_seq):
    t_rows = x_all.shape[0]
    n_tiles = t_rows // ROW_TILE

    def mod_idx(i):
        return (jnp.where(i < n_ctx_tiles, 0, 1 + (i - n_ctx_tiles) // tiles_per_seq), 0, 0)

    def rope_idx(i):
        return (jnp.where(i < n_ctx_tiles, 0, 1 + (i - n_ctx_tiles) % tiles_per_seq), 0)

    row = lambda w: pl.BlockSpec((ROW_TILE, w), lambda i: (i, 0))
    outs = [(1024, BF16), (512, BF16), (1024, F32), (768, F32), (1280, F32), (LANES, F32)]
    return pl.pallas_call(
        _proj_kernel,
        grid=(n_tiles,),
        in_specs=[row(D_MODEL),
                  pl.BlockSpec((None, 1, 6 * D_MODEL), mod_idx),
                  pl.BlockSpec((1, D_MODEL), lambda i: (0, 0)),
                  pl.BlockSpec((D_MODEL, N_PROJ), lambda i: (0, 0)),
                  pl.BlockSpec((1, N_PROJ), lambda i: (0, 0)),
                  pl.BlockSpec((ROW_TILE, LANES), rope_idx),
                  pl.BlockSpec((ROW_TILE, LANES), rope_idx)],
        out_specs=[row(w) for w, _ in outs],
        out_shape=[jax.ShapeDtypeStruct((t_rows, w), dt) for w, dt in outs],
        compiler_params=_cparams(("parallel",)),
        name="in_proj",
    )(x_all, mod_l.reshape(8, 1, 6 * D_MODEL), g1.reshape(1, D_MODEL), w_p, b_p.reshape(1, N_PROJ),
      cos_t, sin_t)


GATE_ROWS = 80


def _gates_kernel(g_ref, par_ref, o_ref):
    x = g_ref[...]
    alog = par_ref[0:8, 0:1]
    dtb = par_ref[0:8, 1:2]
    fb = par_ref[24:32, 1:2]
    g = -jnp.exp(alog) * _softplus(x[0:8] + dtb)
    beta = _sigmoid(x[8:16])
    ig = x[16:24]
    lf = -_softplus(-(x[24:32] + fb))
    tl = x.shape[1]
    r = lax.broadcasted_iota(jnp.int32, (LANES, LANES), 0)
    c = lax.broadcasted_iota(jnp.int32, (LANES, LANES), 1)
    same = (r // CHUNK) == (c // CHUNK)
    pre = jnp.where(same & (r <= c), 1.0, 0.0).astype(BF16)
    suf = jnp.where(same & (r >= c), 1.0, 0.0).astype(BF16)
    tot = jnp.where(same, 1.0, 0.0).astype(BF16)
    m01 = jnp.concatenate([pre, suf, tot], axis=1)
    fwd16 = (lax.broadcasted_iota(jnp.int32, (16, LANES), 0) % 8) < 4
    fwd8 = lax.broadcasted_iota(jnp.int32, (8, LANES), 0) < 4
    pos = lax.broadcasted_iota(jnp.int32, (8, LANES), 1) % CHUNK

    def running_max(v):
        pf = sf = v
        s = 1
        while s < CHUNK:
            pf = jnp.maximum(pf, jnp.where(pos >= s, pltpu.roll(pf, s, 1), -jnp.inf))
            sf = jnp.maximum(sf, jnp.where(pos < CHUNK - s, pltpu.roll(sf, LANES - s, 1), -jnp.inf))
            s *= 2
        return pf, sf

    o_ref[16:24, :] = beta
    for j in range(tl // LANES):
        sl = slice(j * LANES, (j + 1) * LANES)
        xs = jnp.concatenate([g[:, sl], lf[:, sl]], axis=0)
        cs = _dot01(xs, m01)
        cum = jnp.where(fwd16, cs[:, 0:LANES], cs[:, LANES:2 * LANES])
        total = cs[:, 2 * LANES:3 * LANES]
        gc, gtot, b, btot = cum[0:8], total[0:8], cum[8:16], total[8:16]
        rr = ig[:, sl] - b
        kvl = btot - b + ig[:, sl]
        r_pf, r_sf = running_max(rr)
        k_pf, k_sf = running_max(kvl)
        o_ref[0:8, sl] = gc
        o_ref[8:16, sl] = gtot - gc
        o_ref[24:32, sl] = gtot
        o_ref[32:40, sl] = b
        o_ref[40:48, sl] = rr
        o_ref[48:56, sl] = jnp.where(fwd8, r_pf, r_sf)
        o_ref[56:64, sl] = kvl
        o_ref[64:72, sl] = btot
        o_ref[72:80, sl] = jnp.maximum(k_pf, k_sf)


def _gate_prep(gates_t, par):
    t_rows = gates_t.shape[1]
    tl = math.gcd(t_rows, 2048)
    return pl.pallas_call(
        _gates_kernel,
        grid=(t_rows // tl,),
        in_specs=[pl.BlockSpec((32, tl), lambda i: (0, i)),
                  pl.BlockSpec((32, LANES), lambda i: (0, 0))],
        out_specs=pl.BlockSpec((GATE_ROWS, tl), lambda i: (0, i)),
        out_shape=jax.ShapeDtypeStruct((GATE_ROWS, t_rows), F32),
        compiler_params=_cparams(("parallel",)),
        name="gate_prep",
    )(gates_t, par)


def _conv_kernel(x_ref, p_ref, n_ref, w_ref, o_ref, ext_s, *, n_ctx_tiles, tiles_per_seq):
    i = pl.program_id(0)
    j = (i - n_ctx_tiles) % tiles_per_seq
    is_ctx = i < n_ctx_tiles
    first = jnp.logical_or(is_ctx, j == 0)
    last = jnp.logical_or(is_ctx, j == tiles_per_seq - 1)
    ext_s[0:HALO, :] = jnp.where(first, 0.0, p_ref[...])
    ext_s[HALO:HALO + ROW_TILE, :] = x_ref[...]
    ext_s[HALO + ROW_TILE:, :] = jnp.where(last, 0.0, n_ref[...])
    w = w_ref[...]
    y = None
    for k in range(CONV_K):
        off = HALO - CONV_K // 2 + k
        term = ext_s[off:off + ROW_TILE, :] * w[k:k + 1, :]
        y = term if y is None else y + term
    y = y * _sigmoid(y)
    ones = _seg_ones(HW, HEAD_DIM)
    q = y[:, 0:HW]
    k_ = y[:, HW:2 * HW]
    o_ref[:, 0:HW] = q * lax.rsqrt(_dot01(q * q, ones) + EPS) * (HEAD_DIM ** -0.5)
    o_ref[:, HW:2 * HW] = k_ * lax.rsqrt(_dot01(k_ * k_, ones) + EPS)
    o_ref[:, 2 * HW:3 * HW] = y[:, 2 * HW:3 * HW]


def _delta_conv(bqkv, conv_w, n_ctx_tiles, tiles_per_seq):
    t_rows, width = bqkv.shape
    n_tiles = t_rows // ROW_TILE
    hb = ROW_TILE // HALO
    n_hb = t_rows // HALO
    return pl.pallas_call(
        functools.partial(_conv_kernel, n_ctx_tiles=n_ctx_tiles, tiles_per_seq=tiles_per_seq),
        grid=(n_tiles,),
        in_specs=[pl.BlockSpec((ROW_TILE, width), lambda i: (i, 0)),
                  pl.BlockSpec((HALO, width), lambda i: (jnp.maximum(i * hb - 1, 0), 0)),
                  pl.BlockSpec((HALO, width), lambda i: (jnp.minimum((i + 1) * hb, n_hb - 1), 0)),
                  pl.BlockSpec((CONV_K, width), lambda i: (0, 0))],
        out_specs=pl.BlockSpec((ROW_TILE, width), lambda i: (i, 0)),
        out_shape=jax.ShapeDtypeStruct((t_rows, width), F32),
        scratch_shapes=[pltpu.VMEM((ROW_TILE + 2 * HALO, width), F32)],
        compiler_params=_cparams(("parallel",)),
        name="delta_conv",
    )(bqkv, bqkv, bqkv, conv_w)


def _attn_kernel(lqk_ref, q_ref, k_ref, vt_ref, o_ref, *, lam_init, tq, tk):
    q = q_ref[...]
    lane = lax.broadcasted_iota(jnp.int32, q.shape, 1)
    zero = jnp.zeros_like(q)
    q2 = jnp.concatenate([jnp.where(lane < A_QK_DIM, q, zero), jnp.where(lane >= A_QK_DIM, q, zero)], axis=0)

    def scores(j):
        return lax.dot_general(k_ref[j * tk:(j + 1) * tk, :], q2, (((1,), (1,)), ((), ())),
                               preferred_element_type=F32)

    m = acc = None
    n_kv = k_ref.shape[0] // tk
    st_next = scores(0)
    for j in range(n_kv):
        st = st_next
        if j + 1 < n_kv:
            st_next = scores(j + 1)
        m_new = jnp.max(st, axis=0, keepdims=True)
        if j > 0:
            m_new = jnp.maximum(m, m_new)
        p = jnp.exp(st - m_new)
        pv = jnp.dot(vt_ref[:, j * tk:(j + 1) * tk], p.astype(BF16), preferred_element_type=F32)
        acc = pv if j == 0 else jnp.exp(m - m_new) * acc + pv
        m = m_new
    o = acc[0:A_V_DIM] / acc[A_V_DIM:A_V_DIM + 1]
    lq = lqk_ref[...]
    lam = (jnp.exp(jnp.sum(lq[0:1] * lq[1:2], axis=-1, keepdims=True))
           - jnp.exp(jnp.sum(lq[2:3] * lq[3:4], axis=-1, keepdims=True)) + lam_init)
    o_ref[...] = (o[:, 0:tq] - lam * o[:, tq:2 * tq]).T


def _diff_attention(lqk, qk, k_all, v_all, lam_init, q_row0, n_seq, lq, tq, tk):
    lk = k_all.shape[1]
    qb0 = q_row0 // tq
    nq = lq // tq
    vt_all = jnp.concatenate([v_all.reshape(n_seq, lk, HEADS, A_V_DIM).transpose(0, 2, 3, 1),
                              jnp.ones((n_seq, HEADS, BF16_ROWS, lk), BF16)], axis=2)
    return pl.pallas_call(
        functools.partial(_attn_kernel, lam_init=lam_init, tq=tq, tk=tk),
        grid=(n_seq, HEADS, nq),
        in_specs=[pl.BlockSpec((4, A_QK_DIM), lambda b, h, i: (0, 0)),
                  pl.BlockSpec((tq, LANES), lambda b, h, i: (qb0 + b * nq + i, h)),
                  pl.BlockSpec((None, lk, LANES), lambda b, h, i: (b, 0, h)),
                  pl.BlockSpec((None, None, A_V_DIM + BF16_ROWS, lk), lambda b, h, i: (b, h, 0, 0))],
        out_specs=pl.BlockSpec((tq, LANES), lambda b, h, i: (b * nq + i, h)),
        out_shape=jax.ShapeDtypeStruct((n_seq * lq, HEADS * A_V_DIM), F32),
        compiler_params=_cparams(("parallel", "parallel", "parallel")),
        name="diff_attention",
    )(lqk, qk, k_all, vt_all)


def _same_head(n_cols=HW):
    r = lax.broadcasted_iota(jnp.int32, (HW, n_cols), 0) // HEAD_DIM
    c = (lax.broadcasted_iota(jnp.int32, (HW, n_cols), 1) % HW) // HEAD_DIM
    return r == c


def _bd(x, same):
    return jnp.where(same, jnp.concatenate([x, x, x, x], axis=0), 0.0)


def _cat_index():
    i = lax.broadcasted_iota(jnp.int32, (CHUNK, HW), 0)
    j = lax.broadcasted_iota(jnp.int32, (CHUNK, HW), 1) % CHUNK
    return i, j


def _scan_masks(direction):
    i, j = _cat_index()
    diff = (i - j) * (1 - 2 * direction)
    return diff >= 0, diff > 0


def _expand_matrix(first_lane, n_q):
    r = lax.broadcasted_iota(jnp.int32, (LANES, n_q * HW), 0)
    c = lax.broadcasted_iota(jnp.int32, (LANES, n_q * HW), 1)
    s = r % GATE_GROUP - first_lane
    hit = (r < 3 * GATE_GROUP) & (s >= 0) & (s < 4 * n_q) & (s // HEADS == c // HW) \
        & (s % HEADS == (c % HW) // HEAD_DIM)
    return jnp.where(hit, 1.0, 0.0).astype(BF16)


def _expand(tok, e3):
    hi, mid, lo = _split3(tok)
    x3 = hi + pltpu.roll(mid, GATE_GROUP, 1) + pltpu.roll(lo, 2 * GATE_GROUP, 1)
    return jnp.dot(x3.astype(BF16), e3, preferred_element_type=F32)


def _head_rows(hd, lane):
    return jnp.concatenate([jnp.broadcast_to(hd[h:h + 1, lane:lane + 1], (HEAD_DIM, 1)) for h in range(HEADS)],
                           axis=0)


def _pad_rows(x):
    return jnp.concatenate([x, jnp.zeros_like(x)], axis=0)


def _unit_tri_inverse(mats, same):
    i, j = _cat_index()
    b16 = (i // 16) == (j // 16)
    b32 = (i // 32) == (j // 32)
    eye = jnp.where(i == j, 1.0, 0.0)
    xs = [jnp.where(b16, a, 0.0) for a in mats]
    ts = [eye - x for x in xs]
    xbd = [_bd(x, same) for x in xs]
    for _ in range(3):
        xs = [_bdot(x, b) for x, b in zip(xs, xbd)]
        xbd = [_bd(x, same) for x in xs]
        ts = [t + _bdot(t, b) for t, b in zip(ts, xbd)]
    for inside in (jnp.logical_and(b32, jnp.logical_not(b16)), jnp.logical_not(b32)):
        mids = [_bdot(t, _bd(jnp.where(inside, a, 0.0), same)) for t, a in zip(ts, mats)]
        ts = [t - _bdot(m, _bd(t, same)) for t, m in zip(ts, mids)]
    return ts


def _scan_job(n_ctx_tiles, tiles_per_seq):
    j = pl.program_id(0)
    t = (j - n_ctx_tiles) % tiles_per_seq
    is_ctx = j < n_ctx_tiles
    return jnp.logical_or(is_ctx, t == 0), jnp.logical_or(is_ctx, t == tiles_per_seq - 1)


def _chunks_of(refs_f, refs_b):
    out = []
    for d, (q, k, v, tok, rows, heads) in enumerate((refs_f, refs_b)):
        for c in range(CHUNKS_PER_TILE):
            sl = slice(c * CHUNK, (c + 1) * CHUNK)
            out.append((q[sl, :], k[sl, :], v[sl, :], tok[sl, :], rows[c], heads[c], d))
    return out


def _delta_prep(chunks):
    same = _same_head()
    masks = [_scan_masks(0), _scan_masks(1)]
    e3 = _expand_matrix(0, 3)
    n = len(chunks)
    ex = [_expand(ch[3], e3) for ch in chunks]
    gc = [e[:, 0:HW] for e in ex]
    kb = [chunks[i][1] * ex[i][:, HW:2 * HW] for i in range(n)]
    kk = [_bdot_nt(jnp.concatenate([kb[i], chunks[i][0]], axis=0), _bd(chunks[i][1], same)) for i in range(n)]
    e_incl = [jnp.exp(jnp.where(masks[chunks[i][6]][0], gc[i] - chunks[i][4][0:1, :], -jnp.inf)) for i in range(n)]
    a = [jnp.where(masks[chunks[i][6]][1], kk[i][0:CHUNK] * e_incl[i], 0.0) for i in range(n)]
    aqk = [kk[i][CHUNK:2 * CHUNK] * e_incl[i] for i in range(n)]
    tinv = _unit_tri_inverse(a, same)
    egc = [jnp.exp(g) for g in gc]
    rhs = [jnp.concatenate([_bd(chunks[i][2] * ex[i][:, HW:2 * HW], same), _bd(kb[i] * egc[i], same)], axis=1)
           for i in range(n)]
    uw = [_bdot(tinv[i], rhs[i]) for i in range(n)]
    kd_t = [_pad_rows(chunks[i][1] * jnp.exp(ex[i][:, 2 * HW:3 * HW])).T for i in range(n)]
    gl = [jnp.exp(_head_rows(chunks[i][5][0:HEADS], CHUNK)) for i in range(n)]
    return [(uw[i][:, 0:HW], uw[i][:, HW:2 * HW], chunks[i][0] * egc[i], kd_t[i], aqk[i], gl[i])
            for i in range(n)]


def _delta_step(pre, s_ref, o_ref, c, same):
    u, w, qd, kd_t, aqk, gl = pre
    s = s_ref[...]
    wq = _bdot(jnp.concatenate([w, qd], axis=0), s)
    v_new = u - wq[0:CHUNK]
    o_ref[c * CHUNK:(c + 1) * CHUNK, :] = wq[CHUNK:2 * CHUNK] + _bdot(aqk, _bd(v_new, same))
    s_ref[...] = s * gl + jnp.where(same, _bdot(kd_t, _pad_rows(v_new)), 0.0)


def _delta_kernel(qf, kf, vf, qb, kb, vb, tokf, rowf, headf, tokb, rowb, headb, s0_ref,
                  of_ref, ob_ref, sout_ref, sf_s, sb_s, *, n_ctx_tiles, tiles_per_seq):
    first, last = _scan_job(n_ctx_tiles, tiles_per_seq)

    @pl.when(first)
    def _():
        sf_s[...] = s0_ref[0]
        sb_s[...] = s0_ref[1]

    pre = _delta_prep(_chunks_of((qf, kf, vf, tokf, rowf, headf), (qb, kb, vb, tokb, rowb, headb)))
    same = _same_head()
    for i in range(CHUNKS_PER_TILE):
        cb = CHUNKS_PER_TILE - 1 - i
        _delta_step(pre[i], sf_s, of_ref, i, same)
        _delta_step(pre[CHUNKS_PER_TILE + cb], sb_s, ob_ref, cb, same)

    @pl.when(last)
    def _():
        sout_ref[0] = sf_s[...]
        sout_ref[1] = sb_s[...]


def _mlstm_prep(chunks):
    same = _same_head()
    masks = [_scan_masks(0), _scan_masks(1)]
    e3 = _expand_matrix(12, 2)
    n = len(chunks)
    ex = [_expand(ch[3], e3) for ch in chunks]
    b = [e[:, 0:HW] for e in ex]
    ks = [ch[1] * (HEAD_DIM ** -0.5) for ch in chunks]
    qk = [_bdot_nt(chunks[i][0], _bd(ks[i], same)) for i in range(n)]
    dmat = [jnp.where(masks[chunks[i][6]][0], b[i] + chunks[i][4][1:2, :], -jnp.inf) for i in range(n)]
    dmax = [ex[i][:, 0:HW] + ex[i][:, HW:2 * HW] for i in range(n)]
    k_t = [_pad_rows(k).T for k in ks]
    ones = jnp.ones((CHUNK, HW), F32)
    vaug = [_pad_rows(jnp.concatenate([ch[2], ones], axis=1)) for ch in chunks]
    vbd = [jnp.concatenate([_bd(ch[2], same), jnp.where(same, 1.0, 0.0)], axis=1) for ch in chunks]
    kvl = [jnp.concatenate([jnp.broadcast_to(ch[5][HEADS + h:HEADS + h + 1, :], (HEAD_DIM, LANES))
                            for h in range(HEADS)], axis=0) for ch in chunks]
    btot_col = [_head_rows(ch[5][HEADS:2 * HEADS], CHUNK) for ch in chunks]
    mkv_col = [_head_rows(ch[5][HEADS:2 * HEADS], CHUNK + 1) for ch in chunks]
    return [(chunks[i][0], k_t[i], vaug[i], vbd[i], b[i], dmat[i], dmax[i], qk[i], kvl[i], btot_col[i],
             mkv_col[i], chunks[i][4][2:3, :], chunks[i][4][3:4, :]) for i in range(n)]


def _mlstm_step(pre, cn_ref, mcol_ref, mrow_ref, o_ref, c, same2):
    q, k_t, vaug, vbd, b, dmat, dmax, qk, kvl, btot_col, mkv_col, btot_row, mkv_row = pre
    cn = cn_ref[...]
    ms_col = mcol_ref[...]
    ms_row = mrow_ref[...]
    inter = b + ms_row
    m_t = jnp.maximum(inter, dmax)
    s = qk * jnp.exp(dmat - m_t)
    w_inter = jnp.exp(inter - m_t)
    nd = jnp.concatenate([w_inter, w_inter], axis=1) * _bdot(q, cn) + _bdot(s, vbd)
    o_ref[c * CHUNK:(c + 1) * CHUNK, :] = nd[:, 0:HW] / jnp.maximum(jnp.abs(nd[:, HW:2 * HW]), jnp.exp(-m_t))
    m_new = jnp.maximum(btot_col + ms_col, mkv_col)
    lane = lax.broadcasted_iota(jnp.int32, kvl.shape, 1)
    wk = jnp.where(lane < CHUNK, jnp.exp(kvl - m_new), 0.0)
    cn_ref[...] = jnp.exp(btot_col + ms_col - m_new) * cn + jnp.where(same2, _bdot(k_t * wk, vaug), 0.0)
    mcol_ref[...] = m_new
    mrow_ref[...] = jnp.maximum(btot_row + ms_row, mkv_row)


def _mlstm_kernel(qf, kf, vf, qb, kb, vb, tokf, rowf, headf, tokb, rowb, headb, cn0_ref, mc0_ref, mr0_ref,
                  of_ref, ob_ref, cnout_ref, mout_ref, cnf_s, cnb_s, mcf_s, mcb_s, mrf_s, mrb_s,
                  *, n_ctx_tiles, tiles_per_seq):
    first, last = _scan_job(n_ctx_tiles, tiles_per_seq)

    @pl.when(first)
    def _():
        cnf_s[...] = cn0_ref[0]
        cnb_s[...] = cn0_ref[1]
        mcf_s[...] = mc0_ref[0]
        mcb_s[...] = mc0_ref[1]
        mrf_s[...] = mr0_ref[0]
        mrb_s[...] = mr0_ref[1]

    pre = _mlstm_prep(_chunks_of((qf, kf, vf, tokf, rowf, headf), (qb, kb, vb, tokb, rowb, headb)))
    same2 = _same_head(2 * HW)
    for i in range(CHUNKS_PER_TILE):
        cb = CHUNKS_PER_TILE - 1 - i
        _mlstm_step(pre[i], cnf_s, mcf_s, mrf_s, of_ref, i, same2)
        _mlstm_step(pre[CHUNKS_PER_TILE + cb], cnb_s, mcb_s, mrb_s, ob_ref, cb, same2)

    @pl.when(last)
    def _():
        cnout_ref[0] = cnf_s[...]
        cnout_ref[1] = cnb_s[...]
        mout_ref[0] = jnp.broadcast_to(mcf_s[...], (HW, LANES))
        mout_ref[1] = jnp.broadcast_to(mcb_s[...], (HW, LANES))


def _scan_call(kernel_fn, name, src, col_blocks, gtok, grow, ghead, states, state_out_shapes, scratch,
               n_ctx_tiles, tiles_per_seq):
    t_rows = src.shape[0]
    n_tiles = t_rows // ROW_TILE

    def seq_of(j):
        return jnp.where(j < n_ctx_tiles, j, n_ctx_tiles + (j - n_ctx_tiles) // tiles_per_seq)

    def back(j):
        jj = j - n_ctx_tiles
        mirrored = n_ctx_tiles + (jj // tiles_per_seq) * tiles_per_seq + tiles_per_seq - 1 - jj % tiles_per_seq
        return jnp.where(j < n_ctx_tiles, j, mirrored)

    def tile_spec(cb, bwd):
        return pl.BlockSpec((ROW_TILE, HW), (lambda j: (back(j), cb)) if bwd else (lambda j: (j, cb)))

    def gate_specs(d):
        tile = (lambda j: back(j)) if d else (lambda j: j)
        return [pl.BlockSpec((None, ROW_TILE, LANES), lambda j: (d, tile(j), 0)),
                pl.BlockSpec((None, CHUNKS_PER_TILE, 8, HW), lambda j: (d, tile(j), 0, 0)),
                pl.BlockSpec((None, CHUNKS_PER_TILE, 8, LANES), lambda j: (d, tile(j), 0, 0))]

    def state_spec(shape):
        return pl.BlockSpec((None,) + tuple(shape), lambda j: (seq_of(j),) + (0,) * len(shape))

    in_specs = ([tile_spec(cb, False) for cb in col_blocks] + [tile_spec(cb, True) for cb in col_blocks]
                + gate_specs(0) + gate_specs(1) + [state_spec(s.shape[1:]) for s in states])
    out_specs = ([pl.BlockSpec((ROW_TILE, HW), lambda j: (j, 0)),
                  pl.BlockSpec((ROW_TILE, HW), lambda j: (back(j), 0))]
                 + [state_spec(s[1:]) for s in state_out_shapes])
    out_shape = ([jax.ShapeDtypeStruct((t_rows, HW), F32)] * 2
                 + [jax.ShapeDtypeStruct(tuple(s), F32) for s in state_out_shapes])
    return pl.pallas_call(
        functools.partial(kernel_fn, n_ctx_tiles=n_ctx_tiles, tiles_per_seq=tiles_per_seq),
        grid=(n_tiles,),
        in_specs=in_specs, out_specs=out_specs, out_shape=out_shape,
        scratch_shapes=scratch,
        compiler_params=_cparams(("arbitrary",)),
        name=name,
    )(*([src] * (2 * len(col_blocks))), gtok, grow, ghead, gtok, grow, ghead, *states)


def _delta_scan(dqkv, gtok, grow, ghead, s0, n_ctx_tiles, tiles_per_seq):
    return _scan_call(_delta_kernel, "delta_scan", dqkv, (0, 1, 2), gtok, grow, ghead, [s0], [s0.shape],
                      [pltpu.VMEM((HW, HW), F32)] * 2, n_ctx_tiles, tiles_per_seq)


def _mlstm_scan(bgc, gtok, grow, ghead, cn0, mc0, mr0, n_ctx_tiles, tiles_per_seq):
    n_seq = cn0.shape[0]
    return _scan_call(_mlstm_kernel, "mlstm_scan", bgc, (1, 2, 3), gtok, grow, ghead, [cn0, mc0, mr0],
                      [cn0.shape, (n_seq, N_DIR, HW, LANES)],
                      [pltpu.VMEM((HW, 2 * HW), F32)] * 2 + [pltpu.VMEM((HW, 1), F32)] * 2
                      + [pltpu.VMEM((1, HW), F32)] * 2, n_ctx_tiles, tiles_per_seq)


def _merge_ffn_kernel(x_ref, mod_ref, actx_ref, alat_ref, odf_ref, odb_ref, hmf_ref, hmb_ref, bg_ref, co_ref,
                      ag_ref, dg_ref, mg_ref, w_ref, g2_ref, wgu_ref, wdn_ref, fg_ref, o_ref,
                      *, lam_init, n_ctx_tiles, final):
    a = jnp.where(pl.program_id(0) < n_ctx_tiles, actx_ref[...], alat_ref[...])
    a = a * lax.rsqrt(_dot01(a * a, _seg_ones(HEADS * A_V_DIM, A_V_DIM)) * (1.0 / A_V_DIM) + EPS)
    a = a * ag_ref[...] * (1.0 - lam_init)
    ones = _seg_ones(HW, HEAD_DIM)
    od = odf_ref[...] + odb_ref[...]
    bg = bg_ref[...]
    d = od * lax.rsqrt(_dot01(od * od, ones) * (1.0 / HEAD_DIM) + EPS) * dg_ref[...] * (bg * _sigmoid(bg))
    hm = hmf_ref[...] + hmb_ref[...]
    m = _sigmoid(co_ref[...]) * (hm * lax.rsqrt(_dot01(hm * hm, ones) * (1.0 / HEAD_DIM) + EPS) * mg_ref[...])
    cat = jnp.concatenate([a.astype(BF16), d.astype(BF16), m.astype(BF16)], axis=1)
    mix = jnp.dot(cat, w_ref[...], preferred_element_type=F32)
    x = x_ref[...] + mod_ref[:, 2 * D_MODEL:3 * D_MODEL] * mix
    y = x * lax.rsqrt(jnp.mean(x * x, axis=-1, keepdims=True) + EPS) * g2_ref[...]
    u = y * (1.0 + mod_ref[:, 4 * D_MODEL:5 * D_MODEL]) + mod_ref[:, 3 * D_MODEL:4 * D_MODEL]
    h = jnp.dot(u.astype(BF16), wgu_ref[...], preferred_element_type=F32)
    gate = h[:, 0:FFN_HIDDEN]
    act = (gate * _sigmoid(gate)) * h[:, FFN_HIDDEN:2 * FFN_HIDDEN]
    out = x + mod_ref[:, 5 * D_MODEL:6 * D_MODEL] * jnp.dot(act.astype(BF16), wdn_ref[...],
                                                             preferred_element_type=F32)
    if final:
        out = out * lax.rsqrt(jnp.mean(out * out, axis=-1, keepdims=True) + EPS) * fg_ref[...]
    o_ref[...] = out


def _merge_ffn(x_all, mod_l, a_ctx, a_lat, od, hm, bgc, attn_g, delta_g, mlstm_g, w_out, lam_init,
               g2, w_gu, w_dn, final_g, final, n_ctx_tiles, tiles_per_seq):
    t_rows = x_all.shape[0]

    def mod_idx(i):
        return (jnp.where(i < n_ctx_tiles, 0, 1 + (i - n_ctx_tiles) // tiles_per_seq), 0, 0)

    full = lambda r, c: pl.BlockSpec((r, c), lambda i: (0, 0), pipeline_mode=pl.Buffered(1))
    row = lambda w, cb=0: pl.BlockSpec((ROW_TILE, w), lambda i: (i, cb))
    a_w = HEADS * A_V_DIM
    return pl.pallas_call(
        functools.partial(_merge_ffn_kernel, lam_init=lam_init, n_ctx_tiles=n_ctx_tiles, final=final),
        grid=(t_rows // ROW_TILE,),
        in_specs=[row(D_MODEL),
                  pl.BlockSpec((None, 1, 6 * D_MODEL), mod_idx),
                  pl.BlockSpec((ROW_TILE, a_w), lambda i: (jnp.minimum(i, n_ctx_tiles - 1), 0)),
                  pl.BlockSpec((ROW_TILE, a_w), lambda i: (jnp.maximum(i - n_ctx_tiles, 0), 0)),
                  row(HW), row(HW), row(HW), row(HW), row(HW), row(HW, 4),
                  full(1, a_w), full(1, HW), full(1, HW), full(D_MODEL, D_MODEL),
                  full(1, D_MODEL), full(D_MODEL, 2 * FFN_HIDDEN), full(FFN_HIDDEN, D_MODEL),
                  full(1, D_MODEL)],
        out_specs=row(D_MODEL),
        out_shape=jax.ShapeDtypeStruct((t_rows, D_MODEL), F32),
        compiler_params=_cparams(("parallel",)),
        name="merge_ffn",
    )(x_all, mod_l.reshape(8, 1, 6 * D_MODEL), a_ctx, a_lat, od[0], od[1], hm[0], hm[1], bgc, bgc,
      jnp.tile(attn_g, HEADS).reshape(1, -1), jnp.tile(delta_g, HEADS).reshape(1, -1),
      jnp.tile(mlstm_g, HEADS).reshape(1, -1), w_out, g2.reshape(1, D_MODEL), w_gu, w_dn,
      final_g.reshape(1, D_MODEL))


def _rope_tables(dec_seq):
    n_rows = dec_seq // GRID_W
    rows = jnp.repeat(jnp.arange(n_rows, dtype=F32), GRID_W)
    cols = jnp.tile(jnp.arange(GRID_W, dtype=F32), n_rows)
    n_freq = A_QK_DIM // 4
    inv_freq = ROPE_BASE ** (-jnp.arange(n_freq, dtype=F32) / n_freq)
    ang = jnp.concatenate([rows[:, None] * inv_freq, cols[:, None] * inv_freq], axis=-1)
    cos = jnp.repeat(jnp.cos(ang), 2, axis=-1)
    sin = jnp.repeat(jnp.sin(ang), 2, axis=-1) * jnp.tile(jnp.array([-1.0, 1.0], F32), A_QK_DIM // 2)
    cos = jnp.concatenate([jnp.ones((ROW_TILE, A_QK_DIM), F32), cos], axis=0)
    sin = jnp.concatenate([jnp.zeros((ROW_TILE, A_QK_DIM), F32), sin], axis=0)
    return jnp.tile(cos, (1, 2)), jnp.tile(sin, (1, 2))


def _permute_proj(w_in_l, b_in_l):
    def perm(a):
        head, ba_bb, tail, ci_cf = a[..., 0:2560], a[..., 2560:2576], a[..., 2576:3600], a[..., 3600:3616]
        pad = jnp.zeros(a.shape[:-1] + (N_PROJ - 3616,), a.dtype)
        return jnp.concatenate([head, tail, ba_bb, ci_cf, pad], axis=-1)
    return perm(w_in_l).astype(BF16), perm(b_in_l)


def _gate_layouts(grow80, n_chunks):
    r = grow80.reshape(10, N_DIR, HEADS, n_chunks, CHUNK)
    tok = r[jnp.array([0, 2, 1, 4, 6])].transpose(1, 3, 4, 0, 2).reshape(N_DIR, n_chunks * CHUNK, 5 * HEADS)
    gtok = jnp.pad(tok, ((0, 0), (0, 0), (0, LANES - 5 * HEADS)))
    rows = r[jnp.array([0, 5, 8, 9])].transpose(1, 3, 0, 2, 4).reshape(N_DIR, n_chunks, 4, HW)
    grow = jnp.pad(rows, ((0, 0), (0, 0), (0, 4), (0, 0)))
    per_head = lambda q: r[q, :, :, :, 0].transpose(0, 2, 1)[..., None]
    zeros = lambda w: jnp.zeros((N_DIR, n_chunks, HEADS, w), F32)
    delta_rows = jnp.concatenate([zeros(CHUNK), per_head(3), zeros(LANES - CHUNK - 1)], axis=-1)
    mlstm_rows = jnp.concatenate([r[7].transpose(0, 2, 1, 3), per_head(8), per_head(9),
                                  zeros(LANES - CHUNK - 2)], axis=-1)
    return gtok, grow, jnp.concatenate([delta_rows, mlstm_rows], axis=2)


def _block_diag(s):
    eye = jnp.eye(HEADS, dtype=s.dtype)
    out = s[..., :, :, None, :] * eye[:, None, :, None]
    return out.reshape(s.shape[:-3] + (HW, HW))


def _block_diag_inv(s_bd):
    s6 = s_bd.reshape(s_bd.shape[:-2] + (HEADS, HEAD_DIM, HEADS, HEAD_DIM))
    return jnp.stack([s6[..., h, :, h, :] for h in range(HEADS)], axis=-3)


def _norm_block(n):
    return _block_diag(jnp.broadcast_to(n[..., None], n.shape + (HEAD_DIM,)))


def _with_zero_ctx(batch, lat_state):
    return jnp.concatenate([jnp.zeros((batch,) + lat_state.shape[1:], lat_state.dtype), lat_state], axis=0)


def kernel(x_prompt, x_sample, cache_attn_k, cache_attn_v, state_delta, state_mlstm_C, state_mlstm_n,
           state_mlstm_m, c, c_ctx, norm1_g, norm2_g, w_mod, b_mod, w_in, b_in, w_out, lambda_qk,
           attn_subln_g, delta_conv_w, delta_A_log, delta_dt_bias, delta_norm_g, mlstm_f_bias,
           mlstm_norm_g, w_gate_up, w_down, final_norm_g):
    batch, seq, _ = x_prompt.shape
    dec_batch, dec_seq, _ = x_sample.shape
    past_len = cache_attn_k.shape[2]
    assert seq == ROW_TILE and dec_seq % ROW_TILE == 0 and dec_batch + 1 <= 8
    t_ctx = batch * seq
    t_lat = dec_batch * dec_seq
    t_rows = t_ctx + t_lat
    n_ctx_tiles = t_ctx // ROW_TILE
    tiles_per_seq = dec_seq // ROW_TILE
    n_chunks = t_rows // CHUNK

    x_all = jnp.concatenate([x_prompt.reshape(t_ctx, D_MODEL), x_sample.reshape(t_lat, D_MODEL)], axis=0)
    cvecs = jnp.zeros((8, D_MODEL), F32).at[0].set(c_ctx).at[1:1 + dec_batch].set(c)
    mod = _ada_mod(cvecs, w_mod, b_mod)
    cos_t, sin_t = _rope_tables(dec_seq)

    ks_l, vs_l, sd_l, cm_l, nm_l, mm_l = [], [], [], [], [], []
    for l in range(DEPTH):
        lam_init = 0.8 - 0.6 * math.exp(-0.3 * l)
        w_p, b_p = _permute_proj(w_in[l], b_in[l])
        qk, va, kv32, bqkv, bgc, gates = _in_proj(x_all, mod[l], norm1_g[l], w_p, b_p, cos_t, sin_t,
                                                 n_ctx_tiles, tiles_per_seq)
        ks_l.append(kv32[:t_ctx, 0:512].reshape(batch, seq, HEADS, 2, A_QK_DIM))
        vs_l.append(kv32[:t_ctx, 512:1024].reshape(batch, seq, HEADS, A_V_DIM))

        par = jnp.zeros((32, LANES), F32)
        par = par.at[0:8, 0].set(delta_A_log[l].reshape(-1)).at[0:8, 1].set(delta_dt_bias[l].reshape(-1))
        par = par.at[24:32, 1].set(mlstm_f_bias[l].reshape(-1))
        gtok, grow, ghead = _gate_layouts(_gate_prep(gates[:, 0:32].T, par), n_chunks)

        k_ctx = qk[:t_ctx, 512:1024].reshape(batch, seq, 512)
        v_ctx = va[:t_ctx].reshape(batch, seq, 512)
        a_ctx = _diff_attention(lambda_qk[l], qk, k_ctx, v_ctx, lam_init, 0, batch, seq, seq, seq)
        k_lat = jnp.concatenate([qk[t_ctx:, 512:1024].reshape(dec_batch, dec_seq, 512),
                                 cache_attn_k[:, l].reshape(dec_batch, past_len, 512).astype(BF16)], axis=1)
        v_lat = jnp.concatenate([va[t_ctx:].reshape(dec_batch, dec_seq, 512),
                                 cache_attn_v[:, l].reshape(dec_batch, past_len, 512).astype(BF16)], axis=1)
        a_lat = _diff_attention(lambda_qk[l], qk, k_lat, v_lat, lam_init, t_ctx, dec_batch, dec_seq, 512, 512)

        dqkv = _delta_conv(bqkv, delta_conv_w[l], n_ctx_tiles, tiles_per_seq)
        od_f, od_b, s_fin = _delta_scan(dqkv, gtok, grow, ghead,
                                        _with_zero_ctx(batch, _block_diag(state_delta[:, l])),
                                        n_ctx_tiles, tiles_per_seq)
        sd_l.append(_block_diag_inv(s_fin[:batch]))

        cn_lat = jnp.concatenate([_block_diag(state_mlstm_C[:, l]), _norm_block(state_mlstm_n[:, l])], axis=-1)
        m_lat = jnp.repeat(state_mlstm_m[:, l], HEAD_DIM, axis=-1)
        hm_f, hm_b, cn_fin, m_fin = _mlstm_scan(
            bgc, gtok, grow, ghead, _with_zero_ctx(batch, cn_lat), _with_zero_ctx(batch, m_lat[..., None]),
            _with_zero_ctx(batch, m_lat[:, :, None, :]), n_ctx_tiles, tiles_per_seq)
        cm_l.append(_block_diag_inv(cn_fin[:batch, :, :, 0:HW]))
        nm_l.append(_block_diag_inv(cn_fin[:batch, :, :, HW:2 * HW])[..., 0])
        mm_l.append(m_fin[:batch, :, ::HEAD_DIM, 0])

        x_all = _merge_ffn(x_all, mod[l], a_ctx, a_lat, (od_f, od_b), (hm_f, hm_b), bgc, attn_subln_g[l],
                           delta_norm_g[l], mlstm_norm_g[l], w_out[l].astype(BF16), lam_init, norm2_g[l],
                           w_gate_up[l].astype(BF16), w_down[l].astype(BF16), final_norm_g, l == DEPTH - 1,
                           n_ctx_tiles, tiles_per_seq)

    y_prompt = x_all[:t_ctx].reshape(batch, seq, D_MODEL)
    y_sample = x_all[t_ctx:].reshape(dec_batch, dec_seq, D_MODEL)
    return (y_prompt, y_sample, jnp.stack(ks_l, axis=1), jnp.stack(vs_l, axis=1), jnp.stack(sd_l, axis=1),
            jnp.stack(cm_l, axis=1), jnp.stack(nm_l, axis=1), jnp.stack(mm_l, axis=1))
```

```python
import functools
import math

import jax
import jax.numpy as jnp
from jax import lax
from jax.experimental import pallas as pl
from jax.experimental.pallas import tpu as pltpu

F32 = jnp.float32
BF16 = jnp.bfloat16

D_MODEL = 1024
DEPTH = 2
GRID_W = 64
N_DIR = 2
CHUNK = 64
ROPE_BASE = 10000.0
EPS = 1e-6
HEADS = 4
A_QK_DIM = 64
A_V_DIM = 128
HEAD_DIM = 64
CONV_K = 5
FFN_HIDDEN = 2816
ROW_TILE = 256
CHUNKS_PER_TILE = ROW_TILE // CHUNK
HW = HEADS * HEAD_DIM
LANES = 128
HALO = 8
BF16_ROWS = 16
GATE_GROUP = 32

N_PROJ = 3712
VMEM_LIMIT = 56 * 1024 * 1024


DENSE_TILE = 512


def _dense_tiles(n_ctx_tiles, tiles_per_seq):
    ratio = DENSE_TILE // ROW_TILE
    assert n_ctx_tiles % ratio == 0 and tiles_per_seq % ratio == 0
    return n_ctx_tiles // ratio, tiles_per_seq // ratio


def _cparams(sem):
    return pltpu.CompilerParams(dimension_semantics=sem, vmem_limit_bytes=VMEM_LIMIT)


def _bdot(a, b):
    return jnp.dot(a.astype(BF16), b.astype(BF16), preferred_element_type=F32)


def _bdot_nt(a, b):
    return lax.dot_general(a.astype(BF16), b.astype(BF16), (((1,), (1,)), ((), ())),
                           preferred_element_type=F32)


def _split3(x):
    hi = x.astype(BF16).astype(F32)
    r1 = x - hi
    mid = r1.astype(BF16).astype(F32)
    lo = (r1 - mid).astype(BF16).astype(F32)
    return hi, mid, lo


def _dot01(x, m01):
    d = functools.partial(jnp.dot, preferred_element_type=F32)
    hi, mid, lo = _split3(x)
    return d(hi.astype(BF16), m01) + d(mid.astype(BF16), m01) + d(lo.astype(BF16), m01)


def _group_sumsq(x, seg):
    sq = x * x
    out = []
    for t in range(x.shape[1] // LANES):
        tile = sq[:, t * LANES:(t + 1) * LANES]
        if seg == LANES:
            out.append(jnp.broadcast_to(jnp.sum(tile, axis=-1, keepdims=True), tile.shape))
        else:
            low = lax.broadcasted_iota(jnp.int32, tile.shape, 1) < seg
            s_lo = jnp.sum(jnp.where(low, tile, 0.0), axis=-1, keepdims=True)
            s_hi = jnp.sum(jnp.where(low, 0.0, tile), axis=-1, keepdims=True)
            out.append(jnp.where(low, s_lo, s_hi))
    return jnp.concatenate(out, axis=1)


def _sigmoid(x):
    return 1.0 / (1.0 + jnp.exp(-x))


def _softplus(x):
    return jnp.maximum(x, 0.0) + jnp.log1p(jnp.exp(-jnp.abs(x)))


def _ada_kernel(c_ref, w_ref, b_ref, o_ref):
    c = c_ref[...]
    s = c * _sigmoid(c)
    o_ref[...] = _bdot(s, w_ref[...]) + b_ref[...]


def _ada_mod(cvecs, w_mod, b_mod):
    n_out = w_mod.shape[-1]
    tn = 1024
    return pl.pallas_call(
        _ada_kernel,
        grid=(DEPTH, n_out // tn),
        in_specs=[pl.BlockSpec((8, D_MODEL), lambda l, j: (0, 0)),
                  pl.BlockSpec((None, D_MODEL, tn), lambda l, j: (l, 0, j)),
                  pl.BlockSpec((None, 1, tn), lambda l, j: (l, 0, j))],
        out_specs=pl.BlockSpec((None, 8, tn), lambda l, j: (l, 0, j)),
        out_shape=jax.ShapeDtypeStruct((DEPTH, 8, n_out), F32),
        compiler_params=_cparams(("parallel", "parallel")),
        name="ada_mod",
    )(cvecs, w_mod, b_mod.reshape(DEPTH, 1, n_out))


def _proj_kernel(xc_ref, xl_ref, mod_ref, g_ref, w_ref, b_ref, cos_ref, sin_ref,
                 qk_ref, va_ref, kv32_ref, bqkv_ref, bgc_ref, gates_ref, *, n_ctx_tiles):
    x = jnp.where(pl.program_id(0) < n_ctx_tiles, xc_ref[...], xl_ref[...])
    y = x * lax.rsqrt(jnp.mean(x * x, axis=-1, keepdims=True) + EPS) * g_ref[...]
    u = y * (1.0 + mod_ref[:, D_MODEL:2 * D_MODEL]) + mod_ref[:, 0:D_MODEL]
    acc = _bdot(u, w_ref[...]) + b_ref[...]
    kv32_ref[...] = acc[:, 512:1536]
    va_ref[...] = acc[:, 1024:1536].astype(BF16)
    bqkv_ref[...] = acc[:, 1536:2304]
    bgc_ref[...] = acc[:, 2304:3584]
    gates_ref[...] = acc[:, 3584:3712]
    cos = cos_ref[...]
    sin = sin_ref[...]
    even = (lax.broadcasted_iota(jnp.int32, cos.shape, 1) % 2) == 0
    for j in range(8):
        xj = acc[:, j * LANES:(j + 1) * LANES]
        swapped = jnp.where(even, pltpu.roll(xj, LANES - 1, 1), pltpu.roll(xj, 1, 1))
        r = xj * cos + swapped * sin
        if j < 4:
            r = r * (A_QK_DIM ** -0.5)
        qk_ref[:, j * LANES:(j + 1) * LANES] = r.astype(BF16)


def _x_specs(n_ctx_tiles, lat_tile0):
    return [pl.BlockSpec((DENSE_TILE, D_MODEL), lambda i: (jnp.minimum(i, n_ctx_tiles - 1), 0)),
            pl.BlockSpec((DENSE_TILE, D_MODEL), lambda i: (jnp.maximum(i - n_ctx_tiles, 0) + lat_tile0, 0))]


def _in_proj(x_ctx, x_lat, lat_tile0, t_rows, mod_l, g1, w_p, b_p, cos_t, sin_t, n_ctx_tiles, tiles_per_seq):
    n_tiles = t_rows // DENSE_TILE
    n_ctx_tiles, tiles_per_seq = _dense_tiles(n_ctx_tiles, tiles_per_seq)

    def mod_idx(i):
        return (jnp.where(i < n_ctx_tiles, 0, 1 + (i - n_ctx_tiles) // tiles_per_seq), 0, 0)

    def rope_idx(i):
        return (jnp.where(i < n_ctx_tiles, 0, 1 + (i - n_ctx_tiles) % tiles_per_seq), 0)

    row = lambda w: pl.BlockSpec((DENSE_TILE, w), lambda i: (i, 0))
    outs = [(1024, BF16), (512, BF16), (1024, F32), (768, F32), (1280, F32), (LANES, F32)]
    return pl.pallas_call(
        functools.partial(_proj_kernel, n_ctx_tiles=n_ctx_tiles),
        grid=(n_tiles,),
        in_specs=_x_specs(n_ctx_tiles, lat_tile0) + [
                  pl.BlockSpec((None, 1, 6 * D_MODEL), mod_idx),
                  pl.BlockSpec((1, D_MODEL), lambda i: (0, 0)),
                  pl.BlockSpec((D_MODEL, N_PROJ), lambda i: (0, 0)),
                  pl.BlockSpec((1, N_PROJ), lambda i: (0, 0)),
                  pl.BlockSpec((DENSE_TILE, LANES), rope_idx),
                  pl.BlockSpec((DENSE_TILE, LANES), rope_idx)],
        out_specs=[row(w) for w, _ in outs],
        out_shape=[jax.ShapeDtypeStruct((t_rows, w), dt) for w, dt in outs],
        compiler_params=_cparams(("parallel",)),
        name="in_proj",
    )(x_ctx, x_lat, mod_l.reshape(8, 1, 6 * D_MODEL), g1.reshape(1, D_MODEL), w_p, b_p.reshape(1, N_PROJ),
      cos_t, sin_t)


GATE_ROWS = 80


def _gates_kernel(g_ref, par_ref, o_ref):
    x = g_ref[...]
    alog = par_ref[0:8, 0:1]
    dtb = par_ref[0:8, 1:2]
    fb = par_ref[24:32, 1:2]
    g = -jnp.exp(alog) * _softplus(x[0:8] + dtb)
    beta = _sigmoid(x[8:16])
    ig = x[16:24]
    lf = -_softplus(-(x[24:32] + fb))
    tl = x.shape[1]
    r = lax.broadcasted_iota(jnp.int32, (LANES, LANES), 0)
    c = lax.broadcasted_iota(jnp.int32, (LANES, LANES), 1)
    same = (r // CHUNK) == (c // CHUNK)
    pre = jnp.where(same & (r <= c), 1.0, 0.0).astype(BF16)
    suf = jnp.where(same & (r >= c), 1.0, 0.0).astype(BF16)
    tot = jnp.where(same, 1.0, 0.0).astype(BF16)
    m01 = jnp.concatenate([pre, suf, tot], axis=1)
    fwd16 = (lax.broadcasted_iota(jnp.int32, (16, LANES), 0) % 8) < 4
    fwd8 = lax.broadcasted_iota(jnp.int32, (8, LANES), 0) < 4
    pos = lax.broadcasted_iota(jnp.int32, (8, LANES), 1) % CHUNK

    def running_max(v):
        pf = sf = v
        s = 1
        while s < CHUNK:
            pf = jnp.maximum(pf, jnp.where(pos >= s, pltpu.roll(pf, s, 1), -jnp.inf))
            sf = jnp.maximum(sf, jnp.where(pos < CHUNK - s, pltpu.roll(sf, LANES - s, 1), -jnp.inf))
            s *= 2
        return pf, sf

    o_ref[16:24, :] = beta
    for j in range(tl // LANES):
        sl = slice(j * LANES, (j + 1) * LANES)
        xs = jnp.concatenate([g[:, sl], lf[:, sl]], axis=0)
        cs = _dot01(xs, m01)
        cum = jnp.where(fwd16, cs[:, 0:LANES], cs[:, LANES:2 * LANES])
        total = cs[:, 2 * LANES:3 * LANES]
        gc, gtot, b, btot = cum[0:8], total[0:8], cum[8:16], total[8:16]
        rr = ig[:, sl] - b
        kvl = btot - b + ig[:, sl]
        r_pf, r_sf = running_max(rr)
        k_pf, k_sf = running_max(kvl)
        o_ref[0:8, sl] = gc
        o_ref[8:16, sl] = gtot - gc
        o_ref[24:32, sl] = gtot
        o_ref[32:40, sl] = b
        o_ref[40:48, sl] = rr
        o_ref[48:56, sl] = jnp.where(fwd8, r_pf, r_sf)
        o_ref[56:64, sl] = kvl
        o_ref[64:72, sl] = btot
        o_ref[72:80, sl] = jnp.maximum(k_pf, k_sf)


def _gate_prep(gates_t, par):
    t_rows = gates_t.shape[1]
    tl = math.gcd(t_rows, 2048)
    return pl.pallas_call(
        _gates_kernel,
        grid=(t_rows // tl,),
        in_specs=[pl.BlockSpec((32, tl), lambda i: (0, i)),
                  pl.BlockSpec((32, LANES), lambda i: (0, 0))],
        out_specs=pl.BlockSpec((GATE_ROWS, tl), lambda i: (0, i)),
        out_shape=jax.ShapeDtypeStruct((GATE_ROWS, t_rows), F32),
        compiler_params=_cparams(("parallel",)),
        name="gate_prep",
    )(gates_t, par)


def _conv_kernel(x_ref, p_ref, n_ref, w_ref, o_ref, ext_s, *, n_ctx_tiles, tiles_per_seq):
    i = pl.program_id(0)
    j = (i - n_ctx_tiles) % tiles_per_seq
    is_ctx = i < n_ctx_tiles
    first = jnp.logical_or(is_ctx, j == 0)
    last = jnp.logical_or(is_ctx, j == tiles_per_seq - 1)
    ext_s[0:HALO, :] = jnp.where(first, 0.0, p_ref[...])
    ext_s[HALO:HALO + ROW_TILE, :] = x_ref[...]
    ext_s[HALO + ROW_TILE:, :] = jnp.where(last, 0.0, n_ref[...])
    w = w_ref[...]
    y = None
    for k in range(CONV_K):
        off = HALO - CONV_K // 2 + k
        term = ext_s[off:off + ROW_TILE, :] * w[k:k + 1, :]
        y = term if y is None else y + term
    y = y * _sigmoid(y)
    q = y[:, 0:HW]
    k_ = y[:, HW:2 * HW]
    o_ref[:, 0:HW] = q * lax.rsqrt(_group_sumsq(q, HEAD_DIM) + EPS) * (HEAD_DIM ** -0.5)
    o_ref[:, HW:2 * HW] = k_ * lax.rsqrt(_group_sumsq(k_, HEAD_DIM) + EPS)
    o_ref[:, 2 * HW:3 * HW] = y[:, 2 * HW:3 * HW]


def _delta_conv(bqkv, conv_w, n_ctx_tiles, tiles_per_seq):
    t_rows, width = bqkv.shape
    n_tiles = t_rows // ROW_TILE
    hb = ROW_TILE // HALO
    n_hb = t_rows // HALO
    return pl.pallas_call(
        functools.partial(_conv_kernel, n_ctx_tiles=n_ctx_tiles, tiles_per_seq=tiles_per_seq),
        grid=(n_tiles,),
        in_specs=[pl.BlockSpec((ROW_TILE, width), lambda i: (i, 0)),
                  pl.BlockSpec((HALO, width), lambda i: (jnp.maximum(i * hb - 1, 0), 0)),
                  pl.BlockSpec((HALO, width), lambda i: (jnp.minimum((i + 1) * hb, n_hb - 1), 0)),
                  pl.BlockSpec((CONV_K, width), lambda i: (0, 0))],
        out_specs=pl.BlockSpec((ROW_TILE, width), lambda i: (i, 0)),
        out_shape=jax.ShapeDtypeStruct((t_rows, width), F32),
        scratch_shapes=[pltpu.VMEM((ROW_TILE + 2 * HALO, width), F32)],
        compiler_params=_cparams(("parallel",)),
        name="delta_conv",
    )(bqkv, bqkv, bqkv, conv_w)


def _attn_kernel(lqk_ref, q_ref, k_ref, vt_ref, o_ref, *, lam_init, tq, tk):
    q = q_ref[...]
    lane = lax.broadcasted_iota(jnp.int32, q.shape, 1)
    zero = jnp.zeros_like(q)
    q2 = jnp.concatenate([jnp.where(lane < A_QK_DIM, q, zero), jnp.where(lane >= A_QK_DIM, q, zero)], axis=0)

    def scores(j):
        return lax.dot_general(k_ref[j * tk:(j + 1) * tk, :], q2, (((1,), (1,)), ((), ())),
                               preferred_element_type=F32)

    m = acc = None
    n_kv = k_ref.shape[0] // tk
    st_next = scores(0)
    for j in range(n_kv):
        st = st_next
        if j + 1 < n_kv:
            st_next = scores(j + 1)
        m_new = jnp.max(st, axis=0, keepdims=True)
        if j > 0:
            m_new = jnp.maximum(m, m_new)
        p = jnp.exp(st - m_new)
        pv = jnp.dot(vt_ref[:, j * tk:(j + 1) * tk], p.astype(BF16), preferred_element_type=F32)
        acc = pv if j == 0 else jnp.exp(m - m_new) * acc + pv
        m = m_new
    o = acc[0:A_V_DIM] / acc[A_V_DIM:A_V_DIM + 1]
    lq = lqk_ref[...]
    lam = (jnp.exp(jnp.sum(lq[0:1] * lq[1:2], axis=-1, keepdims=True))
           - jnp.exp(jnp.sum(lq[2:3] * lq[3:4], axis=-1, keepdims=True)) + lam_init)
    o_ref[...] = (o[:, 0:tq] - lam * o[:, tq:2 * tq]).T


def _diff_attention(lqk, qk, k_all, v_all, lam_init, q_row0, n_seq, lq, tq, tk):
    lk = k_all.shape[1]
    qb0 = q_row0 // tq
    nq = lq // tq
    vt_all = jnp.concatenate([v_all.reshape(n_seq, lk, HEADS, A_V_DIM).transpose(0, 2, 3, 1),
                              jnp.ones((n_seq, HEADS, BF16_ROWS, lk), BF16)], axis=2)
    return pl.pallas_call(
        functools.partial(_attn_kernel, lam_init=lam_init, tq=tq, tk=tk),
        grid=(n_seq, HEADS, nq),
        in_specs=[pl.BlockSpec((4, A_QK_DIM), lambda b, h, i: (0, 0)),
                  pl.BlockSpec((tq, LANES), lambda b, h, i: (qb0 + b * nq + i, h)),
                  pl.BlockSpec((None, lk, LANES), lambda b, h, i: (b, 0, h)),
                  pl.BlockSpec((None, None, A_V_DIM + BF16_ROWS, lk), lambda b, h, i: (b, h, 0, 0))],
        out_specs=pl.BlockSpec((tq, LANES), lambda b, h, i: (b * nq + i, h)),
        out_shape=jax.ShapeDtypeStruct((n_seq * lq, HEADS * A_V_DIM), F32),
        compiler_params=_cparams(("parallel", "parallel", "parallel")),
        name="diff_attention",
    )(lqk, qk, k_all, vt_all)


def _same_head(n_cols=HW):
    r = lax.broadcasted_iota(jnp.int32, (HW, n_cols), 0) // HEAD_DIM
    c = (lax.broadcasted_iota(jnp.int32, (HW, n_cols), 1) % HW) // HEAD_DIM
    return r == c


def _bd(x, same):
    return jnp.where(same, jnp.concatenate([x, x, x, x], axis=0), 0.0)


def _cat_index():
    i = lax.broadcasted_iota(jnp.int32, (CHUNK, HW), 0)
    j = lax.broadcasted_iota(jnp.int32, (CHUNK, HW), 1) % CHUNK
    return i, j


def _scan_masks(direction):
    i, j = _cat_index()
    diff = (i - j) * (1 - 2 * direction)
    return diff >= 0, diff > 0


def _expand_matrix(first_lane, n_q):
    r = lax.broadcasted_iota(jnp.int32, (LANES, n_q * HW), 0)
    c = lax.broadcasted_iota(jnp.int32, (LANES, n_q * HW), 1)
    s = r % GATE_GROUP - first_lane
    hit = (r < 3 * GATE_GROUP) & (s >= 0) & (s < 4 * n_q) & (s // HEADS == c // HW) \
        & (s % HEADS == (c % HW) // HEAD_DIM)
    return jnp.where(hit, 1.0, 0.0).astype(BF16)


def _expand(tok, e3):
    hi, mid, lo = _split3(tok)
    x3 = hi + pltpu.roll(mid, GATE_GROUP, 1) + pltpu.roll(lo, 2 * GATE_GROUP, 1)
    return jnp.dot(x3.astype(BF16), e3, preferred_element_type=F32)


def _head_rows(hd, lane):
    return jnp.concatenate([jnp.broadcast_to(hd[h:h + 1, lane:lane + 1], (HEAD_DIM, 1)) for h in range(HEADS)],
                           axis=0)


def _pad_rows(x):
    return jnp.concatenate([x, jnp.zeros_like(x)], axis=0)


def _unit_tri_inverse(mats, same):
    i, j = _cat_index()
    b16 = (i // 16) == (j // 16)
    b32 = (i // 32) == (j // 32)
    eye = jnp.where(i == j, 1.0, 0.0)
    xs = [jnp.where(b16, a, 0.0) for a in mats]
    ts = [eye - x for x in xs]
    xbd = [_bd(x, same) for x in xs]
    for _ in range(3):
        xs = [_bdot(x, b) for x, b in zip(xs, xbd)]
        xbd = [_bd(x, same) for x in xs]
        ts = [t + _bdot(t, b) for t, b in zip(ts, xbd)]
    for inside in (jnp.logical_and(b32, jnp.logical_not(b16)), jnp.logical_not(b32)):
        mids = [_bdot(t, _bd(jnp.where(inside, a, 0.0), same)) for t, a in zip(ts, mats)]
        ts = [t - _bdot(m, _bd(t, same)) for t, m in zip(ts, mids)]
    return ts


def _scan_job(n_ctx_tiles, tiles_per_seq):
    j = pl.program_id(0)
    t = (j - n_ctx_tiles) % tiles_per_seq
    is_ctx = j < n_ctx_tiles
    return jnp.logical_or(is_ctx, t == 0), jnp.logical_or(is_ctx, t == tiles_per_seq - 1)


def _chunks_of(refs_f, refs_b):
    out = []
    for d, (q, k, v, tok, rows, heads) in enumerate((refs_f, refs_b)):
        for c in range(CHUNKS_PER_TILE):
            sl = slice(c * CHUNK, (c + 1) * CHUNK)
            out.append((q[sl, :], k[sl, :], v[sl, :], tok[sl, :], rows[c], heads[c], d))
    return out


def _delta_prep(chunks):
    same = _same_head()
    masks = [_scan_masks(0), _scan_masks(1)]
    e3 = _expand_matrix(0, 3)
    n = len(chunks)
    ex = [_expand(ch[3], e3) for ch in chunks]
    gc = [e[:, 0:HW] for e in ex]
    kb = [chunks[i][1] * ex[i][:, HW:2 * HW] for i in range(n)]
    kk = [_bdot_nt(jnp.concatenate([kb[i], chunks[i][0]], axis=0), _bd(chunks[i][1], same)) for i in range(n)]
    e_incl = [jnp.exp(jnp.where(masks[chunks[i][6]][0], gc[i] - chunks[i][4][0:1, :], -jnp.inf)) for i in range(n)]
    a = [jnp.where(masks[chunks[i][6]][1], kk[i][0:CHUNK] * e_incl[i], 0.0) for i in range(n)]
    aqk = [kk[i][CHUNK:2 * CHUNK] * e_incl[i] for i in range(n)]
    tinv = _unit_tri_inverse(a, same)
    egc = [jnp.exp(g) for g in gc]
    rhs = [jnp.concatenate([_bd(chunks[i][2] * ex[i][:, HW:2 * HW], same), _bd(kb[i] * egc[i], same)], axis=1)
           for i in range(n)]
    uw = [_bdot(tinv[i], rhs[i]) for i in range(n)]
    kd_t = [_pad_rows(chunks[i][1] * jnp.exp(ex[i][:, 2 * HW:3 * HW])).T for i in range(n)]
    gl = [jnp.exp(_head_rows(chunks[i][5][0:HEADS], CHUNK)) for i in range(n)]
    return [(uw[i][:, 0:HW], uw[i][:, HW:2 * HW], chunks[i][0] * egc[i], kd_t[i], aqk[i], gl[i])
            for i in range(n)]


def _delta_step(pre, s_ref, o_ref, c, same):
    u, w, qd, kd_t, aqk, gl = pre
    s = s_ref[...]
    wq = _bdot(jnp.concatenate([w, qd], axis=0), s)
    v_new = u - wq[0:CHUNK]
    o_ref[c * CHUNK:(c + 1) * CHUNK, :] = wq[CHUNK:2 * CHUNK] + _bdot(aqk, _bd(v_new, same))
    s_ref[...] = s * gl + jnp.where(same, _bdot(kd_t, _pad_rows(v_new)), 0.0)


def _delta_kernel(qf, kf, vf, qb, kb, vb, tokf, rowf, headf, tokb, rowb, headb, s0_ref,
                  of_ref, ob_ref, sout_ref, sf_s, sb_s, *, n_ctx_tiles, tiles_per_seq):
    first, last = _scan_job(n_ctx_tiles, tiles_per_seq)

    @pl.when(first)
    def _():
        sf_s[...] = s0_ref[0]
        sb_s[...] = s0_ref[1]

    pre = _delta_prep(_chunks_of((qf, kf, vf, tokf, rowf, headf), (qb, kb, vb, tokb, rowb, headb)))
    same = _same_head()
    for i in range(CHUNKS_PER_TILE):
        cb = CHUNKS_PER_TILE - 1 - i
        _delta_step(pre[i], sf_s, of_ref, i, same)
        _delta_step(pre[CHUNKS_PER_TILE + cb], sb_s, ob_ref, cb, same)

    @pl.when(last)
    def _():
        sout_ref[0] = sf_s[...]
        sout_ref[1] = sb_s[...]


def _mlstm_prep(chunks):
    same = _same_head()
    masks = [_scan_masks(0), _scan_masks(1)]
    e3 = _expand_matrix(12, 2)
    n = len(chunks)
    ex = [_expand(ch[3], e3) for ch in chunks]
    b = [e[:, 0:HW] for e in ex]
    ks = [ch[1] * (HEAD_DIM ** -0.5) for ch in chunks]
    qk = [_bdot_nt(chunks[i][0], _bd(ks[i], same)) for i in range(n)]
    dmat = [jnp.where(masks[chunks[i][6]][0], b[i] + chunks[i][4][1:2, :], -jnp.inf) for i in range(n)]
    dmax = [ex[i][:, 0:HW] + ex[i][:, HW:2 * HW] for i in range(n)]
    k_t = [_pad_rows(k).T for k in ks]
    ones = jnp.ones((CHUNK, HW), F32)
    vaug = [_pad_rows(jnp.concatenate([ch[2], ones], axis=1)) for ch in chunks]
    vbd = [jnp.concatenate([_bd(ch[2], same), jnp.where(same, 1.0, 0.0)], axis=1) for ch in chunks]
    kvl = [jnp.concatenate([jnp.broadcast_to(ch[5][HEADS + h:HEADS + h + 1, :], (HEAD_DIM, LANES))
                            for h in range(HEADS)], axis=0) for ch in chunks]
    btot_col = [_head_rows(ch[5][HEADS:2 * HEADS], CHUNK) for ch in chunks]
    mkv_col = [_head_rows(ch[5][HEADS:2 * HEADS], CHUNK + 1) for ch in chunks]
    return [(chunks[i][0], k_t[i], vaug[i], vbd[i], b[i], dmat[i], dmax[i], qk[i], kvl[i], btot_col[i],
             mkv_col[i], chunks[i][4][2:3, :], chunks[i][4][3:4, :]) for i in range(n)]


def _mlstm_step(pre, cn_ref, mcol_ref, mrow_ref, o_ref, c, same2):
    q, k_t, vaug, vbd, b, dmat, dmax, qk, kvl, btot_col, mkv_col, btot_row, mkv_row = pre
    cn = cn_ref[...]
    ms_col = mcol_ref[...]
    ms_row = mrow_ref[...]
    inter = b + ms_row
    m_t = jnp.maximum(inter, dmax)
    s = qk * jnp.exp(dmat - m_t)
    w_inter = jnp.exp(inter - m_t)
    nd = jnp.concatenate([w_inter, w_inter], axis=1) * _bdot(q, cn) + _bdot(s, vbd)
    o_ref[c * CHUNK:(c + 1) * CHUNK, :] = nd[:, 0:HW] / jnp.maximum(jnp.abs(nd[:, HW:2 * HW]), jnp.exp(-m_t))
    m_new = jnp.maximum(btot_col + ms_col, mkv_col)
    lane = lax.broadcasted_iota(jnp.int32, kvl.shape, 1)
    wk = jnp.where(lane < CHUNK, jnp.exp(kvl - m_new), 0.0)
    cn_ref[...] = jnp.exp(btot_col + ms_col - m_new) * cn + jnp.where(same2, _bdot(k_t * wk, vaug), 0.0)
    mcol_ref[...] = m_new
    mrow_ref[...] = jnp.maximum(btot_row + ms_row, mkv_row)


def _mlstm_kernel(qf, kf, vf, qb, kb, vb, tokf, rowf, headf, tokb, rowb, headb, cn0_ref, mc0_ref, mr0_ref,
                  of_ref, ob_ref, cnout_ref, mout_ref, cnf_s, cnb_s, mcf_s, mcb_s, mrf_s, mrb_s,
                  *, n_ctx_tiles, tiles_per_seq):
    first, last = _scan_job(n_ctx_tiles, tiles_per_seq)

    @pl.when(first)
    def _():
        cnf_s[...] = cn0_ref[0]
        cnb_s[...] = cn0_ref[1]
        mcf_s[...] = mc0_ref[0]
        mcb_s[...] = mc0_ref[1]
        mrf_s[...] = mr0_ref[0]
        mrb_s[...] = mr0_ref[1]

    pre = _mlstm_prep(_chunks_of((qf, kf, vf, tokf, rowf, headf), (qb, kb, vb, tokb, rowb, headb)))
    same2 = _same_head(2 * HW)
    for i in range(CHUNKS_PER_TILE):
        cb = CHUNKS_PER_TILE - 1 - i
        _mlstm_step(pre[i], cnf_s, mcf_s, mrf_s, of_ref, i, same2)
        _mlstm_step(pre[CHUNKS_PER_TILE + cb], cnb_s, mcb_s, mrb_s, ob_ref, cb, same2)

    @pl.when(last)
    def _():
        cnout_ref[0] = cnf_s[...]
        cnout_ref[1] = cnb_s[...]
        mout_ref[0] = jnp.broadcast_to(mcf_s[...], (HW, LANES))
        mout_ref[1] = jnp.broadcast_to(mcb_s[...], (HW, LANES))


def _scan_call(kernel_fn, name, src, col_blocks, gtok, grow, ghead, states, state_out_shapes, scratch,
               n_ctx_tiles, tiles_per_seq):
    t_rows = src.shape[0]
    n_tiles = t_rows // ROW_TILE

    def seq_of(j):
        return jnp.where(j < n_ctx_tiles, j, n_ctx_tiles + (j - n_ctx_tiles) // tiles_per_seq)

    def back(j):
        jj = j - n_ctx_tiles
        mirrored = n_ctx_tiles + (jj // tiles_per_seq) * tiles_per_seq + tiles_per_seq - 1 - jj % tiles_per_seq
        return jnp.where(j < n_ctx_tiles, j, mirrored)

    def tile_spec(cb, bwd):
        return pl.BlockSpec((ROW_TILE, HW), (lambda j: (back(j), cb)) if bwd else (lambda j: (j, cb)))

    def gate_specs(d):
        tile = (lambda j: back(j)) if d else (lambda j: j)
        return [pl.BlockSpec((None, ROW_TILE, LANES), lambda j: (d, tile(j), 0)),
                pl.BlockSpec((None, CHUNKS_PER_TILE, 8, HW), lambda j: (d, tile(j), 0, 0)),
                pl.BlockSpec((None, CHUNKS_PER_TILE, 8, LANES), lambda j: (d, tile(j), 0, 0))]

    def state_spec(shape):
        return pl.BlockSpec((None,) + tuple(shape), lambda j: (seq_of(j),) + (0,) * len(shape))

    in_specs = ([tile_spec(cb, False) for cb in col_blocks] + [tile_spec(cb, True) for cb in col_blocks]
                + gate_specs(0) + gate_specs(1) + [state_spec(s.shape[1:]) for s in states])
    out_specs = ([pl.BlockSpec((ROW_TILE, HW), lambda j: (j, 0)),
                  pl.BlockSpec((ROW_TILE, HW), lambda j: (back(j), 0))]
                 + [state_spec(s[1:]) for s in state_out_shapes])
    out_shape = ([jax.ShapeDtypeStruct((t_rows, HW), F32)] * 2
                 + [jax.ShapeDtypeStruct(tuple(s), F32) for s in state_out_shapes])
    return pl.pallas_call(
        functools.partial(kernel_fn, n_ctx_tiles=n_ctx_tiles, tiles_per_seq=tiles_per_seq),
        grid=(n_tiles,),
        in_specs=in_specs, out_specs=out_specs, out_shape=out_shape,
        scratch_shapes=scratch,
        compiler_params=_cparams(("arbitrary",)),
        name=name,
    )(*([src] * (2 * len(col_blocks))), gtok, grow, ghead, gtok, grow, ghead, *states)


def _delta_scan(dqkv, gtok, grow, ghead, s0, n_ctx_tiles, tiles_per_seq):
    return _scan_call(_delta_kernel, "delta_scan", dqkv, (0, 1, 2), gtok, grow, ghead, [s0], [s0.shape],
                      [pltpu.VMEM((HW, HW), F32)] * 2, n_ctx_tiles, tiles_per_seq)


def _mlstm_scan(bgc, gtok, grow, ghead, cn0, mc0, mr0, n_ctx_tiles, tiles_per_seq):
    n_seq = cn0.shape[0]
    return _scan_call(_mlstm_kernel, "mlstm_scan", bgc, (1, 2, 3), gtok, grow, ghead, [cn0, mc0, mr0],
                      [cn0.shape, (n_seq, N_DIR, HW, LANES)],
                      [pltpu.VMEM((HW, 2 * HW), F32)] * 2 + [pltpu.VMEM((HW, 1), F32)] * 2
                      + [pltpu.VMEM((1, HW), F32)] * 2, n_ctx_tiles, tiles_per_seq)


def _merge_ffn_kernel(xc_ref, xl_ref, mod_ref, actx_ref, alat_ref, odf_ref, odb_ref, hmf_ref, hmb_ref, bg_ref,
                      co_ref, ag_ref, dg_ref, mg_ref, w_ref, g2_ref, wgu_ref, wdn_ref, fg_ref, *o_refs,
                      lam_init, n_ctx_tiles, final):
    is_ctx = pl.program_id(0) < n_ctx_tiles
    a = jnp.where(is_ctx, actx_ref[...], alat_ref[...])
    a = a * lax.rsqrt(_group_sumsq(a, A_V_DIM) * (1.0 / A_V_DIM) + EPS)
    a = a * ag_ref[...] * (1.0 - lam_init)
    od = odf_ref[...] + odb_ref[...]
    bg = bg_ref[...]
    d = od * lax.rsqrt(_group_sumsq(od, HEAD_DIM) * (1.0 / HEAD_DIM) + EPS) * dg_ref[...] * (bg * _sigmoid(bg))
    hm = hmf_ref[...] + hmb_ref[...]
    m = _sigmoid(co_ref[...]) * (hm * lax.rsqrt(_group_sumsq(hm, HEAD_DIM) * (1.0 / HEAD_DIM) + EPS) * mg_ref[...])
    cat = jnp.concatenate([a.astype(BF16), d.astype(BF16), m.astype(BF16)], axis=1)
    mix = jnp.dot(cat, w_ref[...], preferred_element_type=F32)
    x = jnp.where(is_ctx, xc_ref[...], xl_ref[...]) + mod_ref[:, 2 * D_MODEL:3 * D_MODEL] * mix
    y = x * lax.rsqrt(jnp.mean(x * x, axis=-1, keepdims=True) + EPS) * g2_ref[...]
    u = y * (1.0 + mod_ref[:, 4 * D_MODEL:5 * D_MODEL]) + mod_ref[:, 3 * D_MODEL:4 * D_MODEL]
    h = jnp.dot(u.astype(BF16), wgu_ref[...], preferred_element_type=F32)
    gate = h[:, 0:FFN_HIDDEN]
    act = (gate * _sigmoid(gate)) * h[:, FFN_HIDDEN:2 * FFN_HIDDEN]
    out = x + mod_ref[:, 5 * D_MODEL:6 * D_MODEL] * jnp.dot(act.astype(BF16), wdn_ref[...],
                                                             preferred_element_type=F32)
    if not final:
        o_refs[0][...] = out
        return
    out = out * lax.rsqrt(jnp.mean(out * out, axis=-1, keepdims=True) + EPS) * fg_ref[...]
    yc_ref, yl_ref = o_refs
    yl_ref[...] = out

    @pl.when(is_ctx)
    def _():
        yc_ref[...] = out


def _merge_ffn(x_all, mod_l, a_ctx, a_lat, od, hm, bgc, attn_g, delta_g, mlstm_g, w_out, lam_init,
               g2, w_gu, w_dn, final_g, final, n_ctx_tiles, tiles_per_seq):
    x_ctx, x_lat, lat_tile0, t_rows = x_all
    n_ctx_tiles, tiles_per_seq = _dense_tiles(n_ctx_tiles, tiles_per_seq)

    def mod_idx(i):
        return (jnp.where(i < n_ctx_tiles, 0, 1 + (i - n_ctx_tiles) // tiles_per_seq), 0, 0)

    full = lambda r, c: pl.BlockSpec((r, c), lambda i: (0, 0), pipeline_mode=pl.Buffered(1))
    row = lambda w, cb=0: pl.BlockSpec((DENSE_TILE, w), lambda i: (i, cb))
    a_w = HEADS * A_V_DIM
    if final:
        t_ctx = n_ctx_tiles * DENSE_TILE
        out_specs = [pl.BlockSpec((DENSE_TILE, D_MODEL), lambda i: (jnp.minimum(i, n_ctx_tiles - 1), 0)),
                     pl.BlockSpec((DENSE_TILE, D_MODEL), lambda i: (jnp.maximum(i - n_ctx_tiles, 0), 0))]
        out_shape = [jax.ShapeDtypeStruct((t_ctx, D_MODEL), F32),
                     jax.ShapeDtypeStruct((t_rows - t_ctx, D_MODEL), F32)]
    else:
        out_specs = [row(D_MODEL)]
        out_shape = [jax.ShapeDtypeStruct((t_rows, D_MODEL), F32)]
    return pl.pallas_call(
        functools.partial(_merge_ffn_kernel, lam_init=lam_init, n_ctx_tiles=n_ctx_tiles, final=final),
        grid=(t_rows // DENSE_TILE,),
        in_specs=_x_specs(n_ctx_tiles, lat_tile0) + [
                  pl.BlockSpec((None, 1, 6 * D_MODEL), mod_idx),
                  pl.BlockSpec((DENSE_TILE, a_w), lambda i: (jnp.minimum(i, n_ctx_tiles - 1), 0)),
                  pl.BlockSpec((DENSE_TILE, a_w), lambda i: (jnp.maximum(i - n_ctx_tiles, 0), 0)),
                  row(HW), row(HW), row(HW), row(HW), row(HW), row(HW, 4),
                  full(1, a_w), full(1, HW), full(1, HW), full(D_MODEL, D_MODEL),
                  full(1, D_MODEL), full(D_MODEL, 2 * FFN_HIDDEN), full(FFN_HIDDEN, D_MODEL),
                  full(1, D_MODEL)],
        out_specs=out_specs,
        out_shape=out_shape,
        compiler_params=_cparams(("arbitrary",)),
        name="merge_ffn",
    )(x_ctx, x_lat, mod_l.reshape(8, 1, 6 * D_MODEL), a_ctx, a_lat, od[0], od[1], hm[0], hm[1], bgc, bgc,
      jnp.tile(attn_g, HEADS).reshape(1, -1), jnp.tile(delta_g, HEADS).reshape(1, -1),
      jnp.tile(mlstm_g, HEADS).reshape(1, -1), w_out, g2.reshape(1, D_MODEL), w_gu, w_dn,
      final_g.reshape(1, D_MODEL))


def _rope_tables(dec_seq):
    n_rows = dec_seq // GRID_W
    rows = jnp.repeat(jnp.arange(n_rows, dtype=F32), GRID_W)
    cols = jnp.tile(jnp.arange(GRID_W, dtype=F32), n_rows)
    n_freq = A_QK_DIM // 4
    inv_freq = ROPE_BASE ** (-jnp.arange(n_freq, dtype=F32) / n_freq)
    ang = jnp.concatenate([rows[:, None] * inv_freq, cols[:, None] * inv_freq], axis=-1)
    cos = jnp.repeat(jnp.cos(ang), 2, axis=-1)
    sin = jnp.repeat(jnp.sin(ang), 2, axis=-1) * jnp.tile(jnp.array([-1.0, 1.0], F32), A_QK_DIM // 2)
    cos = jnp.concatenate([jnp.ones((DENSE_TILE, A_QK_DIM), F32), cos], axis=0)
    sin = jnp.concatenate([jnp.zeros((DENSE_TILE, A_QK_DIM), F32), sin], axis=0)
    return jnp.tile(cos, (1, 2)), jnp.tile(sin, (1, 2))


def _permute_proj(w_in_l, b_in_l):
    def perm(a):
        head, ba_bb, tail, ci_cf = a[..., 0:2560], a[..., 2560:2576], a[..., 2576:3600], a[..., 3600:3616]
        pad = jnp.zeros(a.shape[:-1] + (N_PROJ - 3616,), a.dtype)
        return jnp.concatenate([head, tail, ba_bb, ci_cf, pad], axis=-1)
    return perm(w_in_l).astype(BF16), perm(b_in_l)


def _gate_layouts(grow80, n_chunks):
    r = grow80.reshape(10, N_DIR, HEADS, n_chunks, CHUNK)
    tok = r[jnp.array([0, 2, 1, 4, 6])].transpose(1, 3, 4, 0, 2).reshape(N_DIR, n_chunks * CHUNK, 5 * HEADS)
    gtok = jnp.pad(tok, ((0, 0), (0, 0), (0, LANES - 5 * HEADS)))
    rows = r[jnp.array([0, 5, 8, 9])].transpose(1, 3, 0, 2, 4).reshape(N_DIR, n_chunks, 4, HW)
    grow = jnp.pad(rows, ((0, 0), (0, 0), (0, 4), (0, 0)))
    per_head = lambda q: r[q, :, :, :, 0].transpose(0, 2, 1)[..., None]
    zeros = lambda w: jnp.zeros((N_DIR, n_chunks, HEADS, w), F32)
    delta_rows = jnp.concatenate([zeros(CHUNK), per_head(3), zeros(LANES - CHUNK - 1)], axis=-1)
    mlstm_rows = jnp.concatenate([r[7].transpose(0, 2, 1, 3), per_head(8), per_head(9),
                                  zeros(LANES - CHUNK - 2)], axis=-1)
    return gtok, grow, jnp.concatenate([delta_rows, mlstm_rows], axis=2)


def _block_diag(s):
    eye = jnp.eye(HEADS, dtype=s.dtype)
    out = s[..., :, :, None, :] * eye[:, None, :, None]
    return out.reshape(s.shape[:-3] + (HW, HW))


def _block_diag_inv(s_bd):
    s6 = s_bd.reshape(s_bd.shape[:-2] + (HEADS, HEAD_DIM, HEADS, HEAD_DIM))
    return jnp.stack([s6[..., h, :, h, :] for h in range(HEADS)], axis=-3)


def _norm_block(n):
    return _block_diag(jnp.broadcast_to(n[..., None], n.shape + (HEAD_DIM,)))


def _with_zero_ctx(batch, lat_state):
    return jnp.concatenate([jnp.zeros((batch,) + lat_state.shape[1:], lat_state.dtype), lat_state], axis=0)


def kernel(x_prompt, x_sample, cache_attn_k, cache_attn_v, state_delta, state_mlstm_C, state_mlstm_n,
           state_mlstm_m, c, c_ctx, norm1_g, norm2_g, w_mod, b_mod, w_in, b_in, w_out, lambda_qk,
           attn_subln_g, delta_conv_w, delta_A_log, delta_dt_bias, delta_norm_g, mlstm_f_bias,
           mlstm_norm_g, w_gate_up, w_down, final_norm_g):
    batch, seq, _ = x_prompt.shape
    dec_batch, dec_seq, _ = x_sample.shape
    past_len = cache_attn_k.shape[2]
    assert seq == ROW_TILE and dec_seq % ROW_TILE == 0 and dec_batch + 1 <= 8
    t_ctx = batch * seq
    t_lat = dec_batch * dec_seq
    t_rows = t_ctx + t_lat
    n_ctx_tiles = t_ctx // ROW_TILE
    tiles_per_seq = dec_seq // ROW_TILE
    n_chunks = t_rows // CHUNK

    x_src = (x_prompt.reshape(t_ctx, D_MODEL), x_sample.reshape(t_lat, D_MODEL), 0, t_rows)
    cvecs = jnp.zeros((8, D_MODEL), F32).at[0].set(c_ctx).at[1:1 + dec_batch].set(c)
    mod = _ada_mod(cvecs, w_mod, b_mod)
    cos_t, sin_t = _rope_tables(dec_seq)

    ks_l, vs_l, sd_l, cm_l, nm_l, mm_l = [], [], [], [], [], []
    for l in range(DEPTH):
        lam_init = 0.8 - 0.6 * math.exp(-0.3 * l)
        w_p, b_p = _permute_proj(w_in[l], b_in[l])
        qk, va, kv32, bqkv, bgc, gates = _in_proj(*x_src, mod[l], norm1_g[l], w_p, b_p, cos_t, sin_t,
                                                 n_ctx_tiles, tiles_per_seq)
        ks_l.append(kv32[:t_ctx, 0:512].reshape(batch, seq, HEADS, 2, A_QK_DIM))
        vs_l.append(kv32[:t_ctx, 512:1024].reshape(batch, seq, HEADS, A_V_DIM))

        par = jnp.zeros((32, LANES), F32)
        par = par.at[0:8, 0].set(delta_A_log[l].reshape(-1)).at[0:8, 1].set(delta_dt_bias[l].reshape(-1))
        par = par.at[24:32, 1].set(mlstm_f_bias[l].reshape(-1))
        gtok, grow, ghead = _gate_layouts(_gate_prep(gates[:, 0:32].T, par), n_chunks)

        k_ctx = qk[:t_ctx, 512:1024].reshape(batch, seq, 512)
        v_ctx = va[:t_ctx].reshape(batch, seq, 512)
        a_ctx = _diff_attention(lambda_qk[l], qk, k_ctx, v_ctx, lam_init, 0, batch, seq, seq, seq)
        k_lat = jnp.concatenate([qk[t_ctx:, 512:1024].reshape(dec_batch, dec_seq, 512),
                                 cache_attn_k[:, l].reshape(dec_batch, past_len, 512).astype(BF16)], axis=1)
        v_lat = jnp.concatenate([va[t_ctx:].reshape(dec_batch, dec_seq, 512),
                                 cache_attn_v[:, l].reshape(dec_batch, past_len, 512).astype(BF16)], axis=1)
        a_lat = _diff_attention(lambda_qk[l], qk, k_lat, v_lat, lam_init, t_ctx, dec_batch, dec_seq, 512, 512)

        dqkv = _delta_conv(bqkv, delta_conv_w[l], n_ctx_tiles, tiles_per_seq)
        od_f, od_b, s_fin = _delta_scan(dqkv, gtok, grow, ghead,
                                        _with_zero_ctx(batch, _block_diag(state_delta[:, l])),
                                        n_ctx_tiles, tiles_per_seq)
        sd_l.append(_block_diag_inv(s_fin[:batch]))

        cn_lat = jnp.concatenate([_block_diag(state_mlstm_C[:, l]), _norm_block(state_mlstm_n[:, l])], axis=-1)
        m_lat = jnp.repeat(state_mlstm_m[:, l], HEAD_DIM, axis=-1)
        hm_f, hm_b, cn_fin, m_fin = _mlstm_scan(
            bgc, gtok, grow, ghead, _with_zero_ctx(batch, cn_lat), _with_zero_ctx(batch, m_lat[..., None]),
            _with_zero_ctx(batch, m_lat[:, :, None, :]), n_ctx_tiles, tiles_per_seq)
        cm_l.append(_block_diag_inv(cn_fin[:batch, :, :, 0:HW]))
        nm_l.append(_block_diag_inv(cn_fin[:batch, :, :, HW:2 * HW])[..., 0])
        mm_l.append(m_fin[:batch, :, ::HEAD_DIM, 0])

        outs = _merge_ffn(x_src, mod[l], a_ctx, a_lat, (od_f, od_b), (hm_f, hm_b), bgc, attn_subln_g[l],
                          delta_norm_g[l], mlstm_norm_g[l], w_out[l].astype(BF16), lam_init, norm2_g[l],
                          w_gate_up[l].astype(BF16), w_down[l].astype(BF16), final_norm_g, l == DEPTH - 1,
                          n_ctx_tiles, tiles_per_seq)
        x_src = (outs[0], outs[0], t_ctx // DENSE_TILE, t_rows)

    y_prompt = outs[0].reshape(batch, seq, D_MODEL)
    y_sample = outs[1].reshape(dec_batch, dec_seq, D_MODEL)
    return (y_prompt, y_sample, jnp.stack(ks_l, axis=1), jnp.stack(vs_l, axis=1), jnp.stack(sd_l, axis=1),
            jnp.stack(cm_l, axis=1), jnp.stack(nm_l, axis=1), jnp.stack(mm_l, axis=1))
```

```python
import functools
import math

import jax
import jax.numpy as jnp
from jax import lax
from jax.experimental import pallas as pl
from jax.experimental.pallas import tpu as pltpu

F32 = jnp.float32
BF16 = jnp.bfloat16

D_MODEL = 1024
DEPTH = 2
GRID_W = 64
N_DIR = 2
CHUNK = 64
ROPE_BASE = 10000.0
EPS = 1e-6
HEADS = 4
A_QK_DIM = 64
A_V_DIM = 128
HEAD_DIM = 64
CONV_K = 5
FFN_HIDDEN = 2816
ROW_TILE = 256
CHUNKS_PER_TILE = ROW_TILE // CHUNK
HW = HEADS * HEAD_DIM
LANES = 128
HALO = 8
BF16_ROWS = 16
GATE_GROUP = 32

N_PROJ = 3712
VMEM_LIMIT = 56 * 1024 * 1024


DENSE_TILE = 512


def _dense_tiles(n_ctx_tiles, tiles_per_seq):
    ratio = DENSE_TILE // ROW_TILE
    assert n_ctx_tiles % ratio == 0 and tiles_per_seq % ratio == 0
    return n_ctx_tiles // ratio, tiles_per_seq // ratio


def _cparams(sem):
    return pltpu.CompilerParams(dimension_semantics=sem, vmem_limit_bytes=VMEM_LIMIT)


def _bdot(a, b):
    return jnp.dot(a.astype(BF16), b.astype(BF16), preferred_element_type=F32)


def _bdot_nt(a, b):
    return lax.dot_general(a.astype(BF16), b.astype(BF16), (((1,), (1,)), ((), ())),
                           preferred_element_type=F32)


def _split3(x):
    hi = x.astype(BF16).astype(F32)
    r1 = x - hi
    mid = r1.astype(BF16).astype(F32)
    lo = (r1 - mid).astype(BF16).astype(F32)
    return hi, mid, lo


def _dot01(x, m01):
    d = functools.partial(jnp.dot, preferred_element_type=F32)
    hi, mid, lo = _split3(x)
    return d(hi.astype(BF16), m01) + d(mid.astype(BF16), m01) + d(lo.astype(BF16), m01)


def _group_sumsq(x, seg):
    sq = x * x
    out = []
    for t in range(x.shape[1] // LANES):
        tile = sq[:, t * LANES:(t + 1) * LANES]
        if seg == LANES:
            out.append(jnp.broadcast_to(jnp.sum(tile, axis=-1, keepdims=True), tile.shape))
        else:
            low = lax.broadcasted_iota(jnp.int32, tile.shape, 1) < seg
            s_lo = jnp.sum(jnp.where(low, tile, 0.0), axis=-1, keepdims=True)
            s_hi = jnp.sum(jnp.where(low, 0.0, tile), axis=-1, keepdims=True)
            out.append(jnp.where(low, s_lo, s_hi))
    return jnp.concatenate(out, axis=1)


def _sigmoid(x):
    return 1.0 / (1.0 + jnp.exp(-x))


def _softplus(x):
    return jnp.maximum(x, 0.0) + jnp.log1p(jnp.exp(-jnp.abs(x)))


def _ada_kernel(c_ref, w_ref, b_ref, o_ref):
    c = c_ref[...]
    s = c * _sigmoid(c)
    o_ref[...] = _bdot(s, w_ref[...]) + b_ref[...]


def _ada_mod(cvecs, w_mod, b_mod):
    n_out = w_mod.shape[-1]
    tn = 1024
    return pl.pallas_call(
        _ada_kernel,
        grid=(DEPTH, n_out // tn),
        in_specs=[pl.BlockSpec((8, D_MODEL), lambda l, j: (0, 0)),
                  pl.BlockSpec((None, D_MODEL, tn), lambda l, j: (l, 0, j)),
                  pl.BlockSpec((None, 1, tn), lambda l, j: (l, 0, j))],
        out_specs=pl.BlockSpec((None, 8, tn), lambda l, j: (l, 0, j)),
        out_shape=jax.ShapeDtypeStruct((DEPTH, 8, n_out), F32),
        compiler_params=_cparams(("parallel", "parallel")),
        name="ada_mod",
    )(cvecs, w_mod, b_mod.reshape(DEPTH, 1, n_out))


def _proj_kernel(xc_ref, xl_ref, mod_ref, g_ref, w_ref, b_ref, cos_ref, sin_ref,
                 qk_ref, vt_ref, kv32_ref, bqkv_ref, bgc_ref, gates_ref, *, n_ctx_tiles):
    x = jnp.where(pl.program_id(0) < n_ctx_tiles, xc_ref[...], xl_ref[...])
    y = x * lax.rsqrt(jnp.mean(x * x, axis=-1, keepdims=True) + EPS) * g_ref[...]
    u = y * (1.0 + mod_ref[:, D_MODEL:2 * D_MODEL]) + mod_ref[:, 0:D_MODEL]
    acc = _bdot(u, w_ref[...]) + b_ref[...]
    kv32_ref[...] = acc[:, 512:1536]
    vt_ref[...] = acc[:, 1024:1536].T.astype(BF16)
    bqkv_ref[...] = acc[:, 1536:2304]
    bgc_ref[...] = acc[:, 2304:3584]
    gates_ref[...] = acc[:, 3584:3712]
    cos = cos_ref[...]
    sin = sin_ref[...]
    even = (lax.broadcasted_iota(jnp.int32, cos.shape, 1) % 2) == 0
    for j in range(8):
        xj = acc[:, j * LANES:(j + 1) * LANES]
        swapped = jnp.where(even, pltpu.roll(xj, LANES - 1, 1), pltpu.roll(xj, 1, 1))
        r = xj * cos + swapped * sin
        if j < 4:
            r = r * (A_QK_DIM ** -0.5)
        qk_ref[:, j * LANES:(j + 1) * LANES] = r.astype(BF16)


def _x_specs(n_ctx_tiles, lat_tile0):
    return [pl.BlockSpec((DENSE_TILE, D_MODEL), lambda i: (jnp.minimum(i, n_ctx_tiles - 1), 0)),
            pl.BlockSpec((DENSE_TILE, D_MODEL), lambda i: (jnp.maximum(i - n_ctx_tiles, 0) + lat_tile0, 0))]


def _in_proj(x_ctx, x_lat, lat_tile0, t_rows, mod_l, g1, w_p, b_p, cos_t, sin_t, n_ctx_tiles, tiles_per_seq):
    n_tiles = t_rows // DENSE_TILE
    n_ctx_tiles, tiles_per_seq = _dense_tiles(n_ctx_tiles, tiles_per_seq)

    def mod_idx(i):
        return (jnp.where(i < n_ctx_tiles, 0, 1 + (i - n_ctx_tiles) // tiles_per_seq), 0, 0)

    def rope_idx(i):
        return (jnp.where(i < n_ctx_tiles, 0, 1 + (i - n_ctx_tiles) % tiles_per_seq), 0)

    row = lambda w: pl.BlockSpec((DENSE_TILE, w), lambda i: (i, 0))
    outs = [(1024, BF16, False), (512, BF16, True), (1024, F32, False), (768, F32, False), (1280, F32, False),
            (LANES, F32, False)]
    return pl.pallas_call(
        functools.partial(_proj_kernel, n_ctx_tiles=n_ctx_tiles),
        grid=(n_tiles,),
        in_specs=_x_specs(n_ctx_tiles, lat_tile0) + [
                  pl.BlockSpec((None, 1, 6 * D_MODEL), mod_idx),
                  pl.BlockSpec((1, D_MODEL), lambda i: (0, 0)),
                  pl.BlockSpec((D_MODEL, N_PROJ), lambda i: (0, 0)),
                  pl.BlockSpec((1, N_PROJ), lambda i: (0, 0)),
                  pl.BlockSpec((DENSE_TILE, LANES), rope_idx),
                  pl.BlockSpec((DENSE_TILE, LANES), rope_idx)],
        out_specs=[pl.BlockSpec((w, DENSE_TILE), lambda i: (0, i)) if tr else row(w) for w, _, tr in outs],
        out_shape=[jax.ShapeDtypeStruct((w, t_rows) if tr else (t_rows, w), dt) for w, dt, tr in outs],
        compiler_params=_cparams(("parallel",)),
        name="in_proj",
    )(x_ctx, x_lat, mod_l.reshape(8, 1, 6 * D_MODEL), g1.reshape(1, D_MODEL), w_p, b_p.reshape(1, N_PROJ),
      cos_t, sin_t)


GATE_ROWS = 80


def _gates_kernel(g_ref, par_ref, o_ref):
    x = g_ref[...]
    alog = par_ref[0:8, 0:1]
    dtb = par_ref[0:8, 1:2]
    fb = par_ref[24:32, 1:2]
    g = -jnp.exp(alog) * _softplus(x[0:8] + dtb)
    beta = _sigmoid(x[8:16])
    ig = x[16:24]
    lf = -_softplus(-(x[24:32] + fb))
    tl = x.shape[1]
    r = lax.broadcasted_iota(jnp.int32, (LANES, LANES), 0)
    c = lax.broadcasted_iota(jnp.int32, (LANES, LANES), 1)
    same = (r // CHUNK) == (c // CHUNK)
    pre = jnp.where(same & (r <= c), 1.0, 0.0).astype(BF16)
    suf = jnp.where(same & (r >= c), 1.0, 0.0).astype(BF16)
    tot = jnp.where(same, 1.0, 0.0).astype(BF16)
    m01 = jnp.concatenate([pre, suf, tot], axis=1)
    fwd16 = (lax.broadcasted_iota(jnp.int32, (16, LANES), 0) % 8) < 4
    fwd8 = lax.broadcasted_iota(jnp.int32, (8, LANES), 0) < 4
    pos = lax.broadcasted_iota(jnp.int32, (8, LANES), 1) % CHUNK

    def running_max(v):
        pf = sf = v
        s = 1
        while s < CHUNK:
            pf = jnp.maximum(pf, jnp.where(pos >= s, pltpu.roll(pf, s, 1), -jnp.inf))
            sf = jnp.maximum(sf, jnp.where(pos < CHUNK - s, pltpu.roll(sf, LANES - s, 1), -jnp.inf))
            s *= 2
        return pf, sf

    o_ref[16:24, :] = beta
    for j in range(tl // LANES):
        sl = slice(j * LANES, (j + 1) * LANES)
        xs = jnp.concatenate([g[:, sl], lf[:, sl]], axis=0)
        cs = _dot01(xs, m01)
        cum = jnp.where(fwd16, cs[:, 0:LANES], cs[:, LANES:2 * LANES])
        total = cs[:, 2 * LANES:3 * LANES]
        gc, gtot, b, btot = cum[0:8], total[0:8], cum[8:16], total[8:16]
        rr = ig[:, sl] - b
        kvl = btot - b + ig[:, sl]
        r_pf, r_sf = running_max(rr)
        k_pf, k_sf = running_max(kvl)
        o_ref[0:8, sl] = gc
        o_ref[8:16, sl] = gtot - gc
        o_ref[24:32, sl] = gtot
        o_ref[32:40, sl] = b
        o_ref[40:48, sl] = rr
        o_ref[48:56, sl] = jnp.where(fwd8, r_pf, r_sf)
        o_ref[56:64, sl] = kvl
        o_ref[64:72, sl] = btot
        o_ref[72:80, sl] = jnp.maximum(k_pf, k_sf)


def _gate_prep(gates_t, par):
    t_rows = gates_t.shape[1]
    tl = math.gcd(t_rows, 2048)
    return pl.pallas_call(
        _gates_kernel,
        grid=(t_rows // tl,),
        in_specs=[pl.BlockSpec((32, tl), lambda i: (0, i)),
                  pl.BlockSpec((32, LANES), lambda i: (0, 0))],
        out_specs=pl.BlockSpec((GATE_ROWS, tl), lambda i: (0, i)),
        out_shape=jax.ShapeDtypeStruct((GATE_ROWS, t_rows), F32),
        compiler_params=_cparams(("parallel",)),
        name="gate_prep",
    )(gates_t, par)


def _conv_kernel(x_ref, p_ref, n_ref, w_ref, o_ref, ext_s, *, n_ctx_tiles, tiles_per_seq):
    i = pl.program_id(0)
    j = (i - n_ctx_tiles) % tiles_per_seq
    is_ctx = i < n_ctx_tiles
    first = jnp.logical_or(is_ctx, j == 0)
    last = jnp.logical_or(is_ctx, j == tiles_per_seq - 1)
    ext_s[0:HALO, :] = jnp.where(first, 0.0, p_ref[...])
    ext_s[HALO:HALO + ROW_TILE, :] = x_ref[...]
    ext_s[HALO + ROW_TILE:, :] = jnp.where(last, 0.0, n_ref[...])
    w = w_ref[...]
    y = None
    for k in range(CONV_K):
        off = HALO - CONV_K // 2 + k
        term = ext_s[off:off + ROW_TILE, :] * w[k:k + 1, :]
        y = term if y is None else y + term
    y = y * _sigmoid(y)
    q = y[:, 0:HW]
    k_ = y[:, HW:2 * HW]
    o_ref[:, 0:HW] = q * lax.rsqrt(_group_sumsq(q, HEAD_DIM) + EPS) * (HEAD_DIM ** -0.5)
    o_ref[:, HW:2 * HW] = k_ * lax.rsqrt(_group_sumsq(k_, HEAD_DIM) + EPS)
    o_ref[:, 2 * HW:3 * HW] = y[:, 2 * HW:3 * HW]


def _delta_conv(bqkv, conv_w, n_ctx_tiles, tiles_per_seq):
    t_rows, width = bqkv.shape
    n_tiles = t_rows // ROW_TILE
    hb = ROW_TILE // HALO
    n_hb = t_rows // HALO
    return pl.pallas_call(
        functools.partial(_conv_kernel, n_ctx_tiles=n_ctx_tiles, tiles_per_seq=tiles_per_seq),
        grid=(n_tiles,),
        in_specs=[pl.BlockSpec((ROW_TILE, width), lambda i: (i, 0)),
                  pl.BlockSpec((HALO, width), lambda i: (jnp.maximum(i * hb - 1, 0), 0)),
                  pl.BlockSpec((HALO, width), lambda i: (jnp.minimum((i + 1) * hb, n_hb - 1), 0)),
                  pl.BlockSpec((CONV_K, width), lambda i: (0, 0))],
        out_specs=pl.BlockSpec((ROW_TILE, width), lambda i: (i, 0)),
        out_shape=jax.ShapeDtypeStruct((t_rows, width), F32),
        scratch_shapes=[pltpu.VMEM((ROW_TILE + 2 * HALO, width), F32)],
        compiler_params=_cparams(("parallel",)),
        name="delta_conv",
    )(bqkv, bqkv, bqkv, conv_w)


def _attn_kernel(lqk_ref, q_ref, *refs, lam_init, tq, tk):
    o_ref = refs[-1]
    q = q_ref[...]
    lane = lax.broadcasted_iota(jnp.int32, q.shape, 1)
    zero = jnp.zeros_like(q)
    q2 = jnp.concatenate([jnp.where(lane < A_QK_DIM, q, zero), jnp.where(lane >= A_QK_DIM, q, zero)], axis=0)
    tiles = [(k_ref, vt_ref, j) for k_ref, vt_ref in zip(refs[0:-1:2], refs[1:-1:2])
             for j in range(k_ref.shape[0] // tk)]
    ones = jnp.ones((BF16_ROWS, tk), BF16)

    def scores(t):
        k_ref, _, j = tiles[t]
        return lax.dot_general(k_ref[j * tk:(j + 1) * tk, :], q2, (((1,), (1,)), ((), ())),
                               preferred_element_type=F32)

    m = acc = None
    n_kv = len(tiles)
    st_next = scores(0)
    for j in range(n_kv):
        st = st_next
        if j + 1 < n_kv:
            st_next = scores(j + 1)
        m_new = jnp.max(st, axis=0, keepdims=True)
        if j > 0:
            m_new = jnp.maximum(m, m_new)
        p = jnp.exp(st - m_new)
        _, vt_ref, jt = tiles[j]
        vt = jnp.concatenate([vt_ref[:, jt * tk:(jt + 1) * tk], ones], axis=0)
        pv = jnp.dot(vt, p.astype(BF16), preferred_element_type=F32)
        acc = pv if j == 0 else jnp.exp(m - m_new) * acc + pv
        m = m_new
    o = acc[0:A_V_DIM] / acc[A_V_DIM:A_V_DIM + 1]
    lq = lqk_ref[...]
    lam = (jnp.exp(jnp.sum(lq[0:1] * lq[1:2], axis=-1, keepdims=True))
           - jnp.exp(jnp.sum(lq[2:3] * lq[3:4], axis=-1, keepdims=True)) + lam_init)
    o_ref[...] = (o[:, 0:tq] - lam * o[:, tq:2 * tq]).T


def _diff_attention(lqk, qk, vt, cache_k, cache_v, lam_init, q_row0, n_seq, lq, tq, tk):
    assert q_row0 % lq == 0
    qb0 = q_row0 // tq
    sb0 = q_row0 // lq
    nq = lq // tq
    in_specs = [pl.BlockSpec((4, A_QK_DIM), lambda b, h, i: (0, 0)),
                pl.BlockSpec((tq, LANES), lambda b, h, i: (qb0 + b * nq + i, h)),
                pl.BlockSpec((lq, LANES), lambda b, h, i: (sb0 + b, HEADS + h)),
                pl.BlockSpec((A_V_DIM, lq), lambda b, h, i: (h, sb0 + b))]
    args = [lqk, qk, qk, vt]
    if cache_k is not None:
        past = cache_k.shape[1]
        in_specs += [pl.BlockSpec((None, past, LANES), lambda b, h, i: (b, 0, h)),
                     pl.BlockSpec((None, A_V_DIM, past), lambda b, h, i: (b, h, 0))]
        args += [cache_k.astype(BF16), cache_v.astype(BF16).transpose(0, 2, 1)]
    return pl.pallas_call(
        functools.partial(_attn_kernel, lam_init=lam_init, tq=tq, tk=tk),
        grid=(n_seq, HEADS, nq),
        in_specs=in_specs,
        out_specs=pl.BlockSpec((tq, LANES), lambda b, h, i: (b * nq + i, h)),
        out_shape=jax.ShapeDtypeStruct((n_seq * lq, HEADS * A_V_DIM), F32),
        compiler_params=_cparams(("parallel", "parallel", "parallel")),
        name="diff_attention",
    )(*args)


def _same_head(n_cols=HW):
    r = lax.broadcasted_iota(jnp.int32, (HW, n_cols), 0) // HEAD_DIM
    c = (lax.broadcasted_iota(jnp.int32, (HW, n_cols), 1) % HW) // HEAD_DIM
    return r == c


def _bd(x, same):
    return jnp.where(same, jnp.concatenate([x, x, x, x], axis=0), 0.0)


def _cat_index():
    i = lax.broadcasted_iota(jnp.int32, (CHUNK, HW), 0)
    j = lax.broadcasted_iota(jnp.int32, (CHUNK, HW), 1) % CHUNK
    return i, j


def _scan_masks(direction):
    i, j = _cat_index()
    diff = (i - j) * (1 - 2 * direction)
    return diff >= 0, diff > 0


def _expand_matrix(first_lane, n_q):
    r = lax.broadcasted_iota(jnp.int32, (LANES, n_q * HW), 0)
    c = lax.broadcasted_iota(jnp.int32, (LANES, n_q * HW), 1)
    s = r % GATE_GROUP - first_lane
    hit = (r < 3 * GATE_GROUP) & (s >= 0) & (s < 4 * n_q) & (s // HEADS == c // HW) \
        & (s % HEADS == (c % HW) // HEAD_DIM)
    return jnp.where(hit, 1.0, 0.0).astype(BF16)


def _expand(tok, e3):
    hi, mid, lo = _split3(tok)
    x3 = hi + pltpu.roll(mid, GATE_GROUP, 1) + pltpu.roll(lo, 2 * GATE_GROUP, 1)
    return jnp.dot(x3.astype(BF16), e3, preferred_element_type=F32)


def _head_rows(hd, lane):
    return jnp.concatenate([jnp.broadcast_to(hd[h:h + 1, lane:lane + 1], (HEAD_DIM, 1)) for h in range(HEADS)],
                           axis=0)


def _pad_rows(x):
    return jnp.concatenate([x, jnp.zeros_like(x)], axis=0)


def _unit_tri_inverse(mats, same):
    i, j = _cat_index()
    b16 = (i // 16) == (j // 16)
    b32 = (i // 32) == (j // 32)
    eye = jnp.where(i == j, 1.0, 0.0)
    xs = [jnp.where(b16, a, 0.0) for a in mats]
    ts = [eye - x for x in xs]
    xbd = [_bd(x, same) for x in xs]
    for _ in range(3):
        xs = [_bdot(x, b) for x, b in zip(xs, xbd)]
        xbd = [_bd(x, same) for x in xs]
        ts = [t + _bdot(t, b) for t, b in zip(ts, xbd)]
    for inside in (jnp.logical_and(b32, jnp.logical_not(b16)), jnp.logical_not(b32)):
        mids = [_bdot(t, _bd(jnp.where(inside, a, 0.0), same)) for t, a in zip(ts, mats)]
        ts = [t - _bdot(m, _bd(t, same)) for t, m in zip(ts, mids)]
    return ts


def _scan_job(n_ctx_tiles, tiles_per_seq):
    j = pl.program_id(0)
    t = (j - n_ctx_tiles) % tiles_per_seq
    is_ctx = j < n_ctx_tiles
    return jnp.logical_or(is_ctx, t == 0), jnp.logical_or(is_ctx, t == tiles_per_seq - 1)


def _chunks_of(refs_f, refs_b):
    out = []
    for d, (q, k, v, tok, rows, heads) in enumerate((refs_f, refs_b)):
        for c in range(CHUNKS_PER_TILE):
            sl = slice(c * CHUNK, (c + 1) * CHUNK)
            out.append((q[sl, :], k[sl, :], v[sl, :], tok[sl, :], rows[c], heads[c], d))
    return out


def _delta_prep(chunks):
    same = _same_head()
    masks = [_scan_masks(0), _scan_masks(1)]
    e3 = _expand_matrix(0, 3)
    n = len(chunks)
    ex = [_expand(ch[3], e3) for ch in chunks]
    gc = [e[:, 0:HW] for e in ex]
    kb = [chunks[i][1] * ex[i][:, HW:2 * HW] for i in range(n)]
    kk = [_bdot_nt(jnp.concatenate([kb[i], chunks[i][0]], axis=0), _bd(chunks[i][1], same)) for i in range(n)]
    e_incl = [jnp.exp(jnp.where(masks[chunks[i][6]][0], gc[i] - chunks[i][4][0:1, :], -jnp.inf)) for i in range(n)]
    a = [jnp.where(masks[chunks[i][6]][1], kk[i][0:CHUNK] * e_incl[i], 0.0) for i in range(n)]
    aqk = [kk[i][CHUNK:2 * CHUNK] * e_incl[i] for i in range(n)]
    tinv = _unit_tri_inverse(a, same)
    egc = [jnp.exp(g) for g in gc]
    rhs = [jnp.concatenate([_bd(chunks[i][2] * ex[i][:, HW:2 * HW], same), _bd(kb[i] * egc[i], same)], axis=1)
           for i in range(n)]
    uw = [_bdot(tinv[i], rhs[i]) for i in range(n)]
    kd_t = [_pad_rows(chunks[i][1] * jnp.exp(ex[i][:, 2 * HW:3 * HW])).T for i in range(n)]
    gl = [jnp.exp(_head_rows(chunks[i][5][0:HEADS], CHUNK)) for i in range(n)]
    return [(uw[i][:, 0:HW], uw[i][:, HW:2 * HW], chunks[i][0] * egc[i], kd_t[i], aqk[i], gl[i])
            for i in range(n)]


def _delta_step(pre, s_ref, o_ref, c, same):
    u, w, qd, kd_t, aqk, gl = pre
    s = s_ref[...]
    wq = _bdot(jnp.concatenate([w, qd], axis=0), s)
    v_new = u - wq[0:CHUNK]
    o_ref[c * CHUNK:(c + 1) * CHUNK, :] = wq[CHUNK:2 * CHUNK] + _bdot(aqk, _bd(v_new, same))
    s_ref[...] = s * gl + jnp.where(same, _bdot(kd_t, _pad_rows(v_new)), 0.0)


def _delta_kernel(qf, kf, vf, qb, kb, vb, tokf, rowf, headf, tokb, rowb, headb, s0_ref,
                  of_ref, ob_ref, sout_ref, sf_s, sb_s, *, n_ctx_tiles, tiles_per_seq):
    first, last = _scan_job(n_ctx_tiles, tiles_per_seq)

    @pl.when(first)
    def _():
        sf_s[...] = s0_ref[0]
        sb_s[...] = s0_ref[1]

    pre = _delta_prep(_chunks_of((qf, kf, vf, tokf, rowf, headf), (qb, kb, vb, tokb, rowb, headb)))
    same = _same_head()
    for i in range(CHUNKS_PER_TILE):
        cb = CHUNKS_PER_TILE - 1 - i
        _delta_step(pre[i], sf_s, of_ref, i, same)
        _delta_step(pre[CHUNKS_PER_TILE + cb], sb_s, ob_ref, cb, same)

    @pl.when(last)
    def _():
        sout_ref[0] = sf_s[...]
        sout_ref[1] = sb_s[...]


def _mlstm_prep(chunks):
    same = _same_head()
    masks = [_scan_masks(0), _scan_masks(1)]
    e3 = _expand_matrix(12, 2)
    n = len(chunks)
    ex = [_expand(ch[3], e3) for ch in chunks]
    b = [e[:, 0:HW] for e in ex]
    ks = [ch[1] * (HEAD_DIM ** -0.5) for ch in chunks]
    qk = [_bdot_nt(chunks[i][0], _bd(ks[i], same)) for i in range(n)]
    dmat = [jnp.where(masks[chunks[i][6]][0], b[i] + chunks[i][4][1:2, :], -jnp.inf) for i in range(n)]
    dmax = [ex[i][:, 0:HW] + ex[i][:, HW:2 * HW] for i in range(n)]
    k_t = [_pad_rows(k).T for k in ks]
    ones = jnp.ones((CHUNK, HW), F32)
    vaug = [_pad_rows(jnp.concatenate([ch[2], ones], axis=1)) for ch in chunks]
    vbd = [jnp.concatenate([_bd(ch[2], same), jnp.where(same, 1.0, 0.0)], axis=1) for ch in chunks]
    kvl = [jnp.concatenate([jnp.broadcast_to(ch[5][HEADS + h:HEADS + h + 1, :], (HEAD_DIM, LANES))
                            for h in range(HEADS)], axis=0) for ch in chunks]
    btot_col = [_head_rows(ch[5][HEADS:2 * HEADS], CHUNK) for ch in chunks]
    mkv_col = [_head_rows(ch[5][HEADS:2 * HEADS], CHUNK + 1) for ch in chunks]
    return [(chunks[i][0], k_t[i], vaug[i], vbd[i], b[i], dmat[i], dmax[i], qk[i], kvl[i], btot_col[i],
             mkv_col[i], chunks[i][4][2:3, :], chunks[i][4][3:4, :]) for i in range(n)]


def _mlstm_step(pre, cn_ref, mcol_ref, mrow_ref, o_ref, c, same2):
    q, k_t, vaug, vbd, b, dmat, dmax, qk, kvl, btot_col, mkv_col, btot_row, mkv_row = pre
    cn = cn_ref[...]
    ms_col = mcol_ref[...]
    ms_row = mrow_ref[...]
    inter = b + ms_row
    m_t = jnp.maximum(inter, dmax)
    s = qk * jnp.exp(dmat - m_t)
    w_inter = jnp.exp(inter - m_t)
    nd = jnp.concatenate([w_inter, w_inter], axis=1) * _bdot(q, cn) + _bdot(s, vbd)
    o_ref[c * CHUNK:(c + 1) * CHUNK, :] = nd[:, 0:HW] / jnp.maximum(jnp.abs(nd[:, HW:2 * HW]), jnp.exp(-m_t))
    m_new = jnp.maximum(btot_col + ms_col, mkv_col)
    lane = lax.broadcasted_iota(jnp.int32, kvl.shape, 1)
    wk = jnp.where(lane < CHUNK, jnp.exp(kvl - m_new), 0.0)
    cn_ref[...] = jnp.exp(btot_col + ms_col - m_new) * cn + jnp.where(same2, _bdot(k_t * wk, vaug), 0.0)
    mcol_ref[...] = m_new
    mrow_ref[...] = jnp.maximum(btot_row + ms_row, mkv_row)


def _mlstm_kernel(qf, kf, vf, qb, kb, vb, tokf, rowf, headf, tokb, rowb, headb, cn0_ref, mc0_ref, mr0_ref,
                  of_ref, ob_ref, cnout_ref, mout_ref, cnf_s, cnb_s, mcf_s, mcb_s, mrf_s, mrb_s,
                  *, n_ctx_tiles, tiles_per_seq):
    first, last = _scan_job(n_ctx_tiles, tiles_per_seq)

    @pl.when(first)
    def _():
        cnf_s[...] = cn0_ref[0]
        cnb_s[...] = cn0_ref[1]
        mcf_s[...] = mc0_ref[0]
        mcb_s[...] = mc0_ref[1]
        mrf_s[...] = mr0_ref[0]
        mrb_s[...] = mr0_ref[1]

    pre = _mlstm_prep(_chunks_of((qf, kf, vf, tokf, rowf, headf), (qb, kb, vb, tokb, rowb, headb)))
    same2 = _same_head(2 * HW)
    for i in range(CHUNKS_PER_TILE):
        cb = CHUNKS_PER_TILE - 1 - i
        _mlstm_step(pre[i], cnf_s, mcf_s, mrf_s, of_ref, i, same2)
        _mlstm_step(pre[CHUNKS_PER_TILE + cb], cnb_s, mcb_s, mrb_s, ob_ref, cb, same2)

    @pl.when(last)
    def _():
        cnout_ref[0] = cnf_s[...]
        cnout_ref[1] = cnb_s[...]
        mout_ref[0] = jnp.broadcast_to(mcf_s[...], (HW, LANES))
        mout_ref[1] = jnp.broadcast_to(mcb_s[...], (HW, LANES))


def _scan_call(kernel_fn, name, src, col_blocks, gtok, grow, ghead, states, state_out_shapes, scratch,
               n_ctx_tiles, tiles_per_seq):
    t_rows = src.shape[0]
    n_tiles = t_rows // ROW_TILE

    def seq_of(j):
        return jnp.where(j < n_ctx_tiles, j, n_ctx_tiles + (j - n_ctx_tiles) // tiles_per_seq)

    def back(j):
        jj = j - n_ctx_tiles
        mirrored = n_ctx_tiles + (jj // tiles_per_seq) * tiles_per_seq + tiles_per_seq - 1 - jj % tiles_per_seq
        return jnp.where(j < n_ctx_tiles, j, mirrored)

    def tile_spec(cb, bwd):
        return pl.BlockSpec((ROW_TILE, HW), (lambda j: (back(j), cb)) if bwd else (lambda j: (j, cb)))

    def gate_specs(d):
        tile = (lambda j: back(j)) if d else (lambda j: j)
        return [pl.BlockSpec((None, ROW_TILE, LANES), lambda j: (d, tile(j), 0)),
                pl.BlockSpec((None, CHUNKS_PER_TILE, 8, HW), lambda j: (d, tile(j), 0, 0)),
                pl.BlockSpec((None, CHUNKS_PER_TILE, 8, LANES), lambda j: (d, tile(j), 0, 0))]

    def state_spec(shape):
        return pl.BlockSpec((None,) + tuple(shape), lambda j: (seq_of(j),) + (0,) * len(shape))

    in_specs = ([tile_spec(cb, False) for cb in col_blocks] + [tile_spec(cb, True) for cb in col_blocks]
                + gate_specs(0) + gate_specs(1) + [state_spec(s.shape[1:]) for s in states])
    out_specs = ([pl.BlockSpec((ROW_TILE, HW), lambda j: (j, 0)),
                  pl.BlockSpec((ROW_TILE, HW), lambda j: (back(j), 0))]
                 + [state_spec(s[1:]) for s in state_out_shapes])
    out_shape = ([jax.ShapeDtypeStruct((t_rows, HW), F32)] * 2
                 + [jax.ShapeDtypeStruct(tuple(s), F32) for s in state_out_shapes])
    return pl.pallas_call(
        functools.partial(kernel_fn, n_ctx_tiles=n_ctx_tiles, tiles_per_seq=tiles_per_seq),
        grid=(n_tiles,),
        in_specs=in_specs, out_specs=out_specs, out_shape=out_shape,
        scratch_shapes=scratch,
        compiler_params=_cparams(("arbitrary",)),
        name=name,
    )(*([src] * (2 * len(col_blocks))), gtok, grow, ghead, gtok, grow, ghead, *states)


def _delta_scan(dqkv, gtok, grow, ghead, s0, n_ctx_tiles, tiles_per_seq):
    return _scan_call(_delta_kernel, "delta_scan", dqkv, (0, 1, 2), gtok, grow, ghead, [s0], [s0.shape],
                      [pltpu.VMEM((HW, HW), F32)] * 2, n_ctx_tiles, tiles_per_seq)


def _mlstm_scan(bgc, gtok, grow, ghead, cn0, mc0, mr0, n_ctx_tiles, tiles_per_seq):
    n_seq = cn0.shape[0]
    return _scan_call(_mlstm_kernel, "mlstm_scan", bgc, (1, 2, 3), gtok, grow, ghead, [cn0, mc0, mr0],
                      [cn0.shape, (n_seq, N_DIR, HW, LANES)],
                      [pltpu.VMEM((HW, 2 * HW), F32)] * 2 + [pltpu.VMEM((HW, 1), F32)] * 2
                      + [pltpu.VMEM((1, HW), F32)] * 2, n_ctx_tiles, tiles_per_seq)


def _merge_ffn_kernel(xc_ref, xl_ref, mod_ref, actx_ref, alat_ref, odf_ref, odb_ref, hmf_ref, hmb_ref, bg_ref,
                      co_ref, ag_ref, dg_ref, mg_ref, w_ref, g2_ref, wgu_ref, wdn_ref, fg_ref, *o_refs,
                      lam_init, n_ctx_tiles, final):
    is_ctx = pl.program_id(0) < n_ctx_tiles
    a = jnp.where(is_ctx, actx_ref[...], alat_ref[...])
    a = a * lax.rsqrt(_group_sumsq(a, A_V_DIM) * (1.0 / A_V_DIM) + EPS)
    a = a * ag_ref[...] * (1.0 - lam_init)
    od = odf_ref[...] + odb_ref[...]
    bg = bg_ref[...]
    d = od * lax.rsqrt(_group_sumsq(od, HEAD_DIM) * (1.0 / HEAD_DIM) + EPS) * dg_ref[...] * (bg * _sigmoid(bg))
    hm = hmf_ref[...] + hmb_ref[...]
    m = _sigmoid(co_ref[...]) * (hm * lax.rsqrt(_group_sumsq(hm, HEAD_DIM) * (1.0 / HEAD_DIM) + EPS) * mg_ref[...])
    cat = jnp.concatenate([a.astype(BF16), d.astype(BF16), m.astype(BF16)], axis=1)
    mix = jnp.dot(cat, w_ref[...], preferred_element_type=F32)
    x = jnp.where(is_ctx, xc_ref[...], xl_ref[...]) + mod_ref[:, 2 * D_MODEL:3 * D_MODEL] * mix
    y = x * lax.rsqrt(jnp.mean(x * x, axis=-1, keepdims=True) + EPS) * g2_ref[...]
    u = y * (1.0 + mod_ref[:, 4 * D_MODEL:5 * D_MODEL]) + mod_ref[:, 3 * D_MODEL:4 * D_MODEL]
    h = jnp.dot(u.astype(BF16), wgu_ref[...], preferred_element_type=F32)
    gate = h[:, 0:FFN_HIDDEN]
    act = (gate * _sigmoid(gate)) * h[:, FFN_HIDDEN:2 * FFN_HIDDEN]
    out = x + mod_ref[:, 5 * D_MODEL:6 * D_MODEL] * jnp.dot(act.astype(BF16), wdn_ref[...],
                                                             preferred_element_type=F32)
    if not final:
        o_refs[0][...] = out
        return
    out = out * lax.rsqrt(jnp.mean(out * out, axis=-1, keepdims=True) + EPS) * fg_ref[...]
    yc_ref, yl_ref = o_refs
    yl_ref[...] = out

    @pl.when(is_ctx)
    def _():
        yc_ref[...] = out


def _merge_ffn(x_all, mod_l, a_ctx, a_lat, od, hm, bgc, attn_g, delta_g, mlstm_g, w_out, lam_init,
               g2, w_gu, w_dn, final_g, final, n_ctx_tiles, tiles_per_seq):
    x_ctx, x_lat, lat_tile0, t_rows = x_all
    n_ctx_tiles, tiles_per_seq = _dense_tiles(n_ctx_tiles, tiles_per_seq)

    def mod_idx(i):
        return (jnp.where(i < n_ctx_tiles, 0, 1 + (i - n_ctx_tiles) // tiles_per_seq), 0, 0)

    full = lambda r, c: pl.BlockSpec((r, c), lambda i: (0, 0), pipeline_mode=pl.Buffered(1))
    row = lambda w, cb=0: pl.BlockSpec((DENSE_TILE, w), lambda i: (i, cb))
    a_w = HEADS * A_V_DIM
    if final:
        t_ctx = n_ctx_tiles * DENSE_TILE
        out_specs = [pl.BlockSpec((DENSE_TILE, D_MODEL), lambda i: (jnp.minimum(i, n_ctx_tiles - 1), 0)),
                     pl.BlockSpec((DENSE_TILE, D_MODEL), lambda i: (jnp.maximum(i - n_ctx_tiles, 0), 0))]
        out_shape = [jax.ShapeDtypeStruct((t_ctx, D_MODEL), F32),
                     jax.ShapeDtypeStruct((t_rows - t_ctx, D_MODEL), F32)]
    else:
        out_specs = [row(D_MODEL)]
        out_shape = [jax.ShapeDtypeStruct((t_rows, D_MODEL), F32)]
    return pl.pallas_call(
        functools.partial(_merge_ffn_kernel, lam_init=lam_init, n_ctx_tiles=n_ctx_tiles, final=final),
        grid=(t_rows // DENSE_TILE,),
        in_specs=_x_specs(n_ctx_tiles, lat_tile0) + [
                  pl.BlockSpec((None, 1, 6 * D_MODEL), mod_idx),
                  pl.BlockSpec((DENSE_TILE, a_w), lambda i: (jnp.minimum(i, n_ctx_tiles - 1), 0)),
                  pl.BlockSpec((DENSE_TILE, a_w), lambda i: (jnp.maximum(i - n_ctx_tiles, 0), 0)),
                  row(HW), row(HW), row(HW), row(HW), row(HW), row(HW, 4),
                  full(1, a_w), full(1, HW), full(1, HW), full(D_MODEL, D_MODEL),
                  full(1, D_MODEL), full(D_MODEL, 2 * FFN_HIDDEN), full(FFN_HIDDEN, D_MODEL),
                  full(1, D_MODEL)],
        out_specs=out_specs,
        out_shape=out_shape,
        compiler_params=_cparams(("arbitrary",)),
        name="merge_ffn",
    )(x_ctx, x_lat, mod_l.reshape(8, 1, 6 * D_MODEL), a_ctx, a_lat, od[0], od[1], hm[0], hm[1], bgc, bgc,
      jnp.tile(attn_g, HEADS).reshape(1, -1), jnp.tile(delta_g, HEADS).reshape(1, -1),
      jnp.tile(mlstm_g, HEADS).reshape(1, -1), w_out, g2.reshape(1, D_MODEL), w_gu, w_dn,
      final_g.reshape(1, D_MODEL))


def _rope_tables(dec_seq):
    n_rows = dec_seq // GRID_W
    rows = jnp.repeat(jnp.arange(n_rows, dtype=F32), GRID_W)
    cols = jnp.tile(jnp.arange(GRID_W, dtype=F32), n_rows)
    n_freq = A_QK_DIM // 4
    inv_freq = ROPE_BASE ** (-jnp.arange(n_freq, dtype=F32) / n_freq)
    ang = jnp.concatenate([rows[:, None] * inv_freq, cols[:, None] * inv_freq], axis=-1)
    cos = jnp.repeat(jnp.cos(ang), 2, axis=-1)
    sin = jnp.repeat(jnp.sin(ang), 2, axis=-1) * jnp.tile(jnp.array([-1.0, 1.0], F32), A_QK_DIM // 2)
    cos = jnp.concatenate([jnp.ones((DENSE_TILE, A_QK_DIM), F32), cos], axis=0)
    sin = jnp.concatenate([jnp.zeros((DENSE_TILE, A_QK_DIM), F32), sin], axis=0)
    return jnp.tile(cos, (1, 2)), jnp.tile(sin, (1, 2))


def _permute_proj(w_in_l, b_in_l):
    def perm(a):
        head, ba_bb, tail, ci_cf = a[..., 0:2560], a[..., 2560:2576], a[..., 2576:3600], a[..., 3600:3616]
        pad = jnp.zeros(a.shape[:-1] + (N_PROJ - 3616,), a.dtype)
        return jnp.concatenate([head, tail, ba_bb, ci_cf, pad], axis=-1)
    return perm(w_in_l).astype(BF16), perm(b_in_l)


def _gate_layouts(grow80, n_chunks):
    r = grow80.reshape(10, N_DIR, HEADS, n_chunks, CHUNK)
    tok = r[jnp.array([0, 2, 1, 4, 6])].transpose(1, 3, 4, 0, 2).reshape(N_DIR, n_chunks * CHUNK, 5 * HEADS)
    gtok = jnp.pad(tok, ((0, 0), (0, 0), (0, LANES - 5 * HEADS)))
    rows = r[jnp.array([0, 5, 8, 9])].transpose(1, 3, 0, 2, 4).reshape(N_DIR, n_chunks, 4, HW)
    grow = jnp.pad(rows, ((0, 0), (0, 0), (0, 4), (0, 0)))
    per_head = lambda q: r[q, :, :, :, 0].transpose(0, 2, 1)[..., None]
    zeros = lambda w: jnp.zeros((N_DIR, n_chunks, HEADS, w), F32)
    delta_rows = jnp.concatenate([zeros(CHUNK), per_head(3), zeros(LANES - CHUNK - 1)], axis=-1)
    mlstm_rows = jnp.concatenate([r[7].transpose(0, 2, 1, 3), per_head(8), per_head(9),
                                  zeros(LANES - CHUNK - 2)], axis=-1)
    return gtok, grow, jnp.concatenate([delta_rows, mlstm_rows], axis=2)


def _block_diag(s):
    eye = jnp.eye(HEADS, dtype=s.dtype)
    out = s[..., :, :, None, :] * eye[:, None, :, None]
    return out.reshape(s.shape[:-3] + (HW, HW))


def _block_diag_inv(s_bd):
    s6 = s_bd.reshape(s_bd.shape[:-2] + (HEADS, HEAD_DIM, HEADS, HEAD_DIM))
    return jnp.stack([s6[..., h, :, h, :] for h in range(HEADS)], axis=-3)


def _norm_block(n):
    return _block_diag(jnp.broadcast_to(n[..., None], n.shape + (HEAD_DIM,)))


def _with_zero_ctx(batch, lat_state):
    return jnp.concatenate([jnp.zeros((batch,) + lat_state.shape[1:], lat_state.dtype), lat_state], axis=0)


def kernel(x_prompt, x_sample, cache_attn_k, cache_attn_v, state_delta, state_mlstm_C, state_mlstm_n,
           state_mlstm_m, c, c_ctx, norm1_g, norm2_g, w_mod, b_mod, w_in, b_in, w_out, lambda_qk,
           attn_subln_g, delta_conv_w, delta_A_log, delta_dt_bias, delta_norm_g, mlstm_f_bias,
           mlstm_norm_g, w_gate_up, w_down, final_norm_g):
    batch, seq, _ = x_prompt.shape
    dec_batch, dec_seq, _ = x_sample.shape
    past_len = cache_attn_k.shape[2]
    assert seq == ROW_TILE and dec_seq % ROW_TILE == 0 and dec_batch + 1 <= 8
    t_ctx = batch * seq
    t_lat = dec_batch * dec_seq
    t_rows = t_ctx + t_lat
    n_ctx_tiles = t_ctx // ROW_TILE
    tiles_per_seq = dec_seq // ROW_TILE
    n_chunks = t_rows // CHUNK

    x_src = (x_prompt.reshape(t_ctx, D_MODEL), x_sample.reshape(t_lat, D_MODEL), 0, t_rows)
    cvecs = jnp.zeros((8, D_MODEL), F32).at[0].set(c_ctx).at[1:1 + dec_batch].set(c)
    mod = _ada_mod(cvecs, w_mod, b_mod)
    cos_t, sin_t = _rope_tables(dec_seq)

    ks_l, vs_l, sd_l, cm_l, nm_l, mm_l = [], [], [], [], [], []
    for l in range(DEPTH):
        lam_init = 0.8 - 0.6 * math.exp(-0.3 * l)
        w_p, b_p = _permute_proj(w_in[l], b_in[l])
        qk, vt, kv32, bqkv, bgc, gates = _in_proj(*x_src, mod[l], norm1_g[l], w_p, b_p, cos_t, sin_t,
                                                 n_ctx_tiles, tiles_per_seq)
        ks_l.append(kv32[:t_ctx, 0:512].reshape(batch, seq, 512))
        vs_l.append(kv32[:t_ctx, 512:1024].reshape(batch, seq, 512))

        par = jnp.zeros((32, LANES), F32)
        par = par.at[0:8, 0].set(delta_A_log[l].reshape(-1)).at[0:8, 1].set(delta_dt_bias[l].reshape(-1))
        par = par.at[24:32, 1].set(mlstm_f_bias[l].reshape(-1))
        gtok, grow, ghead = _gate_layouts(_gate_prep(gates[:, 0:32].T, par), n_chunks)

        a_ctx = _diff_attention(lambda_qk[l], qk, vt, None, None, lam_init, 0, batch, seq, seq, seq)
        a_lat = _diff_attention(lambda_qk[l], qk, vt, cache_attn_k[:, l].reshape(dec_batch, past_len, 512),
                                cache_attn_v[:, l].reshape(dec_batch, past_len, 512), lam_init, t_ctx,
                                dec_batch, dec_seq, min(1024, dec_seq), 512)

        dqkv = _delta_conv(bqkv, delta_conv_w[l], n_ctx_tiles, tiles_per_seq)
        od_f, od_b, s_fin = _delta_scan(dqkv, gtok, grow, ghead,
                                        _with_zero_ctx(batch, _block_diag(state_delta[:, l])),
                                        n_ctx_tiles, tiles_per_seq)
        sd_l.append(_block_diag_inv(s_fin[:batch]))

        cn_lat = jnp.concatenate([_block_diag(state_mlstm_C[:, l]), _norm_block(state_mlstm_n[:, l])], axis=-1)
        m_lat = jnp.repeat(state_mlstm_m[:, l], HEAD_DIM, axis=-1)
        hm_f, hm_b, cn_fin, m_fin = _mlstm_scan(
            bgc, gtok, grow, ghead, _with_zero_ctx(batch, cn_lat), _with_zero_ctx(batch, m_lat[..., None]),
            _with_zero_ctx(batch, m_lat[:, :, None, :]), n_ctx_tiles, tiles_per_seq)
        cm_l.append(_block_diag_inv(cn_fin[:batch, :, :, 0:HW]))
        nm_l.append(_block_diag_inv(cn_fin[:batch, :, :, HW:2 * HW])[..., 0])
        mm_l.append(m_fin[:batch, :, ::HEAD_DIM, 0])

        outs = _merge_ffn(x_src, mod[l], a_ctx, a_lat, (od_f, od_b), (hm_f, hm_b), bgc, attn_subln_g[l],
                          delta_norm_g[l], mlstm_norm_g[l], w_out[l].astype(BF16), lam_init, norm2_g[l],
                          w_gate_up[l].astype(BF16), w_down[l].astype(BF16), final_norm_g, l == DEPTH - 1,
                          n_ctx_tiles, tiles_per_seq)
        x_src = (outs[0], outs[0], t_ctx // DENSE_TILE, t_rows)

    y_prompt = outs[0].reshape(batch, seq, D_MODEL)
    y_sample = outs[1].reshape(dec_batch, dec_seq, D_MODEL)
    new_k = jnp.stack(ks_l, axis=1).reshape(batch, DEPTH, seq, HEADS, 2, A_QK_DIM)
    new_v = jnp.stack(vs_l, axis=1).reshape(batch, DEPTH, seq, HEADS, A_V_DIM)
    return (y_prompt, y_sample, new_k, new_v, jnp.stack(sd_l, axis=1),
            jnp.stack(cm_l, axis=1), jnp.stack(nm_l, axis=1), jnp.stack(mm_l, axis=1))
```

```python
import functools
import math

import jax
import jax.numpy as jnp
from jax import lax
from jax.experimental import pallas as pl
from jax.experimental.pallas import tpu as pltpu

F32 = jnp.float32
BF16 = jnp.bfloat16

D_MODEL = 1024
DEPTH = 2
GRID_W = 64
N_DIR = 2
CHUNK = 64
ROPE_BASE = 10000.0
EPS = 1e-6
HEADS = 4
A_QK_DIM = 64
A_V_DIM = 128
HEAD_DIM = 64
CONV_K = 5
FFN_HIDDEN = 2816
ROW_TILE = 256
CHUNKS_PER_TILE = ROW_TILE // CHUNK
HW = HEADS * HEAD_DIM
LANES = 128
HALO = 8
BF16_ROWS = 16
GATE_GROUP = 32

N_PROJ = 3712
VMEM_LIMIT = 56 * 1024 * 1024


DENSE_TILE = 512


def _dense_tiles(n_ctx_tiles, tiles_per_seq):
    ratio = DENSE_TILE // ROW_TILE
    assert n_ctx_tiles % ratio == 0 and tiles_per_seq % ratio == 0
    return n_ctx_tiles // ratio, tiles_per_seq // ratio


def _cparams(sem):
    return pltpu.CompilerParams(dimension_semantics=sem, vmem_limit_bytes=VMEM_LIMIT)


def _bdot(a, b):
    return jnp.dot(a.astype(BF16), b.astype(BF16), preferred_element_type=F32)


def _bdot_nt(a, b):
    return lax.dot_general(a.astype(BF16), b.astype(BF16), (((1,), (1,)), ((), ())),
                           preferred_element_type=F32)


def _split3(x):
    hi = x.astype(BF16).astype(F32)
    r1 = x - hi
    mid = r1.astype(BF16).astype(F32)
    lo = (r1 - mid).astype(BF16).astype(F32)
    return hi, mid, lo


def _dot01(x, m01):
    d = functools.partial(jnp.dot, preferred_element_type=F32)
    hi, mid, lo = _split3(x)
    return d(hi.astype(BF16), m01) + d(mid.astype(BF16), m01) + d(lo.astype(BF16), m01)


def _group_sumsq(x, seg):
    sq = x * x
    out = []
    for t in range(x.shape[1] // LANES):
        tile = sq[:, t * LANES:(t + 1) * LANES]
        if seg == LANES:
            out.append(jnp.broadcast_to(jnp.sum(tile, axis=-1, keepdims=True), tile.shape))
        else:
            low = lax.broadcasted_iota(jnp.int32, tile.shape, 1) < seg
            s_lo = jnp.sum(jnp.where(low, tile, 0.0), axis=-1, keepdims=True)
            s_hi = jnp.sum(jnp.where(low, 0.0, tile), axis=-1, keepdims=True)
            out.append(jnp.where(low, s_lo, s_hi))
    return jnp.concatenate(out, axis=1)


def _sigmoid(x):
    return 1.0 / (1.0 + jnp.exp(-x))


def _softplus(x):
    return jnp.maximum(x, 0.0) + jnp.log1p(jnp.exp(-jnp.abs(x)))


def _ada_kernel(c_ref, w_ref, b_ref, o_ref):
    c = c_ref[...]
    s = c * _sigmoid(c)
    o_ref[...] = _bdot(s, w_ref[...]) + b_ref[...]


def _ada_mod(cvecs, w_mod, b_mod):
    n_out = w_mod.shape[-1]
    tn = 1024
    return pl.pallas_call(
        _ada_kernel,
        grid=(DEPTH, n_out // tn),
        in_specs=[pl.BlockSpec((8, D_MODEL), lambda l, j: (0, 0)),
                  pl.BlockSpec((None, D_MODEL, tn), lambda l, j: (l, 0, j)),
                  pl.BlockSpec((None, 1, tn), lambda l, j: (l, 0, j))],
        out_specs=pl.BlockSpec((None, 8, tn), lambda l, j: (l, 0, j)),
        out_shape=jax.ShapeDtypeStruct((DEPTH, 8, n_out), F32),
        compiler_params=_cparams(("parallel", "parallel")),
        name="ada_mod",
    )(cvecs, w_mod, b_mod.reshape(DEPTH, 1, n_out))


def _proj_kernel(xc_ref, xl_ref, mod_ref, g_ref, w_ref, b_ref, cos_ref, sin_ref,
                 qk_ref, vt_ref, kv32_ref, bqkv_ref, bgc_ref, gates_ref, *, n_ctx_tiles):
    x = jnp.where(pl.program_id(0) < n_ctx_tiles, xc_ref[...], xl_ref[...])
    y = x * lax.rsqrt(jnp.mean(x * x, axis=-1, keepdims=True) + EPS) * g_ref[...]
    u = y * (1.0 + mod_ref[:, D_MODEL:2 * D_MODEL]) + mod_ref[:, 0:D_MODEL]
    acc = _bdot(u, w_ref[...]) + b_ref[...]
    kv32_ref[...] = acc[:, 512:1536]
    vt_ref[...] = acc[:, 1024:1536].T.astype(BF16)
    bqkv_ref[...] = acc[:, 1536:2304]
    bgc_ref[...] = acc[:, 2304:3584]
    gates_ref[...] = acc[:, 3584:3712]
    cos = cos_ref[...]
    sin = sin_ref[...]
    even = (lax.broadcasted_iota(jnp.int32, cos.shape, 1) % 2) == 0
    for j in range(8):
        xj = acc[:, j * LANES:(j + 1) * LANES]
        swapped = jnp.where(even, pltpu.roll(xj, LANES - 1, 1), pltpu.roll(xj, 1, 1))
        r = xj * cos + swapped * sin
        if j < 4:
            r = r * (A_QK_DIM ** -0.5)
        qk_ref[:, j * LANES:(j + 1) * LANES] = r.astype(BF16)


def _x_specs(n_ctx_tiles, lat_tile0):
    return [pl.BlockSpec((DENSE_TILE, D_MODEL), lambda i: (jnp.minimum(i, n_ctx_tiles - 1), 0)),
            pl.BlockSpec((DENSE_TILE, D_MODEL), lambda i: (jnp.maximum(i - n_ctx_tiles, 0) + lat_tile0, 0))]


def _in_proj(x_ctx, x_lat, lat_tile0, t_rows, mod_l, g1, w_p, b_p, cos_t, sin_t, n_ctx_tiles, tiles_per_seq):
    n_tiles = t_rows // DENSE_TILE
    n_ctx_tiles, tiles_per_seq = _dense_tiles(n_ctx_tiles, tiles_per_seq)

    def mod_idx(i):
        return (jnp.where(i < n_ctx_tiles, 0, 1 + (i - n_ctx_tiles) // tiles_per_seq), 0, 0)

    def rope_idx(i):
        return (jnp.where(i < n_ctx_tiles, 0, 1 + (i - n_ctx_tiles) % tiles_per_seq), 0)

    row = lambda w: pl.BlockSpec((DENSE_TILE, w), lambda i: (i, 0))
    outs = [(1024, BF16, False), (512, BF16, True), (1024, F32, False), (768, F32, False), (1280, F32, False),
            (LANES, F32, False)]
    return pl.pallas_call(
        functools.partial(_proj_kernel, n_ctx_tiles=n_ctx_tiles),
        grid=(n_tiles,),
        in_specs=_x_specs(n_ctx_tiles, lat_tile0) + [
                  pl.BlockSpec((None, 1, 6 * D_MODEL), mod_idx),
                  pl.BlockSpec((1, D_MODEL), lambda i: (0, 0)),
                  pl.BlockSpec((D_MODEL, N_PROJ), lambda i: (0, 0)),
                  pl.BlockSpec((1, N_PROJ), lambda i: (0, 0)),
                  pl.BlockSpec((DENSE_TILE, LANES), rope_idx),
                  pl.BlockSpec((DENSE_TILE, LANES), rope_idx)],
        out_specs=[pl.BlockSpec((w, DENSE_TILE), lambda i: (0, i)) if tr else row(w) for w, _, tr in outs],
        out_shape=[jax.ShapeDtypeStruct((w, t_rows) if tr else (t_rows, w), dt) for w, dt, tr in outs],
        compiler_params=_cparams(("parallel",)),
        name="in_proj",
    )(x_ctx, x_lat, mod_l.reshape(8, 1, 6 * D_MODEL), g1.reshape(1, D_MODEL), w_p, b_p.reshape(1, N_PROJ),
      cos_t, sin_t)


GATE_ROWS = 80


def _gates_kernel(g_ref, par_ref, o_ref):
    x = g_ref[...]
    alog = par_ref[0:8, 0:1]
    dtb = par_ref[0:8, 1:2]
    fb = par_ref[24:32, 1:2]
    g = -jnp.exp(alog) * _softplus(x[0:8] + dtb)
    beta = _sigmoid(x[8:16])
    ig = x[16:24]
    lf = -_softplus(-(x[24:32] + fb))
    tl = x.shape[1]
    r = lax.broadcasted_iota(jnp.int32, (LANES, LANES), 0)
    c = lax.broadcasted_iota(jnp.int32, (LANES, LANES), 1)
    same = (r // CHUNK) == (c // CHUNK)
    pre = jnp.where(same & (r <= c), 1.0, 0.0).astype(BF16)
    suf = jnp.where(same & (r >= c), 1.0, 0.0).astype(BF16)
    tot = jnp.where(same, 1.0, 0.0).astype(BF16)
    m01 = jnp.concatenate([pre, suf, tot], axis=1)
    fwd16 = (lax.broadcasted_iota(jnp.int32, (16, LANES), 0) % 8) < 4
    fwd8 = lax.broadcasted_iota(jnp.int32, (8, LANES), 0) < 4
    pos = lax.broadcasted_iota(jnp.int32, (8, LANES), 1) % CHUNK

    def running_max(v):
        pf = sf = v
        s = 1
        while s < CHUNK:
            pf = jnp.maximum(pf, jnp.where(pos >= s, pltpu.roll(pf, s, 1), -jnp.inf))
            sf = jnp.maximum(sf, jnp.where(pos < CHUNK - s, pltpu.roll(sf, LANES - s, 1), -jnp.inf))
            s *= 2
        return pf, sf

    o_ref[16:24, :] = beta
    for j in range(tl // LANES):
        sl = slice(j * LANES, (j + 1) * LANES)
        xs = jnp.concatenate([g[:, sl], lf[:, sl]], axis=0)
        cs = _dot01(xs, m01)
        cum = jnp.where(fwd16, cs[:, 0:LANES], cs[:, LANES:2 * LANES])
        total = cs[:, 2 * LANES:3 * LANES]
        gc, gtot, b, btot = cum[0:8], total[0:8], cum[8:16], total[8:16]
        rr = ig[:, sl] - b
        kvl = btot - b + ig[:, sl]
        r_pf, r_sf = running_max(rr)
        k_pf, k_sf = running_max(kvl)
        o_ref[0:8, sl] = gc
        o_ref[8:16, sl] = gtot - gc
        o_ref[24:32, sl] = gtot
        o_ref[32:40, sl] = b
        o_ref[40:48, sl] = rr
        o_ref[48:56, sl] = jnp.where(fwd8, r_pf, r_sf)
        o_ref[56:64, sl] = kvl
        o_ref[64:72, sl] = btot
        o_ref[72:80, sl] = jnp.maximum(k_pf, k_sf)


def _gate_prep(gates_t, par):
    t_rows = gates_t.shape[1]
    tl = math.gcd(t_rows, 2048)
    return pl.pallas_call(
        _gates_kernel,
        grid=(t_rows // tl,),
        in_specs=[pl.BlockSpec((32, tl), lambda i: (0, i)),
                  pl.BlockSpec((32, LANES), lambda i: (0, 0))],
        out_specs=pl.BlockSpec((GATE_ROWS, tl), lambda i: (0, i)),
        out_shape=jax.ShapeDtypeStruct((GATE_ROWS, t_rows), F32),
        compiler_params=_cparams(("parallel",)),
        name="gate_prep",
    )(gates_t, par)


def _conv_kernel(x_ref, p_ref, n_ref, w_ref, o_ref, ext_s, *, n_ctx_tiles, tiles_per_seq):
    i = pl.program_id(0)
    j = (i - n_ctx_tiles) % tiles_per_seq
    is_ctx = i < n_ctx_tiles
    first = jnp.logical_or(is_ctx, j == 0)
    last = jnp.logical_or(is_ctx, j == tiles_per_seq - 1)
    ext_s[0:HALO, :] = jnp.where(first, 0.0, p_ref[...])
    ext_s[HALO:HALO + ROW_TILE, :] = x_ref[...]
    ext_s[HALO + ROW_TILE:, :] = jnp.where(last, 0.0, n_ref[...])
    w = w_ref[...]
    y = None
    for k in range(CONV_K):
        off = HALO - CONV_K // 2 + k
        term = ext_s[off:off + ROW_TILE, :] * w[k:k + 1, :]
        y = term if y is None else y + term
    y = y * _sigmoid(y)
    q = y[:, 0:HW]
    k_ = y[:, HW:2 * HW]
    o_ref[:, 0:HW] = q * lax.rsqrt(_group_sumsq(q, HEAD_DIM) + EPS) * (HEAD_DIM ** -0.5)
    o_ref[:, HW:2 * HW] = k_ * lax.rsqrt(_group_sumsq(k_, HEAD_DIM) + EPS)
    o_ref[:, 2 * HW:3 * HW] = y[:, 2 * HW:3 * HW]


def _delta_conv(bqkv, conv_w, n_ctx_tiles, tiles_per_seq):
    t_rows, width = bqkv.shape
    n_tiles = t_rows // ROW_TILE
    hb = ROW_TILE // HALO
    n_hb = t_rows // HALO
    return pl.pallas_call(
        functools.partial(_conv_kernel, n_ctx_tiles=n_ctx_tiles, tiles_per_seq=tiles_per_seq),
        grid=(n_tiles,),
        in_specs=[pl.BlockSpec((ROW_TILE, width), lambda i: (i, 0)),
                  pl.BlockSpec((HALO, width), lambda i: (jnp.maximum(i * hb - 1, 0), 0)),
                  pl.BlockSpec((HALO, width), lambda i: (jnp.minimum((i + 1) * hb, n_hb - 1), 0)),
                  pl.BlockSpec((CONV_K, width), lambda i: (0, 0))],
        out_specs=pl.BlockSpec((ROW_TILE, width), lambda i: (i, 0)),
        out_shape=jax.ShapeDtypeStruct((t_rows, width), F32),
        scratch_shapes=[pltpu.VMEM((ROW_TILE + 2 * HALO, width), F32)],
        compiler_params=_cparams(("parallel",)),
        name="delta_conv",
    )(bqkv, bqkv, bqkv, conv_w)


def _attn_kernel(lqk_ref, q_ref, *refs, lam_init, tq, tk):
    o_ref = refs[-1]
    lq = lqk_ref[...]
    lam = (jnp.exp(jnp.sum(lq[0:1] * lq[1:2], axis=-1, keepdims=True))
           - jnp.exp(jnp.sum(lq[2:3] * lq[3:4], axis=-1, keepdims=True)) + lam_init)
    tiles = [(k_ref, vt_ref, j) for k_ref, vt_ref in zip(refs[0:-1:2], refs[1:-1:2])
             for j in range(k_ref.shape[0] // tk)]
    n_kv = len(tiles)
    ones = jnp.ones((BF16_ROWS, tk), BF16)
    for hh in range(q_ref.shape[1] // LANES):
        hs = slice(hh * LANES, (hh + 1) * LANES)
        q = q_ref[:, hs]
        lane = lax.broadcasted_iota(jnp.int32, q.shape, 1)
        zero = jnp.zeros_like(q)
        q2 = jnp.concatenate([jnp.where(lane < A_QK_DIM, q, zero), jnp.where(lane >= A_QK_DIM, q, zero)],
                             axis=0)

        def scores(t):
            k_ref, _, j = tiles[t]
            return lax.dot_general(k_ref[j * tk:(j + 1) * tk, hs], q2, (((1,), (1,)), ((), ())),
                                   preferred_element_type=F32)

        m = acc = None
        st_next = scores(0)
        for j in range(n_kv):
            st = st_next
            if j + 1 < n_kv:
                st_next = scores(j + 1)
            m_new = jnp.max(st, axis=0, keepdims=True)
            if j > 0:
                m_new = jnp.maximum(m, m_new)
            p = jnp.exp(st - m_new)
            _, vt_ref, jt = tiles[j]
            vt = jnp.concatenate([vt_ref[hs, jt * tk:(jt + 1) * tk], ones], axis=0)
            pv = jnp.dot(vt, p.astype(BF16), preferred_element_type=F32)
            acc = pv if j == 0 else jnp.exp(m - m_new) * acc + pv
            m = m_new
        o = acc[0:A_V_DIM] / acc[A_V_DIM:A_V_DIM + 1]
        o_ref[:, hs] = (o[:, 0:tq] - lam * o[:, tq:2 * tq]).T


def _diff_attention(lqk, qk, vt, cache_k, cache_v, lam_init, q_row0, n_seq, lq, tq, tk):
    assert q_row0 % lq == 0
    qb0 = q_row0 // tq
    sb0 = q_row0 // lq
    nq = lq // tq
    hb = HEADS if lq <= ROW_TILE else 1
    n_hblk = HEADS // hb
    in_specs = [pl.BlockSpec((4, A_QK_DIM), lambda b, h, i: (0, 0)),
                pl.BlockSpec((tq, hb * LANES), lambda b, h, i: (qb0 + b * nq + i, h)),
                pl.BlockSpec((lq, hb * LANES), lambda b, h, i: (sb0 + b, n_hblk + h)),
                pl.BlockSpec((hb * A_V_DIM, lq), lambda b, h, i: (h, sb0 + b))]
    args = [lqk, qk, qk, vt]
    if cache_k is not None:
        past = cache_k.shape[1]
        in_specs += [pl.BlockSpec((None, past, hb * LANES), lambda b, h, i: (b, 0, h)),
                     pl.BlockSpec((None, hb * A_V_DIM, past), lambda b, h, i: (b, h, 0))]
        args += [cache_k.astype(BF16), cache_v.astype(BF16).transpose(0, 2, 1)]
    return pl.pallas_call(
        functools.partial(_attn_kernel, lam_init=lam_init, tq=tq, tk=tk),
        grid=(n_seq, n_hblk, nq),
        in_specs=in_specs,
        out_specs=pl.BlockSpec((tq, hb * LANES), lambda b, h, i: (b * nq + i, h)),
        out_shape=jax.ShapeDtypeStruct((n_seq * lq, HEADS * A_V_DIM), F32),
        compiler_params=_cparams(("parallel", "parallel", "parallel")),
        name="diff_attention",
    )(*args)


def _same_head(n_cols=HW):
    r = lax.broadcasted_iota(jnp.int32, (HW, n_cols), 0) // HEAD_DIM
    c = (lax.broadcasted_iota(jnp.int32, (HW, n_cols), 1) % HW) // HEAD_DIM
    return r == c


def _bd(x, same):
    return jnp.where(same, jnp.concatenate([x, x, x, x], axis=0), 0.0)


def _cat_index():
    i = lax.broadcasted_iota(jnp.int32, (CHUNK, HW), 0)
    j = lax.broadcasted_iota(jnp.int32, (CHUNK, HW), 1) % CHUNK
    return i, j


def _scan_masks(direction):
    i, j = _cat_index()
    diff = (i - j) * (1 - 2 * direction)
    return diff >= 0, diff > 0


def _expand_matrix(first_lane, n_q):
    r = lax.broadcasted_iota(jnp.int32, (LANES, n_q * HW), 0)
    c = lax.broadcasted_iota(jnp.int32, (LANES, n_q * HW), 1)
    s = r % GATE_GROUP - first_lane
    hit = (r < 3 * GATE_GROUP) & (s >= 0) & (s < 4 * n_q) & (s // HEADS == c // HW) \
        & (s % HEADS == (c % HW) // HEAD_DIM)
    return jnp.where(hit, 1.0, 0.0).astype(BF16)


def _expand(tok, e3):
    hi, mid, lo = _split3(tok)
    x3 = hi + pltpu.roll(mid, GATE_GROUP, 1) + pltpu.roll(lo, 2 * GATE_GROUP, 1)
    return jnp.dot(x3.astype(BF16), e3, preferred_element_type=F32)


def _head_rows(hd, lane):
    return jnp.concatenate([jnp.broadcast_to(hd[h:h + 1, lane:lane + 1], (HEAD_DIM, 1)) for h in range(HEADS)],
                           axis=0)


def _pad_rows(x):
    return jnp.concatenate([x, jnp.zeros_like(x)], axis=0)


def _unit_tri_inverse(mats, same):
    i, j = _cat_index()
    b16 = (i // 16) == (j // 16)
    b32 = (i // 32) == (j // 32)
    eye = jnp.where(i == j, 1.0, 0.0)
    xs = [jnp.where(b16, a, 0.0) for a in mats]
    ts = [eye - x for x in xs]
    xbd = [_bd(x, same) for x in xs]
    for _ in range(3):
        xs = [_bdot(x, b) for x, b in zip(xs, xbd)]
        xbd = [_bd(x, same) for x in xs]
        ts = [t + _bdot(t, b) for t, b in zip(ts, xbd)]
    for inside in (jnp.logical_and(b32, jnp.logical_not(b16)), jnp.logical_not(b32)):
        mids = [_bdot(t, _bd(jnp.where(inside, a, 0.0), same)) for t, a in zip(ts, mats)]
        ts = [t - _bdot(m, _bd(t, same)) for t, m in zip(ts, mids)]
    return ts


def _scan_job(n_ctx_tiles, tiles_per_seq):
    j = pl.program_id(0)
    t = (j - n_ctx_tiles) % tiles_per_seq
    is_ctx = j < n_ctx_tiles
    return jnp.logical_or(is_ctx, t == 0), jnp.logical_or(is_ctx, t == tiles_per_seq - 1)


def _chunks_of(refs_f, refs_b):
    out = []
    for d, (q, k, v, tok, rows, heads) in enumerate((refs_f, refs_b)):
        for c in range(CHUNKS_PER_TILE):
            sl = slice(c * CHUNK, (c + 1) * CHUNK)
            out.append((q[sl, :], k[sl, :], v[sl, :], tok[sl, :], rows[c], heads[c], d))
    return out


def _delta_prep(chunks):
    same = _same_head()
    masks = [_scan_masks(0), _scan_masks(1)]
    e3 = _expand_matrix(0, 3)
    n = len(chunks)
    ex = [_expand(ch[3], e3) for ch in chunks]
    gc = [e[:, 0:HW] for e in ex]
    kb = [chunks[i][1] * ex[i][:, HW:2 * HW] for i in range(n)]
    kk = [_bdot_nt(jnp.concatenate([kb[i], chunks[i][0]], axis=0), _bd(chunks[i][1], same)) for i in range(n)]
    e_incl = [jnp.exp(jnp.where(masks[chunks[i][6]][0], gc[i] - chunks[i][4][0:1, :], -jnp.inf)) for i in range(n)]
    a = [jnp.where(masks[chunks[i][6]][1], kk[i][0:CHUNK] * e_incl[i], 0.0) for i in range(n)]
    aqk = [kk[i][CHUNK:2 * CHUNK] * e_incl[i] for i in range(n)]
    tinv = _unit_tri_inverse(a, same)
    egc = [jnp.exp(g) for g in gc]
    rhs = [jnp.concatenate([_bd(chunks[i][2] * ex[i][:, HW:2 * HW], same), _bd(kb[i] * egc[i], same)], axis=1)
           for i in range(n)]
    uw = [_bdot(tinv[i], rhs[i]) for i in range(n)]
    kd_t = [_pad_rows(chunks[i][1] * jnp.exp(ex[i][:, 2 * HW:3 * HW])).T for i in range(n)]
    gl = [jnp.exp(_head_rows(chunks[i][5][0:HEADS], CHUNK)) for i in range(n)]
    return [(uw[i][:, 0:HW], uw[i][:, HW:2 * HW], chunks[i][0] * egc[i], kd_t[i], aqk[i], gl[i])
            for i in range(n)]


def _delta_step(pre, s_ref, o_ref, c, same):
    u, w, qd, kd_t, aqk, gl = pre
    s = s_ref[...]
    wq = _bdot(jnp.concatenate([w, qd], axis=0), s)
    v_new = u - wq[0:CHUNK]
    o_ref[c * CHUNK:(c + 1) * CHUNK, :] = wq[CHUNK:2 * CHUNK] + _bdot(aqk, _bd(v_new, same))
    s_ref[...] = s * gl + jnp.where(same, _bdot(kd_t, _pad_rows(v_new)), 0.0)


def _delta_kernel(qf, kf, vf, qb, kb, vb, tokf, rowf, headf, tokb, rowb, headb, s0_ref,
                  of_ref, ob_ref, sout_ref, sf_s, sb_s, *, n_ctx_tiles, tiles_per_seq):
    first, last = _scan_job(n_ctx_tiles, tiles_per_seq)

    @pl.when(first)
    def _():
        sf_s[...] = s0_ref[0]
        sb_s[...] = s0_ref[1]

    pre = _delta_prep(_chunks_of((qf, kf, vf, tokf, rowf, headf), (qb, kb, vb, tokb, rowb, headb)))
    same = _same_head()
    for i in range(CHUNKS_PER_TILE):
        cb = CHUNKS_PER_TILE - 1 - i
        _delta_step(pre[i], sf_s, of_ref, i, same)
        _delta_step(pre[CHUNKS_PER_TILE + cb], sb_s, ob_ref, cb, same)

    @pl.when(last)
    def _():
        sout_ref[0] = sf_s[...]
        sout_ref[1] = sb_s[...]


def _mlstm_prep(chunks):
    same = _same_head()
    masks = [_scan_masks(0), _scan_masks(1)]
    e3 = _expand_matrix(12, 2)
    n = len(chunks)
    ex = [_expand(ch[3], e3) for ch in chunks]
    b = [e[:, 0:HW] for e in ex]
    ks = [ch[1] * (HEAD_DIM ** -0.5) for ch in chunks]
    qk = [_bdot_nt(chunks[i][0], _bd(ks[i], same)) for i in range(n)]
    dmat = [jnp.where(masks[chunks[i][6]][0], b[i] + chunks[i][4][1:2, :], -jnp.inf) for i in range(n)]
    dmax = [ex[i][:, 0:HW] + ex[i][:, HW:2 * HW] for i in range(n)]
    k_t = [_pad_rows(k).T for k in ks]
    ones = jnp.ones((CHUNK, HW), F32)
    vaug = [_pad_rows(jnp.concatenate([ch[2], ones], axis=1)) for ch in chunks]
    vbd = [jnp.concatenate([_bd(ch[2], same), jnp.where(same, 1.0, 0.0)], axis=1) for ch in chunks]
    kvl = [jnp.concatenate([jnp.broadcast_to(ch[5][HEADS + h:HEADS + h + 1, :], (HEAD_DIM, LANES))
                            for h in range(HEADS)], axis=0) for ch in chunks]
    btot_col = [_head_rows(ch[5][HEADS:2 * HEADS], CHUNK) for ch in chunks]
    mkv_col = [_head_rows(ch[5][HEADS:2 * HEADS], CHUNK + 1) for ch in chunks]
    return [(chunks[i][0], k_t[i], vaug[i], vbd[i], b[i], dmat[i], dmax[i], qk[i], kvl[i], btot_col[i],
             mkv_col[i], chunks[i][4][2:3, :], chunks[i][4][3:4, :]) for i in range(n)]


def _mlstm_step(pre, cn_ref, mcol_ref, mrow_ref, o_ref, c, same2):
    q, k_t, vaug, vbd, b, dmat, dmax, qk, kvl, btot_col, mkv_col, btot_row, mkv_row = pre
    cn = cn_ref[...]
    ms_col = mcol_ref[...]
    ms_row = mrow_ref[...]
    inter = b + ms_row
    m_t = jnp.maximum(inter, dmax)
    s = qk * jnp.exp(dmat - m_t)
    w_inter = jnp.exp(inter - m_t)
    nd = jnp.concatenate([w_inter, w_inter], axis=1) * _bdot(q, cn) + _bdot(s, vbd)
    o_ref[c * CHUNK:(c + 1) * CHUNK, :] = nd[:, 0:HW] / jnp.maximum(jnp.abs(nd[:, HW:2 * HW]), jnp.exp(-m_t))
    m_new = jnp.maximum(btot_col + ms_col, mkv_col)
    lane = lax.broadcasted_iota(jnp.int32, kvl.shape, 1)
    wk = jnp.where(lane < CHUNK, jnp.exp(kvl - m_new), 0.0)
    cn_ref[...] = jnp.exp(btot_col + ms_col - m_new) * cn + jnp.where(same2, _bdot(k_t * wk, vaug), 0.0)
    mcol_ref[...] = m_new
    mrow_ref[...] = jnp.maximum(btot_row + ms_row, mkv_row)


def _mlstm_kernel(qf, kf, vf, qb, kb, vb, tokf, rowf, headf, tokb, rowb, headb, cn0_ref, mc0_ref, mr0_ref,
                  of_ref, ob_ref, cnout_ref, mout_ref, cnf_s, cnb_s, mcf_s, mcb_s, mrf_s, mrb_s,
                  *, n_ctx_tiles, tiles_per_seq):
    first, last = _scan_job(n_ctx_tiles, tiles_per_seq)

    @pl.when(first)
    def _():
        cnf_s[...] = cn0_ref[0]
        cnb_s[...] = cn0_ref[1]
        mcf_s[...] = mc0_ref[0]
        mcb_s[...] = mc0_ref[1]
        mrf_s[...] = mr0_ref[0]
        mrb_s[...] = mr0_ref[1]

    pre = _mlstm_prep(_chunks_of((qf, kf, vf, tokf, rowf, headf), (qb, kb, vb, tokb, rowb, headb)))
    same2 = _same_head(2 * HW)
    for i in range(CHUNKS_PER_TILE):
        cb = CHUNKS_PER_TILE - 1 - i
        _mlstm_step(pre[i], cnf_s, mcf_s, mrf_s, of_ref, i, same2)
        _mlstm_step(pre[CHUNKS_PER_TILE + cb], cnb_s, mcb_s, mrb_s, ob_ref, cb, same2)

    @pl.when(last)
    def _():
        cnout_ref[0] = cnf_s[...]
        cnout_ref[1] = cnb_s[...]
        mout_ref[0] = jnp.broadcast_to(mcf_s[...], (HW, LANES))
        mout_ref[1] = jnp.broadcast_to(mcb_s[...], (HW, LANES))


def _scan_call(kernel_fn, name, src, col_blocks, gtok, grow, ghead, states, state_out_shapes, scratch,
               n_ctx_tiles, tiles_per_seq):
    t_rows = src.shape[0]
    n_tiles = t_rows // ROW_TILE

    def seq_of(j):
        return jnp.where(j < n_ctx_tiles, j, n_ctx_tiles + (j - n_ctx_tiles) // tiles_per_seq)

    def back(j):
        jj = j - n_ctx_tiles
        mirrored = n_ctx_tiles + (jj // tiles_per_seq) * tiles_per_seq + tiles_per_seq - 1 - jj % tiles_per_seq
        return jnp.where(j < n_ctx_tiles, j, mirrored)

    def tile_spec(cb, bwd):
        return pl.BlockSpec((ROW_TILE, HW), (lambda j: (back(j), cb)) if bwd else (lambda j: (j, cb)))

    def gate_specs(d):
        tile = (lambda j: back(j)) if d else (lambda j: j)
        return [pl.BlockSpec((None, ROW_TILE, LANES), lambda j: (d, tile(j), 0)),
                pl.BlockSpec((None, CHUNKS_PER_TILE, 8, HW), lambda j: (d, tile(j), 0, 0)),
                pl.BlockSpec((None, CHUNKS_PER_TILE, 8, LANES), lambda j: (d, tile(j), 0, 0))]

    def state_spec(shape):
        return pl.BlockSpec((None,) + tuple(shape), lambda j: (seq_of(j),) + (0,) * len(shape))

    in_specs = ([tile_spec(cb, False) for cb in col_blocks] + [tile_spec(cb, True) for cb in col_blocks]
                + gate_specs(0) + gate_specs(1) + [state_spec(s.shape[1:]) for s in states])
    out_specs = ([pl.BlockSpec((ROW_TILE, HW), lambda j: (j, 0)),
                  pl.BlockSpec((ROW_TILE, HW), lambda j: (back(j), 0))]
                 + [state_spec(s[1:]) for s in state_out_shapes])
    out_shape = ([jax.ShapeDtypeStruct((t_rows, HW), F32)] * 2
                 + [jax.ShapeDtypeStruct(tuple(s), F32) for s in state_out_shapes])
    return pl.pallas_call(
        functools.partial(kernel_fn, n_ctx_tiles=n_ctx_tiles, tiles_per_seq=tiles_per_seq),
        grid=(n_tiles,),
        in_specs=in_specs, out_specs=out_specs, out_shape=out_shape,
        scratch_shapes=scratch,
        compiler_params=_cparams(("arbitrary",)),
        name=name,
    )(*([src] * (2 * len(col_blocks))), gtok, grow, ghead, gtok, grow, ghead, *states)


def _delta_scan(dqkv, gtok, grow, ghead, s0, n_ctx_tiles, tiles_per_seq):
    return _scan_call(_delta_kernel, "delta_scan", dqkv, (0, 1, 2), gtok, grow, ghead, [s0], [s0.shape],
                      [pltpu.VMEM((HW, HW), F32)] * 2, n_ctx_tiles, tiles_per_seq)


def _mlstm_scan(bgc, gtok, grow, ghead, cn0, mc0, mr0, n_ctx_tiles, tiles_per_seq):
    n_seq = cn0.shape[0]
    return _scan_call(_mlstm_kernel, "mlstm_scan", bgc, (1, 2, 3), gtok, grow, ghead, [cn0, mc0, mr0],
                      [cn0.shape, (n_seq, N_DIR, HW, LANES)],
                      [pltpu.VMEM((HW, 2 * HW), F32)] * 2 + [pltpu.VMEM((HW, 1), F32)] * 2
                      + [pltpu.VMEM((1, HW), F32)] * 2, n_ctx_tiles, tiles_per_seq)


def _merge_ffn_kernel(xc_ref, xl_ref, mod_ref, actx_ref, alat_ref, odf_ref, odb_ref, hmf_ref, hmb_ref, bg_ref,
                      co_ref, ag_ref, dg_ref, mg_ref, w_ref, g2_ref, wgu_ref, wdn_ref, fg_ref, *o_refs,
                      lam_init, n_ctx_tiles, final):
    is_ctx = pl.program_id(0) < n_ctx_tiles
    a = jnp.where(is_ctx, actx_ref[...], alat_ref[...])
    a = a * lax.rsqrt(_group_sumsq(a, A_V_DIM) * (1.0 / A_V_DIM) + EPS)
    a = a * ag_ref[...] * (1.0 - lam_init)
    od = odf_ref[...] + odb_ref[...]
    bg = bg_ref[...]
    d = od * lax.rsqrt(_group_sumsq(od, HEAD_DIM) * (1.0 / HEAD_DIM) + EPS) * dg_ref[...] * (bg * _sigmoid(bg))
    hm = hmf_ref[...] + hmb_ref[...]
    m = _sigmoid(co_ref[...]) * (hm * lax.rsqrt(_group_sumsq(hm, HEAD_DIM) * (1.0 / HEAD_DIM) + EPS) * mg_ref[...])
    cat = jnp.concatenate([a.astype(BF16), d.astype(BF16), m.astype(BF16)], axis=1)
    mix = jnp.dot(cat, w_ref[...], preferred_element_type=F32)
    x = jnp.where(is_ctx, xc_ref[...], xl_ref[...]) + mod_ref[:, 2 * D_MODEL:3 * D_MODEL] * mix
    y = x * lax.rsqrt(jnp.mean(x * x, axis=-1, keepdims=True) + EPS) * g2_ref[...]
    u = y * (1.0 + mod_ref[:, 4 * D_MODEL:5 * D_MODEL]) + mod_ref[:, 3 * D_MODEL:4 * D_MODEL]
    h = jnp.dot(u.astype(BF16), wgu_ref[...], preferred_element_type=F32)
    gate = h[:, 0:FFN_HIDDEN]
    act = (gate * _sigmoid(gate)) * h[:, FFN_HIDDEN:2 * FFN_HIDDEN]
    out = x + mod_ref[:, 5 * D_MODEL:6 * D_MODEL] * jnp.dot(act.astype(BF16), wdn_ref[...],
                                                             preferred_element_type=F32)
    if not final:
        o_refs[0][...] = out
        return
    out = out * lax.rsqrt(jnp.mean(out * out, axis=-1, keepdims=True) + EPS) * fg_ref[...]
    yc_ref, yl_ref = o_refs
    yl_ref[...] = out

    @pl.when(is_ctx)
    def _():
        yc_ref[...] = out


def _merge_ffn(x_all, mod_l, a_ctx, a_lat, od, hm, bgc, attn_g, delta_g, mlstm_g, w_out, lam_init,
               g2, w_gu, w_dn, final_g, final, n_ctx_tiles, tiles_per_seq):
    x_ctx, x_lat, lat_tile0, t_rows = x_all
    n_ctx_tiles, tiles_per_seq = _dense_tiles(n_ctx_tiles, tiles_per_seq)

    def mod_idx(i):
        return (jnp.where(i < n_ctx_tiles, 0, 1 + (i - n_ctx_tiles) // tiles_per_seq), 0, 0)

    full = lambda r, c: pl.BlockSpec((r, c), lambda i: (0, 0), pipeline_mode=pl.Buffered(1))
    row = lambda w, cb=0: pl.BlockSpec((DENSE_TILE, w), lambda i: (i, cb))
    a_w = HEADS * A_V_DIM
    if final:
        t_ctx = n_ctx_tiles * DENSE_TILE
        out_specs = [pl.BlockSpec((DENSE_TILE, D_MODEL), lambda i: (jnp.minimum(i, n_ctx_tiles - 1), 0)),
                     pl.BlockSpec((DENSE_TILE, D_MODEL), lambda i: (jnp.maximum(i - n_ctx_tiles, 0), 0))]
        out_shape = [jax.ShapeDtypeStruct((t_ctx, D_MODEL), F32),
                     jax.ShapeDtypeStruct((t_rows - t_ctx, D_MODEL), F32)]
    else:
        out_specs = [row(D_MODEL)]
        out_shape = [jax.ShapeDtypeStruct((t_rows, D_MODEL), F32)]
    return pl.pallas_call(
        functools.partial(_merge_ffn_kernel, lam_init=lam_init, n_ctx_tiles=n_ctx_tiles, final=final),
        grid=(t_rows // DENSE_TILE,),
        in_specs=_x_specs(n_ctx_tiles, lat_tile0) + [
                  pl.BlockSpec((None, 1, 6 * D_MODEL), mod_idx),
                  pl.BlockSpec((DENSE_TILE, a_w), lambda i: (jnp.minimum(i, n_ctx_tiles - 1), 0)),
                  pl.BlockSpec((DENSE_TILE, a_w), lambda i: (jnp.maximum(i - n_ctx_tiles, 0), 0)),
                  row(HW), row(HW), row(HW), row(HW), row(HW), row(HW, 4),
                  full(1, a_w), full(1, HW), full(1, HW), full(D_MODEL, D_MODEL),
                  full(1, D_MODEL), full(D_MODEL, 2 * FFN_HIDDEN), full(FFN_HIDDEN, D_MODEL),
                  full(1, D_MODEL)],
        out_specs=out_specs,
        out_shape=out_shape,
        compiler_params=_cparams(("arbitrary",)),
        name="merge_ffn",
    )(x_ctx, x_lat, mod_l.reshape(8, 1, 6 * D_MODEL), a_ctx, a_lat, od[0], od[1], hm[0], hm[1], bgc, bgc,
      jnp.tile(attn_g, HEADS).reshape(1, -1), jnp.tile(delta_g, HEADS).reshape(1, -1),
      jnp.tile(mlstm_g, HEADS).reshape(1, -1), w_out, g2.reshape(1, D_MODEL), w_gu, w_dn,
      final_g.reshape(1, D_MODEL))


def _rope_tables(dec_seq):
    n_rows = dec_seq // GRID_W
    rows = jnp.repeat(jnp.arange(n_rows, dtype=F32), GRID_W)
    cols = jnp.tile(jnp.arange(GRID_W, dtype=F32), n_rows)
    n_freq = A_QK_DIM // 4
    inv_freq = ROPE_BASE ** (-jnp.arange(n_freq, dtype=F32) / n_freq)
    ang = jnp.concatenate([rows[:, None] * inv_freq, cols[:, None] * inv_freq], axis=-1)
    cos = jnp.repeat(jnp.cos(ang), 2, axis=-1)
    sin = jnp.repeat(jnp.sin(ang), 2, axis=-1) * jnp.tile(jnp.array([-1.0, 1.0], F32), A_QK_DIM // 2)
    cos = jnp.concatenate([jnp.ones((DENSE_TILE, A_QK_DIM), F32), cos], axis=0)
    sin = jnp.concatenate([jnp.zeros((DENSE_TILE, A_QK_DIM), F32), sin], axis=0)
    return jnp.tile(cos, (1, 2)), jnp.tile(sin, (1, 2))


def _permute_proj(w_in_l, b_in_l):
    def perm(a):
        head, ba_bb, tail, ci_cf = a[..., 0:2560], a[..., 2560:2576], a[..., 2576:3600], a[..., 3600:3616]
        pad = jnp.zeros(a.shape[:-1] + (N_PROJ - 3616,), a.dtype)
        return jnp.concatenate([head, tail, ba_bb, ci_cf, pad], axis=-1)
    return perm(w_in_l).astype(BF16), perm(b_in_l)


def _gate_layouts(grow80, n_chunks):
    r = grow80.reshape(10, N_DIR, HEADS, n_chunks, CHUNK)
    tok = r[jnp.array([0, 2, 1, 4, 6])].transpose(1, 3, 4, 0, 2).reshape(N_DIR, n_chunks * CHUNK, 5 * HEADS)
    gtok = jnp.pad(tok, ((0, 0), (0, 0), (0, LANES - 5 * HEADS)))
    rows = r[jnp.array([0, 5, 8, 9])].transpose(1, 3, 0, 2, 4).reshape(N_DIR, n_chunks, 4, HW)
    grow = jnp.pad(rows, ((0, 0), (0, 0), (0, 4), (0, 0)))
    per_head = lambda q: r[q, :, :, :, 0].transpose(0, 2, 1)[..., None]
    zeros = lambda w: jnp.zeros((N_DIR, n_chunks, HEADS, w), F32)
    delta_rows = jnp.concatenate([zeros(CHUNK), per_head(3), zeros(LANES - CHUNK - 1)], axis=-1)
    mlstm_rows = jnp.concatenate([r[7].transpose(0, 2, 1, 3), per_head(8), per_head(9),
                                  zeros(LANES - CHUNK - 2)], axis=-1)
    return gtok, grow, jnp.concatenate([delta_rows, mlstm_rows], axis=2)


def _block_diag(s):
    eye = jnp.eye(HEADS, dtype=s.dtype)
    out = s[..., :, :, None, :] * eye[:, None, :, None]
    return out.reshape(s.shape[:-3] + (HW, HW))


def _block_diag_inv(s_bd, n_cols=HEAD_DIM):
    blk = lambda h: s_bd[..., h * HEAD_DIM:(h + 1) * HEAD_DIM, h * HEAD_DIM:h * HEAD_DIM + n_cols]
    return jnp.stack([blk(h) for h in range(HEADS)], axis=-3)


def _norm_block(n):
    return _block_diag(jnp.broadcast_to(n[..., None], n.shape + (HEAD_DIM,)))


def _with_zero_ctx(batch, lat_state):
    return jnp.concatenate([jnp.zeros((batch,) + lat_state.shape[1:], lat_state.dtype), lat_state], axis=0)


def kernel(x_prompt, x_sample, cache_attn_k, cache_attn_v, state_delta, state_mlstm_C, state_mlstm_n,
           state_mlstm_m, c, c_ctx, norm1_g, norm2_g, w_mod, b_mod, w_in, b_in, w_out, lambda_qk,
           attn_subln_g, delta_conv_w, delta_A_log, delta_dt_bias, delta_norm_g, mlstm_f_bias,
           mlstm_norm_g, w_gate_up, w_down, final_norm_g):
    batch, seq, _ = x_prompt.shape
    dec_batch, dec_seq, _ = x_sample.shape
    past_len = cache_attn_k.shape[2]
    assert seq == ROW_TILE and dec_seq % ROW_TILE == 0 and dec_batch + 1 <= 8
    t_ctx = batch * seq
    t_lat = dec_batch * dec_seq
    t_rows = t_ctx + t_lat
    n_ctx_tiles = t_ctx // ROW_TILE
    tiles_per_seq = dec_seq // ROW_TILE
    n_chunks = t_rows // CHUNK

    x_src = (x_prompt.reshape(t_ctx, D_MODEL), x_sample.reshape(t_lat, D_MODEL), 0, t_rows)
    cvecs = jnp.zeros((8, D_MODEL), F32).at[0].set(c_ctx).at[1:1 + dec_batch].set(c)
    mod = _ada_mod(cvecs, w_mod, b_mod)
    cos_t, sin_t = _rope_tables(dec_seq)

    ks_l, vs_l, sd_l, cm_l, nm_l, mm_l = [], [], [], [], [], []
    for l in range(DEPTH):
        lam_init = 0.8 - 0.6 * math.exp(-0.3 * l)
        w_p, b_p = _permute_proj(w_in[l], b_in[l])
        qk, vt, kv32, bqkv, bgc, gates = _in_proj(*x_src, mod[l], norm1_g[l], w_p, b_p, cos_t, sin_t,
                                                 n_ctx_tiles, tiles_per_seq)
        ks_l.append(kv32[:t_ctx, 0:512].reshape(batch, seq, 512))
        vs_l.append(kv32[:t_ctx, 512:1024].reshape(batch, seq, 512))

        par = jnp.zeros((32, LANES), F32)
        par = par.at[0:8, 0].set(delta_A_log[l].reshape(-1)).at[0:8, 1].set(delta_dt_bias[l].reshape(-1))
        par = par.at[24:32, 1].set(mlstm_f_bias[l].reshape(-1))
        gtok, grow, ghead = _gate_layouts(_gate_prep(gates[:, 0:32].T, par), n_chunks)

        a_ctx = _diff_attention(lambda_qk[l], qk, vt, None, None, lam_init, 0, batch, seq, seq, seq)
        a_lat = _diff_attention(lambda_qk[l], qk, vt, cache_attn_k[:, l].reshape(dec_batch, past_len, 512),
                                cache_attn_v[:, l].reshape(dec_batch, past_len, 512), lam_init, t_ctx,
                                dec_batch, dec_seq, min(1024, dec_seq), 512)

        dqkv = _delta_conv(bqkv, delta_conv_w[l], n_ctx_tiles, tiles_per_seq)
        od_f, od_b, s_fin = _delta_scan(dqkv, gtok, grow, ghead,
                                        _with_zero_ctx(batch, _block_diag(state_delta[:, l])),
                                        n_ctx_tiles, tiles_per_seq)
        sd_l.append(_block_diag_inv(s_fin[:batch]))

        cn_lat = jnp.concatenate([_block_diag(state_mlstm_C[:, l]), _norm_block(state_mlstm_n[:, l])], axis=-1)
        m_lat = jnp.repeat(state_mlstm_m[:, l], HEAD_DIM, axis=-1)
        hm_f, hm_b, cn_fin, m_fin = _mlstm_scan(
            bgc, gtok, grow, ghead, _with_zero_ctx(batch, cn_lat), _with_zero_ctx(batch, m_lat[..., None]),
            _with_zero_ctx(batch, m_lat[:, :, None, :]), n_ctx_tiles, tiles_per_seq)
        cm_l.append(_block_diag_inv(cn_fin[:batch, :, :, 0:HW]))
        nm_l.append(_block_diag_inv(cn_fin[:batch, :, :, HW:2 * HW], 1)[..., 0])
        mm_l.append(m_fin[:batch, :, ::HEAD_DIM, 0])

        outs = _merge_ffn(x_src, mod[l], a_ctx, a_lat, (od_f, od_b), (hm_f, hm_b), bgc, attn_subln_g[l],
                          delta_norm_g[l], mlstm_norm_g[l], w_out[l].astype(BF16), lam_init, norm2_g[l],
                          w_gate_up[l].astype(BF16), w_down[l].astype(BF16), final_norm_g, l == DEPTH - 1,
                          n_ctx_tiles, tiles_per_seq)
        x_src = (outs[0], outs[0], t_ctx // DENSE_TILE, t_rows)

    y_prompt = outs[0].reshape(batch, seq, D_MODEL)
    y_sample = outs[1].reshape(dec_batch, dec_seq, D_MODEL)
    new_k = jnp.stack(ks_l, axis=1).reshape(batch, DEPTH, seq, HEADS, 2, A_QK_DIM)
    new_v = jnp.stack(vs_l, axis=1).reshape(batch, DEPTH, seq, HEADS, A_V_DIM)
    return (y_prompt, y_sample, new_k, new_v, jnp.stack(sd_l, axis=1),
            jnp.stack(cm_l, axis=1), jnp.stack(nm_l, axis=1), jnp.stack(mm_l, axis=1))
```

```python
import functools
import math

import jax
import jax.numpy as jnp
from jax import lax
from jax.experimental import pallas as pl
from jax.experimental.pallas import tpu as pltpu

F32 = jnp.float32
BF16 = jnp.bfloat16

D_MODEL = 1024
DEPTH = 2
GRID_W = 64
N_DIR = 2
CHUNK = 64
ROPE_BASE = 10000.0
EPS = 1e-6
HEADS = 4
A_QK_DIM = 64
A_V_DIM = 128
HEAD_DIM = 64
CONV_K = 5
FFN_HIDDEN = 2816
ROW_TILE = 256
CHUNKS_PER_TILE = ROW_TILE // CHUNK
HW = HEADS * HEAD_DIM
LANES = 128
HALO = 8
BF16_ROWS = 16
GATE_GROUP = 32

N_PROJ = 3712
VMEM_LIMIT = 56 * 1024 * 1024


DENSE_TILE = 512


def _dense_tiles(n_ctx_tiles, tiles_per_seq):
    ratio = DENSE_TILE // ROW_TILE
    assert n_ctx_tiles % ratio == 0 and tiles_per_seq % ratio == 0
    return n_ctx_tiles // ratio, tiles_per_seq // ratio


def _cparams(sem):
    return pltpu.CompilerParams(dimension_semantics=sem, vmem_limit_bytes=VMEM_LIMIT)


def _bdot(a, b):
    return jnp.dot(a.astype(BF16), b.astype(BF16), preferred_element_type=F32)


def _bdot_nt(a, b):
    return lax.dot_general(a.astype(BF16), b.astype(BF16), (((1,), (1,)), ((), ())),
                           preferred_element_type=F32)


def _split3(x):
    hi = x.astype(BF16).astype(F32)
    r1 = x - hi
    mid = r1.astype(BF16).astype(F32)
    lo = (r1 - mid).astype(BF16).astype(F32)
    return hi, mid, lo


def _dot01(x, m01):
    d = functools.partial(jnp.dot, preferred_element_type=F32)
    hi, mid, lo = _split3(x)
    return d(hi.astype(BF16), m01) + d(mid.astype(BF16), m01) + d(lo.astype(BF16), m01)


def _group_sumsq(x, seg):
    sq = x * x
    out = []
    for t in range(x.shape[1] // LANES):
        tile = sq[:, t * LANES:(t + 1) * LANES]
        if seg == LANES:
            out.append(jnp.broadcast_to(jnp.sum(tile, axis=-1, keepdims=True), tile.shape))
        else:
            low = lax.broadcasted_iota(jnp.int32, tile.shape, 1) < seg
            s_lo = jnp.sum(jnp.where(low, tile, 0.0), axis=-1, keepdims=True)
            s_hi = jnp.sum(jnp.where(low, 0.0, tile), axis=-1, keepdims=True)
            out.append(jnp.where(low, s_lo, s_hi))
    return jnp.concatenate(out, axis=1)


def _sigmoid(x):
    return 1.0 / (1.0 + jnp.exp(-x))


def _softplus(x):
    return jnp.maximum(x, 0.0) + jnp.log1p(jnp.exp(-jnp.abs(x)))


def _ada_kernel(c_ref, w_ref, b_ref, o_ref):
    c = c_ref[...]
    s = c * _sigmoid(c)
    o_ref[...] = _bdot(s, w_ref[...]) + b_ref[...]


def _ada_mod(cvecs, w_mod, b_mod):
    n_out = w_mod.shape[-1]
    tn = 1024
    return pl.pallas_call(
        _ada_kernel,
        grid=(DEPTH, n_out // tn),
        in_specs=[pl.BlockSpec((8, D_MODEL), lambda l, j: (0, 0)),
                  pl.BlockSpec((None, D_MODEL, tn), lambda l, j: (l, 0, j)),
                  pl.BlockSpec((None, 1, tn), lambda l, j: (l, 0, j))],
        out_specs=pl.BlockSpec((None, 8, tn), lambda l, j: (l, 0, j)),
        out_shape=jax.ShapeDtypeStruct((DEPTH, 8, n_out), F32),
        compiler_params=_cparams(("parallel", "parallel")),
        name="ada_mod",
    )(cvecs, w_mod, b_mod.reshape(DEPTH, 1, n_out))


def _proj_kernel(xc_ref, xl_ref, mod_ref, g_ref, w_ref, b_ref, cos_ref, sin_ref,
                 qk_ref, vt_ref, kv32_ref, bqkv_ref, bgc_ref, gates_ref, *, n_ctx_tiles):
    x = jnp.where(pl.program_id(0) < n_ctx_tiles, xc_ref[...], xl_ref[...])
    y = x * lax.rsqrt(jnp.mean(x * x, axis=-1, keepdims=True) + EPS) * g_ref[...]
    u = y * (1.0 + mod_ref[:, D_MODEL:2 * D_MODEL]) + mod_ref[:, 0:D_MODEL]
    acc = _bdot(u, w_ref[...]) + b_ref[...]
    kv32_ref[...] = acc[:, 512:1536]
    vt_ref[...] = acc[:, 1024:1536].T.astype(BF16)
    bqkv_ref[...] = acc[:, 1536:2304]
    bgc_ref[...] = acc[:, 2304:3584]
    gates_ref[...] = acc[:, 3584:3712]
    cos = cos_ref[...]
    sin = sin_ref[...]
    even = (lax.broadcasted_iota(jnp.int32, cos.shape, 1) % 2) == 0
    for j in range(8):
        xj = acc[:, j * LANES:(j + 1) * LANES]
        swapped = jnp.where(even, pltpu.roll(xj, LANES - 1, 1), pltpu.roll(xj, 1, 1))
        r = xj * cos + swapped * sin
        if j < 4:
            r = r * (A_QK_DIM ** -0.5)
        qk_ref[:, j * LANES:(j + 1) * LANES] = r.astype(BF16)


def _x_specs(n_ctx_tiles, lat_tile0):
    return [pl.BlockSpec((DENSE_TILE, D_MODEL), lambda i: (jnp.minimum(i, n_ctx_tiles - 1), 0)),
            pl.BlockSpec((DENSE_TILE, D_MODEL), lambda i: (jnp.maximum(i - n_ctx_tiles, 0) + lat_tile0, 0))]


def _in_proj(x_ctx, x_lat, lat_tile0, t_rows, mod_l, g1, w_p, b_p, cos_t, sin_t, n_ctx_tiles, tiles_per_seq):
    n_tiles = t_rows // DENSE_TILE
    n_ctx_tiles, tiles_per_seq = _dense_tiles(n_ctx_tiles, tiles_per_seq)

    def mod_idx(i):
        return (jnp.where(i < n_ctx_tiles, 0, 1 + (i - n_ctx_tiles) // tiles_per_seq), 0, 0)

    def rope_idx(i):
        return (jnp.where(i < n_ctx_tiles, 0, 1 + (i - n_ctx_tiles) % tiles_per_seq), 0)

    row = lambda w: pl.BlockSpec((DENSE_TILE, w), lambda i: (i, 0))
    outs = [(1024, BF16, False), (512, BF16, True), (1024, F32, False), (768, F32, False), (1280, F32, False),
            (LANES, F32, False)]
    return pl.pallas_call(
        functools.partial(_proj_kernel, n_ctx_tiles=n_ctx_tiles),
        grid=(n_tiles,),
        in_specs=_x_specs(n_ctx_tiles, lat_tile0) + [
                  pl.BlockSpec((None, 1, 6 * D_MODEL), mod_idx),
                  pl.BlockSpec((1, D_MODEL), lambda i: (0, 0)),
                  pl.BlockSpec((D_MODEL, N_PROJ), lambda i: (0, 0)),
                  pl.BlockSpec((1, N_PROJ), lambda i: (0, 0)),
                  pl.BlockSpec((DENSE_TILE, LANES), rope_idx),
                  pl.BlockSpec((DENSE_TILE, LANES), rope_idx)],
        out_specs=[pl.BlockSpec((w, DENSE_TILE), lambda i: (0, i)) if tr else row(w) for w, _, tr in outs],
        out_shape=[jax.ShapeDtypeStruct((w, t_rows) if tr else (t_rows, w), dt) for w, dt, tr in outs],
        compiler_params=_cparams(("parallel",)),
        name="in_proj",
    )(x_ctx, x_lat, mod_l.reshape(8, 1, 6 * D_MODEL), g1.reshape(1, D_MODEL), w_p, b_p.reshape(1, N_PROJ),
      cos_t, sin_t)


GATE_ROWS = 80


def _gates_kernel(g_ref, par_ref, o_ref):
    x = g_ref[...]
    alog = par_ref[0:8, 0:1]
    dtb = par_ref[0:8, 1:2]
    fb = par_ref[24:32, 1:2]
    g = -jnp.exp(alog) * _softplus(x[0:8] + dtb)
    beta = _sigmoid(x[8:16])
    ig = x[16:24]
    lf = -_softplus(-(x[24:32] + fb))
    tl = x.shape[1]
    r = lax.broadcasted_iota(jnp.int32, (LANES, LANES), 0)
    c = lax.broadcasted_iota(jnp.int32, (LANES, LANES), 1)
    same = (r // CHUNK) == (c // CHUNK)
    pre = jnp.where(same & (r <= c), 1.0, 0.0).astype(BF16)
    suf = jnp.where(same & (r >= c), 1.0, 0.0).astype(BF16)
    tot = jnp.where(same, 1.0, 0.0).astype(BF16)
    m01 = jnp.concatenate([pre, suf, tot], axis=1)
    fwd16 = (lax.broadcasted_iota(jnp.int32, (16, LANES), 0) % 8) < 4
    fwd8 = lax.broadcasted_iota(jnp.int32, (8, LANES), 0) < 4
    pos = lax.broadcasted_iota(jnp.int32, (8, LANES), 1) % CHUNK

    def running_max(v):
        pf = sf = v
        s = 1
        while s < CHUNK:
            pf = jnp.maximum(pf, jnp.where(pos >= s, pltpu.roll(pf, s, 1), -jnp.inf))
            sf = jnp.maximum(sf, jnp.where(pos < CHUNK - s, pltpu.roll(sf, LANES - s, 1), -jnp.inf))
            s *= 2
        return pf, sf

    o_ref[16:24, :] = beta
    for j in range(tl // LANES):
        sl = slice(j * LANES, (j + 1) * LANES)
        xs = jnp.concatenate([g[:, sl], lf[:, sl]], axis=0)
        cs = _dot01(xs, m01)
        cum = jnp.where(fwd16, cs[:, 0:LANES], cs[:, LANES:2 * LANES])
        total = cs[:, 2 * LANES:3 * LANES]
        gc, gtot, b, btot = cum[0:8], total[0:8], cum[8:16], total[8:16]
        rr = ig[:, sl] - b
        kvl = btot - b + ig[:, sl]
        r_pf, r_sf = running_max(rr)
        k_pf, k_sf = running_max(kvl)
        o_ref[0:8, sl] = gc
        o_ref[8:16, sl] = gtot - gc
        o_ref[24:32, sl] = gtot
        o_ref[32:40, sl] = b
        o_ref[40:48, sl] = rr
        o_ref[48:56, sl] = jnp.where(fwd8, r_pf, r_sf)
        o_ref[56:64, sl] = kvl
        o_ref[64:72, sl] = btot
        o_ref[72:80, sl] = jnp.maximum(k_pf, k_sf)


def _gate_prep(gates_t, par):
    t_rows = gates_t.shape[1]
    tl = math.gcd(t_rows, 2048)
    return pl.pallas_call(
        _gates_kernel,
        grid=(t_rows // tl,),
        in_specs=[pl.BlockSpec((32, tl), lambda i: (0, i)),
                  pl.BlockSpec((32, LANES), lambda i: (0, 0))],
        out_specs=pl.BlockSpec((GATE_ROWS, tl), lambda i: (0, i)),
        out_shape=jax.ShapeDtypeStruct((GATE_ROWS, t_rows), F32),
        compiler_params=_cparams(("parallel",)),
        name="gate_prep",
    )(gates_t, par)


def _conv_kernel(x_ref, p_ref, n_ref, w_ref, o_ref, ext_s, *, n_ctx_tiles, tiles_per_seq):
    i = pl.program_id(0)
    j = (i - n_ctx_tiles) % tiles_per_seq
    is_ctx = i < n_ctx_tiles
    first = jnp.logical_or(is_ctx, j == 0)
    last = jnp.logical_or(is_ctx, j == tiles_per_seq - 1)
    ext_s[0:HALO, :] = jnp.where(first, 0.0, p_ref[...])
    ext_s[HALO:HALO + ROW_TILE, :] = x_ref[...]
    ext_s[HALO + ROW_TILE:, :] = jnp.where(last, 0.0, n_ref[...])
    w = w_ref[...]
    y = None
    for k in range(CONV_K):
        off = HALO - CONV_K // 2 + k
        term = ext_s[off:off + ROW_TILE, :] * w[k:k + 1, :]
        y = term if y is None else y + term
    y = y * _sigmoid(y)
    q = y[:, 0:HW]
    k_ = y[:, HW:2 * HW]
    o_ref[:, 0:HW] = q * lax.rsqrt(_group_sumsq(q, HEAD_DIM) + EPS) * (HEAD_DIM ** -0.5)
    o_ref[:, HW:2 * HW] = k_ * lax.rsqrt(_group_sumsq(k_, HEAD_DIM) + EPS)
    o_ref[:, 2 * HW:3 * HW] = y[:, 2 * HW:3 * HW]


def _delta_conv(bqkv, conv_w, n_ctx_tiles, tiles_per_seq):
    t_rows, width = bqkv.shape
    n_tiles = t_rows // ROW_TILE
    hb = ROW_TILE // HALO
    n_hb = t_rows // HALO
    return pl.pallas_call(
        functools.partial(_conv_kernel, n_ctx_tiles=n_ctx_tiles, tiles_per_seq=tiles_per_seq),
        grid=(n_tiles,),
        in_specs=[pl.BlockSpec((ROW_TILE, width), lambda i: (i, 0)),
                  pl.BlockSpec((HALO, width), lambda i: (jnp.maximum(i * hb - 1, 0), 0)),
                  pl.BlockSpec((HALO, width), lambda i: (jnp.minimum((i + 1) * hb, n_hb - 1), 0)),
                  pl.BlockSpec((CONV_K, width), lambda i: (0, 0))],
        out_specs=pl.BlockSpec((ROW_TILE, width), lambda i: (i, 0)),
        out_shape=jax.ShapeDtypeStruct((t_rows, width), F32),
        scratch_shapes=[pltpu.VMEM((ROW_TILE + 2 * HALO, width), F32)],
        compiler_params=_cparams(("parallel",)),
        name="delta_conv",
    )(bqkv, bqkv, bqkv, conv_w)


def _attn_kernel(lqk_ref, q_ref, *refs, lam_init, tq, tk):
    o_ref = refs[-1]
    lq = lqk_ref[...]
    lam = (jnp.exp(jnp.sum(lq[0:1] * lq[1:2], axis=-1, keepdims=True))
           - jnp.exp(jnp.sum(lq[2:3] * lq[3:4], axis=-1, keepdims=True)) + lam_init)
    tiles = [(k_ref, vt_ref, j) for k_ref, vt_ref in zip(refs[0:-1:2], refs[1:-1:2])
             for j in range(k_ref.shape[0] // tk)]
    n_kv = len(tiles)
    ones = jnp.ones((BF16_ROWS, tk), BF16)
    for hh in range(q_ref.shape[1] // LANES):
        hs = slice(hh * LANES, (hh + 1) * LANES)
        q = q_ref[:, hs]
        lane = lax.broadcasted_iota(jnp.int32, q.shape, 1)
        zero = jnp.zeros_like(q)
        q2 = jnp.concatenate([jnp.where(lane < A_QK_DIM, q, zero), jnp.where(lane >= A_QK_DIM, q, zero)],
                             axis=0)

        def scores(t):
            k_ref, _, j = tiles[t]
            return lax.dot_general(k_ref[j * tk:(j + 1) * tk, hs], q2, (((1,), (1,)), ((), ())),
                                   preferred_element_type=F32)

        m = acc = None
        st_next = scores(0)
        for j in range(n_kv):
            st = st_next
            if j + 1 < n_kv:
                st_next = scores(j + 1)
            m_new = jnp.max(st, axis=0, keepdims=True)
            if j > 0:
                m_new = jnp.maximum(m, m_new)
            p = jnp.exp(st - m_new)
            _, vt_ref, jt = tiles[j]
            vt = jnp.concatenate([vt_ref[hs, jt * tk:(jt + 1) * tk], ones], axis=0)
            pv = jnp.dot(vt, p.astype(BF16), preferred_element_type=F32)
            acc = pv if j == 0 else jnp.exp(m - m_new) * acc + pv
            m = m_new
        o = acc[0:A_V_DIM] / acc[A_V_DIM:A_V_DIM + 1]
        o_ref[:, hs] = (o[:, 0:tq] - lam * o[:, tq:2 * tq]).T


def _diff_attention(lqk, qk, vt, cache_k, cache_v, lam_init, q_row0, n_seq, lq, tq, tk):
    assert q_row0 % lq == 0
    qb0 = q_row0 // tq
    sb0 = q_row0 // lq
    nq = lq // tq
    hb = HEADS if lq <= ROW_TILE else 1
    n_hblk = HEADS // hb
    in_specs = [pl.BlockSpec((4, A_QK_DIM), lambda b, h, i: (0, 0)),
                pl.BlockSpec((tq, hb * LANES), lambda b, h, i: (qb0 + b * nq + i, h)),
                pl.BlockSpec((lq, hb * LANES), lambda b, h, i: (sb0 + b, n_hblk + h)),
                pl.BlockSpec((hb * A_V_DIM, lq), lambda b, h, i: (h, sb0 + b))]
    args = [lqk, qk, qk, vt]
    if cache_k is not None:
        past = cache_k.shape[1]
        in_specs += [pl.BlockSpec((None, past, hb * LANES), lambda b, h, i: (b, 0, h)),
                     pl.BlockSpec((None, hb * A_V_DIM, past), lambda b, h, i: (b, h, 0))]
        args += [cache_k.astype(BF16), cache_v.astype(BF16).transpose(0, 2, 1)]
    return pl.pallas_call(
        functools.partial(_attn_kernel, lam_init=lam_init, tq=tq, tk=tk),
        grid=(n_seq, n_hblk, nq),
        in_specs=in_specs,
        out_specs=pl.BlockSpec((tq, hb * LANES), lambda b, h, i: (b * nq + i, h)),
        out_shape=jax.ShapeDtypeStruct((n_seq * lq, HEADS * A_V_DIM), F32),
        compiler_params=_cparams(("parallel", "parallel", "parallel")),
        name="diff_attention",
    )(*args)


def _same_head(n_cols=HW):
    r = lax.broadcasted_iota(jnp.int32, (HW, n_cols), 0) // HEAD_DIM
    c = (lax.broadcasted_iota(jnp.int32, (HW, n_cols), 1) % HW) // HEAD_DIM
    return r == c


def _bd(x, same):
    return jnp.where(same, jnp.concatenate([x, x, x, x], axis=0), 0.0)


def _cat_index():
    i = lax.broadcasted_iota(jnp.int32, (CHUNK, HW), 0)
    j = lax.broadcasted_iota(jnp.int32, (CHUNK, HW), 1) % CHUNK
    return i, j


def _scan_masks(direction):
    i, j = _cat_index()
    diff = (i - j) * (1 - 2 * direction)
    return diff >= 0, diff > 0


def _expand_matrix(first_lane, n_q):
    r = lax.broadcasted_iota(jnp.int32, (LANES, n_q * HW), 0)
    c = lax.broadcasted_iota(jnp.int32, (LANES, n_q * HW), 1)
    s = r % GATE_GROUP - first_lane
    hit = (r < 3 * GATE_GROUP) & (s >= 0) & (s < 4 * n_q) & (s // HEADS == c // HW) \
        & (s % HEADS == (c % HW) // HEAD_DIM)
    return jnp.where(hit, 1.0, 0.0).astype(BF16)


def _expand(tok, e3):
    hi, mid, lo = _split3(tok)
    x3 = hi + pltpu.roll(mid, GATE_GROUP, 1) + pltpu.roll(lo, 2 * GATE_GROUP, 1)
    return jnp.dot(x3.astype(BF16), e3, preferred_element_type=F32)


def _head_rows(hd, lane):
    return jnp.concatenate([jnp.broadcast_to(hd[h:h + 1, lane:lane + 1], (HEAD_DIM, 1)) for h in range(HEADS)],
                           axis=0)


def _pad_rows(x):
    return jnp.concatenate([x, jnp.zeros_like(x)], axis=0)


def _unit_tri_inverse(mats, same):
    i, j = _cat_index()
    b16 = (i // 16) == (j // 16)
    b32 = (i // 32) == (j // 32)
    eye = jnp.where(i == j, 1.0, 0.0)
    xs = [jnp.where(b16, a, 0.0) for a in mats]
    ts = [eye - x for x in xs]
    xbd = [_bd(x, same) for x in xs]
    for _ in range(3):
        xs = [_bdot(x, b) for x, b in zip(xs, xbd)]
        xbd = [_bd(x, same) for x in xs]
        ts = [t + _bdot(t, b) for t, b in zip(ts, xbd)]
    for inside in (jnp.logical_and(b32, jnp.logical_not(b16)), jnp.logical_not(b32)):
        mids = [_bdot(t, _bd(jnp.where(inside, a, 0.0), same)) for t, a in zip(ts, mats)]
        ts = [t - _bdot(m, _bd(t, same)) for t, m in zip(ts, mids)]
    return ts


def _scan_job(n_ctx_tiles, tiles_per_seq):
    j = pl.program_id(0)
    is_ctx = j < n_ctx_tiles
    return jnp.logical_or(is_ctx, (j - n_ctx_tiles) % tiles_per_seq == 0), is_ctx


def _chunks_of(refs_f, refs_b):
    out = []
    for d, (q, k, v, tok, rows, heads) in enumerate((refs_f, refs_b)):
        for c in range(CHUNKS_PER_TILE):
            sl = slice(c * CHUNK, (c + 1) * CHUNK)
            out.append((q[sl, :], k[sl, :], v[sl, :], tok[sl, :], rows[c], heads[c], d))
    return out


def _delta_prep(chunks):
    same = _same_head()
    masks = [_scan_masks(0), _scan_masks(1)]
    e3 = _expand_matrix(0, 3)
    n = len(chunks)
    ex = [_expand(ch[3], e3) for ch in chunks]
    gc = [e[:, 0:HW] for e in ex]
    kb = [chunks[i][1] * ex[i][:, HW:2 * HW] for i in range(n)]
    kk = [_bdot_nt(jnp.concatenate([kb[i], chunks[i][0]], axis=0), _bd(chunks[i][1], same)) for i in range(n)]
    e_incl = [jnp.exp(jnp.where(masks[chunks[i][6]][0], gc[i] - chunks[i][4][0:1, :], -jnp.inf)) for i in range(n)]
    a = [jnp.where(masks[chunks[i][6]][1], kk[i][0:CHUNK] * e_incl[i], 0.0) for i in range(n)]
    aqk = [kk[i][CHUNK:2 * CHUNK] * e_incl[i] for i in range(n)]
    tinv = _unit_tri_inverse(a, same)
    egc = [jnp.exp(g) for g in gc]
    rhs = [jnp.concatenate([_bd(chunks[i][2] * ex[i][:, HW:2 * HW], same), _bd(kb[i] * egc[i], same)], axis=1)
           for i in range(n)]
    uw = [_bdot(tinv[i], rhs[i]) for i in range(n)]
    kd_t = [_pad_rows(chunks[i][1] * jnp.exp(ex[i][:, 2 * HW:3 * HW])).T for i in range(n)]
    gl = [jnp.exp(_head_rows(chunks[i][5][0:HEADS], CHUNK)) for i in range(n)]
    return [(uw[i][:, 0:HW], uw[i][:, HW:2 * HW], chunks[i][0] * egc[i], kd_t[i], aqk[i], gl[i])
            for i in range(n)]


def _delta_step(pre, s_ref, o_ref, c, same):
    u, w, qd, kd_t, aqk, gl = pre
    s = s_ref[...]
    wq = _bdot(jnp.concatenate([w, qd], axis=0), s)
    v_new = u - wq[0:CHUNK]
    o_ref[c * CHUNK:(c + 1) * CHUNK, :] = wq[CHUNK:2 * CHUNK] + _bdot(aqk, _bd(v_new, same))
    s_ref[...] = s * gl + jnp.where(same, _bdot(kd_t, _pad_rows(v_new)), 0.0)


def _delta_kernel(qf, kf, vf, qb, kb, vb, tokf, rowf, headf, tokb, rowb, headb, s0_ref,
                  of_ref, ob_ref, sout_ref, sf_s, sb_s, *, n_ctx_tiles, tiles_per_seq):
    first, is_ctx = _scan_job(n_ctx_tiles, tiles_per_seq)

    @pl.when(first)
    def _():
        sf_s[...] = jnp.where(is_ctx, 0.0, s0_ref[0])
        sb_s[...] = jnp.where(is_ctx, 0.0, s0_ref[1])

    pre = _delta_prep(_chunks_of((qf, kf, vf, tokf, rowf, headf), (qb, kb, vb, tokb, rowb, headb)))
    same = _same_head()
    for i in range(CHUNKS_PER_TILE):
        cb = CHUNKS_PER_TILE - 1 - i
        _delta_step(pre[i], sf_s, of_ref, i, same)
        _delta_step(pre[CHUNKS_PER_TILE + cb], sb_s, ob_ref, cb, same)

    @pl.when(is_ctx)
    def _():
        sout_ref[0] = sf_s[...]
        sout_ref[1] = sb_s[...]


def _mlstm_prep(chunks):
    same = _same_head()
    masks = [_scan_masks(0), _scan_masks(1)]
    e3 = _expand_matrix(12, 2)
    n = len(chunks)
    ex = [_expand(ch[3], e3) for ch in chunks]
    b = [e[:, 0:HW] for e in ex]
    ks = [ch[1] * (HEAD_DIM ** -0.5) for ch in chunks]
    qk = [_bdot_nt(chunks[i][0], _bd(ks[i], same)) for i in range(n)]
    dmat = [jnp.where(masks[chunks[i][6]][0], b[i] + chunks[i][4][1:2, :], -jnp.inf) for i in range(n)]
    dmax = [ex[i][:, 0:HW] + ex[i][:, HW:2 * HW] for i in range(n)]
    k_t = [_pad_rows(k).T for k in ks]
    ones = jnp.ones((CHUNK, HW), F32)
    vaug = [_pad_rows(jnp.concatenate([ch[2], ones], axis=1)) for ch in chunks]
    vbd = [jnp.concatenate([_bd(ch[2], same), jnp.where(same, 1.0, 0.0)], axis=1) for ch in chunks]
    kvl = [jnp.concatenate([jnp.broadcast_to(ch[5][HEADS + h:HEADS + h + 1, :], (HEAD_DIM, LANES))
                            for h in range(HEADS)], axis=0) for ch in chunks]
    btot_col = [_head_rows(ch[5][HEADS:2 * HEADS], CHUNK) for ch in chunks]
    mkv_col = [_head_rows(ch[5][HEADS:2 * HEADS], CHUNK + 1) for ch in chunks]
    return [(chunks[i][0], k_t[i], vaug[i], vbd[i], b[i], dmat[i], dmax[i], qk[i], kvl[i], btot_col[i],
             mkv_col[i], chunks[i][4][2:3, :], chunks[i][4][3:4, :]) for i in range(n)]


def _mlstm_step(pre, cn_ref, mcol_ref, mrow_ref, o_ref, c, same2):
    q, k_t, vaug, vbd, b, dmat, dmax, qk, kvl, btot_col, mkv_col, btot_row, mkv_row = pre
    cn = cn_ref[...]
    ms_col = mcol_ref[...]
    ms_row = mrow_ref[...]
    inter = b + ms_row
    m_t = jnp.maximum(inter, dmax)
    s = qk * jnp.exp(dmat - m_t)
    w_inter = jnp.exp(inter - m_t)
    nd = jnp.concatenate([w_inter, w_inter], axis=1) * _bdot(q, cn) + _bdot(s, vbd)
    o_ref[c * CHUNK:(c + 1) * CHUNK, :] = nd[:, 0:HW] / jnp.maximum(jnp.abs(nd[:, HW:2 * HW]), jnp.exp(-m_t))
    m_new = jnp.maximum(btot_col + ms_col, mkv_col)
    lane = lax.broadcasted_iota(jnp.int32, kvl.shape, 1)
    wk = jnp.where(lane < CHUNK, jnp.exp(kvl - m_new), 0.0)
    cn_ref[...] = jnp.exp(btot_col + ms_col - m_new) * cn + jnp.where(same2, _bdot(k_t * wk, vaug), 0.0)
    mcol_ref[...] = m_new
    mrow_ref[...] = jnp.maximum(btot_row + ms_row, mkv_row)


def _mlstm_kernel(qf, kf, vf, qb, kb, vb, tokf, rowf, headf, tokb, rowb, headb, cn0_ref, mc0_ref, mr0_ref,
                  of_ref, ob_ref, cnout_ref, mout_ref, cnf_s, cnb_s, mcf_s, mcb_s, mrf_s, mrb_s,
                  *, n_ctx_tiles, tiles_per_seq):
    first, is_ctx = _scan_job(n_ctx_tiles, tiles_per_seq)

    @pl.when(first)
    def _():
        cnf_s[...] = jnp.where(is_ctx, 0.0, cn0_ref[0])
        cnb_s[...] = jnp.where(is_ctx, 0.0, cn0_ref[1])
        mcf_s[...] = jnp.where(is_ctx, 0.0, mc0_ref[0])
        mcb_s[...] = jnp.where(is_ctx, 0.0, mc0_ref[1])
        mrf_s[...] = jnp.where(is_ctx, 0.0, mr0_ref[0])
        mrb_s[...] = jnp.where(is_ctx, 0.0, mr0_ref[1])

    pre = _mlstm_prep(_chunks_of((qf, kf, vf, tokf, rowf, headf), (qb, kb, vb, tokb, rowb, headb)))
    same2 = _same_head(2 * HW)
    for i in range(CHUNKS_PER_TILE):
        cb = CHUNKS_PER_TILE - 1 - i
        _mlstm_step(pre[i], cnf_s, mcf_s, mrf_s, of_ref, i, same2)
        _mlstm_step(pre[CHUNKS_PER_TILE + cb], cnb_s, mcb_s, mrb_s, ob_ref, cb, same2)

    @pl.when(is_ctx)
    def _():
        cnout_ref[0] = cnf_s[...]
        cnout_ref[1] = cnb_s[...]
        mout_ref[0] = jnp.broadcast_to(mcf_s[...], (HW, LANES))
        mout_ref[1] = jnp.broadcast_to(mcb_s[...], (HW, LANES))


def _scan_call(kernel_fn, name, src, col_blocks, gtok, grow, ghead, states, state_out_shapes, scratch,
               n_ctx_tiles, tiles_per_seq):
    t_rows = src.shape[0]
    n_tiles = t_rows // ROW_TILE

    def seq_of(j):
        return jnp.where(j < n_ctx_tiles, j, n_ctx_tiles + (j - n_ctx_tiles) // tiles_per_seq)

    def back(j):
        jj = j - n_ctx_tiles
        mirrored = n_ctx_tiles + (jj // tiles_per_seq) * tiles_per_seq + tiles_per_seq - 1 - jj % tiles_per_seq
        return jnp.where(j < n_ctx_tiles, j, mirrored)

    def tile_spec(cb, bwd):
        return pl.BlockSpec((ROW_TILE, HW), (lambda j: (back(j), cb)) if bwd else (lambda j: (j, cb)))

    def gate_specs(d):
        tile = (lambda j: back(j)) if d else (lambda j: j)
        return [pl.BlockSpec((None, ROW_TILE, LANES), lambda j: (d, tile(j), 0)),
                pl.BlockSpec((None, CHUNKS_PER_TILE, 8, HW), lambda j: (d, tile(j), 0, 0)),
                pl.BlockSpec((None, CHUNKS_PER_TILE, 8, LANES), lambda j: (d, tile(j), 0, 0))]

    def state_in_spec(shape):
        return pl.BlockSpec((None,) + tuple(shape),
                            lambda j: (jnp.maximum(seq_of(j) - n_ctx_tiles, 0),) + (0,) * len(shape))

    def state_out_spec(shape):
        return pl.BlockSpec((None,) + tuple(shape),
                            lambda j: (jnp.minimum(j, n_ctx_tiles - 1),) + (0,) * len(shape))

    in_specs = ([tile_spec(cb, False) for cb in col_blocks] + [tile_spec(cb, True) for cb in col_blocks]
                + gate_specs(0) + gate_specs(1) + [state_in_spec(s.shape[1:]) for s in states])
    out_specs = ([pl.BlockSpec((ROW_TILE, HW), lambda j: (j, 0)),
                  pl.BlockSpec((ROW_TILE, HW), lambda j: (back(j), 0))]
                 + [state_out_spec(s[1:]) for s in state_out_shapes])
    out_shape = ([jax.ShapeDtypeStruct((t_rows, HW), F32)] * 2
                 + [jax.ShapeDtypeStruct(tuple(s), F32) for s in state_out_shapes])
    return pl.pallas_call(
        functools.partial(kernel_fn, n_ctx_tiles=n_ctx_tiles, tiles_per_seq=tiles_per_seq),
        grid=(n_tiles,),
        in_specs=in_specs, out_specs=out_specs, out_shape=out_shape,
        scratch_shapes=scratch,
        compiler_params=_cparams(("arbitrary",)),
        name=name,
    )(*([src] * (2 * len(col_blocks))), gtok, grow, ghead, gtok, grow, ghead, *states)


def _delta_scan(dqkv, gtok, grow, ghead, s0, n_ctx_tiles, tiles_per_seq):
    return _scan_call(_delta_kernel, "delta_scan", dqkv, (0, 1, 2), gtok, grow, ghead, [s0],
                      [(n_ctx_tiles,) + s0.shape[1:]],
                      [pltpu.VMEM((HW, HW), F32)] * 2, n_ctx_tiles, tiles_per_seq)


def _mlstm_scan(bgc, gtok, grow, ghead, cn0, mc0, mr0, n_ctx_tiles, tiles_per_seq):
    return _scan_call(_mlstm_kernel, "mlstm_scan", bgc, (1, 2, 3), gtok, grow, ghead, [cn0, mc0, mr0],
                      [(n_ctx_tiles,) + cn0.shape[1:], (n_ctx_tiles, N_DIR, HW, LANES)],
                      [pltpu.VMEM((HW, 2 * HW), F32)] * 2 + [pltpu.VMEM((HW, 1), F32)] * 2
                      + [pltpu.VMEM((1, HW), F32)] * 2, n_ctx_tiles, tiles_per_seq)


def _merge_ffn_kernel(xc_ref, xl_ref, mod_ref, actx_ref, alat_ref, odf_ref, odb_ref, hmf_ref, hmb_ref, bg_ref,
                      co_ref, ag_ref, dg_ref, mg_ref, w_ref, g2_ref, wgu_ref, wdn_ref, fg_ref, *o_refs,
                      lam_init, n_ctx_tiles, final):
    is_ctx = pl.program_id(0) < n_ctx_tiles
    a = jnp.where(is_ctx, actx_ref[...], alat_ref[...])
    a = a * lax.rsqrt(_group_sumsq(a, A_V_DIM) * (1.0 / A_V_DIM) + EPS)
    a = a * ag_ref[...] * (1.0 - lam_init)
    od = odf_ref[...] + odb_ref[...]
    bg = bg_ref[...]
    d = od * lax.rsqrt(_group_sumsq(od, HEAD_DIM) * (1.0 / HEAD_DIM) + EPS) * dg_ref[...] * (bg * _sigmoid(bg))
    hm = hmf_ref[...] + hmb_ref[...]
    m = _sigmoid(co_ref[...]) * (hm * lax.rsqrt(_group_sumsq(hm, HEAD_DIM) * (1.0 / HEAD_DIM) + EPS) * mg_ref[...])
    cat = jnp.concatenate([a.astype(BF16), d.astype(BF16), m.astype(BF16)], axis=1)
    mix = jnp.dot(cat, w_ref[...], preferred_element_type=F32)
    x = jnp.where(is_ctx, xc_ref[...], xl_ref[...]) + mod_ref[:, 2 * D_MODEL:3 * D_MODEL] * mix
    y = x * lax.rsqrt(jnp.mean(x * x, axis=-1, keepdims=True) + EPS) * g2_ref[...]
    u = y * (1.0 + mod_ref[:, 4 * D_MODEL:5 * D_MODEL]) + mod_ref[:, 3 * D_MODEL:4 * D_MODEL]
    h = jnp.dot(u.astype(BF16), wgu_ref[...], preferred_element_type=F32)
    gate = h[:, 0:FFN_HIDDEN]
    act = (gate * _sigmoid(gate)) * h[:, FFN_HIDDEN:2 * FFN_HIDDEN]
    out = x + mod_ref[:, 5 * D_MODEL:6 * D_MODEL] * jnp.dot(act.astype(BF16), wdn_ref[...],
                                                             preferred_element_type=F32)
    if not final:
        o_refs[0][...] = out
        return
    out = out * lax.rsqrt(jnp.mean(out * out, axis=-1, keepdims=True) + EPS) * fg_ref[...]
    yc_ref, yl_ref = o_refs
    yl_ref[...] = out

    @pl.when(is_ctx)
    def _():
        yc_ref[...] = out


def _merge_ffn(x_all, mod_l, a_ctx, a_lat, od, hm, bgc, attn_g, delta_g, mlstm_g, w_out, lam_init,
               g2, w_gu, w_dn, final_g, final, n_ctx_tiles, tiles_per_seq):
    x_ctx, x_lat, lat_tile0, t_rows = x_all
    n_ctx_tiles, tiles_per_seq = _dense_tiles(n_ctx_tiles, tiles_per_seq)

    def mod_idx(i):
        return (jnp.where(i < n_ctx_tiles, 0, 1 + (i - n_ctx_tiles) // tiles_per_seq), 0, 0)

    full = lambda r, c: pl.BlockSpec((r, c), lambda i: (0, 0), pipeline_mode=pl.Buffered(1))
    row = lambda w, cb=0: pl.BlockSpec((DENSE_TILE, w), lambda i: (i, cb))
    a_w = HEADS * A_V_DIM
    if final:
        t_ctx = n_ctx_tiles * DENSE_TILE
        out_specs = [pl.BlockSpec((DENSE_TILE, D_MODEL), lambda i: (jnp.minimum(i, n_ctx_tiles - 1), 0)),
                     pl.BlockSpec((DENSE_TILE, D_MODEL), lambda i: (jnp.maximum(i - n_ctx_tiles, 0), 0))]
        out_shape = [jax.ShapeDtypeStruct((t_ctx, D_MODEL), F32),
                     jax.ShapeDtypeStruct((t_rows - t_ctx, D_MODEL), F32)]
    else:
        out_specs = [row(D_MODEL)]
        out_shape = [jax.ShapeDtypeStruct((t_rows, D_MODEL), F32)]
    return pl.pallas_call(
        functools.partial(_merge_ffn_kernel, lam_init=lam_init, n_ctx_tiles=n_ctx_tiles, final=final),
        grid=(t_rows // DENSE_TILE,),
        in_specs=_x_specs(n_ctx_tiles, lat_tile0) + [
                  pl.BlockSpec((None, 1, 6 * D_MODEL), mod_idx),
                  pl.BlockSpec((DENSE_TILE, a_w), lambda i: (jnp.minimum(i, n_ctx_tiles - 1), 0)),
                  pl.BlockSpec((DENSE_TILE, a_w), lambda i: (jnp.maximum(i - n_ctx_tiles, 0), 0)),
                  row(HW), row(HW), row(HW), row(HW), row(HW), row(HW, 4),
                  full(1, a_w), full(1, HW), full(1, HW), full(D_MODEL, D_MODEL),
                  full(1, D_MODEL), full(D_MODEL, 2 * FFN_HIDDEN), full(FFN_HIDDEN, D_MODEL),
                  full(1, D_MODEL)],
        out_specs=out_specs,
        out_shape=out_shape,
        compiler_params=_cparams(("arbitrary",)),
        name="merge_ffn",
    )(x_ctx, x_lat, mod_l.reshape(8, 1, 6 * D_MODEL), a_ctx, a_lat, od[0], od[1], hm[0], hm[1], bgc, bgc,
      jnp.tile(attn_g, HEADS).reshape(1, -1), jnp.tile(delta_g, HEADS).reshape(1, -1),
      jnp.tile(mlstm_g, HEADS).reshape(1, -1), w_out, g2.reshape(1, D_MODEL), w_gu, w_dn,
      final_g.reshape(1, D_MODEL))


def _rope_tables(dec_seq):
    n_rows = dec_seq // GRID_W
    rows = jnp.repeat(jnp.arange(n_rows, dtype=F32), GRID_W)
    cols = jnp.tile(jnp.arange(GRID_W, dtype=F32), n_rows)
    n_freq = A_QK_DIM // 4
    inv_freq = ROPE_BASE ** (-jnp.arange(n_freq, dtype=F32) / n_freq)
    ang = jnp.concatenate([rows[:, None] * inv_freq, cols[:, None] * inv_freq], axis=-1)
    cos = jnp.repeat(jnp.cos(ang), 2, axis=-1)
    sin = jnp.repeat(jnp.sin(ang), 2, axis=-1) * jnp.tile(jnp.array([-1.0, 1.0], F32), A_QK_DIM // 2)
    cos = jnp.concatenate([jnp.ones((DENSE_TILE, A_QK_DIM), F32), cos], axis=0)
    sin = jnp.concatenate([jnp.zeros((DENSE_TILE, A_QK_DIM), F32), sin], axis=0)
    return jnp.tile(cos, (1, 2)), jnp.tile(sin, (1, 2))


def _permute_proj(w_in_l, b_in_l):
    def perm(a):
        head, ba_bb, tail, ci_cf = a[..., 0:2560], a[..., 2560:2576], a[..., 2576:3600], a[..., 3600:3616]
        pad = jnp.zeros(a.shape[:-1] + (N_PROJ - 3616,), a.dtype)
        return jnp.concatenate([head, tail, ba_bb, ci_cf, pad], axis=-1)
    return perm(w_in_l).astype(BF16), perm(b_in_l)


def _gate_layouts(grow80, n_chunks):
    r = grow80.reshape(10, N_DIR, HEADS, n_chunks, CHUNK)
    tok = r[jnp.array([0, 2, 1, 4, 6])].transpose(1, 3, 4, 0, 2).reshape(N_DIR, n_chunks * CHUNK, 5 * HEADS)
    gtok = jnp.pad(tok, ((0, 0), (0, 0), (0, LANES - 5 * HEADS)))
    rows = r[jnp.array([0, 5, 8, 9])].transpose(1, 3, 0, 2, 4).reshape(N_DIR, n_chunks, 4, HW)
    grow = jnp.pad(rows, ((0, 0), (0, 0), (0, 4), (0, 0)))
    per_head = lambda q: r[q, :, :, :, 0].transpose(0, 2, 1)[..., None]
    zeros = lambda w: jnp.zeros((N_DIR, n_chunks, HEADS, w), F32)
    delta_rows = jnp.concatenate([zeros(CHUNK), per_head(3), zeros(LANES - CHUNK - 1)], axis=-1)
    mlstm_rows = jnp.concatenate([r[7].transpose(0, 2, 1, 3), per_head(8), per_head(9),
                                  zeros(LANES - CHUNK - 2)], axis=-1)
    return gtok, grow, jnp.concatenate([delta_rows, mlstm_rows], axis=2)


def _block_diag(s):
    eye = jnp.eye(HEADS, dtype=s.dtype)
    out = s[..., :, :, None, :] * eye[:, None, :, None]
    return out.reshape(s.shape[:-3] + (HW, HW))


def _block_diag_inv(s_bd, n_cols=HEAD_DIM):
    blk = lambda h: s_bd[..., h * HEAD_DIM:(h + 1) * HEAD_DIM, h * HEAD_DIM:h * HEAD_DIM + n_cols]
    return jnp.stack([blk(h) for h in range(HEADS)], axis=-3)


def _norm_block(n):
    return _block_diag(jnp.broadcast_to(n[..., None], n.shape + (HEAD_DIM,)))


def kernel(x_prompt, x_sample, cache_attn_k, cache_attn_v, state_delta, state_mlstm_C, state_mlstm_n,
           state_mlstm_m, c, c_ctx, norm1_g, norm2_g, w_mod, b_mod, w_in, b_in, w_out, lambda_qk,
           attn_subln_g, delta_conv_w, delta_A_log, delta_dt_bias, delta_norm_g, mlstm_f_bias,
           mlstm_norm_g, w_gate_up, w_down, final_norm_g):
    batch, seq, _ = x_prompt.shape
    dec_batch, dec_seq, _ = x_sample.shape
    past_len = cache_attn_k.shape[2]
    assert seq == ROW_TILE and dec_seq % ROW_TILE == 0 and dec_batch + 1 <= 8
    t_ctx = batch * seq
    t_lat = dec_batch * dec_seq
    t_rows = t_ctx + t_lat
    n_ctx_tiles = t_ctx // ROW_TILE
    tiles_per_seq = dec_seq // ROW_TILE
    n_chunks = t_rows // CHUNK

    x_src = (x_prompt.reshape(t_ctx, D_MODEL), x_sample.reshape(t_lat, D_MODEL), 0, t_rows)
    cvecs = jnp.zeros((8, D_MODEL), F32).at[0].set(c_ctx).at[1:1 + dec_batch].set(c)
    mod = _ada_mod(cvecs, w_mod, b_mod)
    cos_t, sin_t = _rope_tables(dec_seq)

    ks_l, vs_l, sd_l, cm_l, nm_l, mm_l = [], [], [], [], [], []
    for l in range(DEPTH):
        lam_init = 0.8 - 0.6 * math.exp(-0.3 * l)
        w_p, b_p = _permute_proj(w_in[l], b_in[l])
        qk, vt, kv32, bqkv, bgc, gates = _in_proj(*x_src, mod[l], norm1_g[l], w_p, b_p, cos_t, sin_t,
                                                 n_ctx_tiles, tiles_per_seq)
        ks_l.append(kv32[:t_ctx, 0:512].reshape(batch, seq, 512))
        vs_l.append(kv32[:t_ctx, 512:1024].reshape(batch, seq, 512))

        par = jnp.zeros((32, LANES), F32)
        par = par.at[0:8, 0].set(delta_A_log[l].reshape(-1)).at[0:8, 1].set(delta_dt_bias[l].reshape(-1))
        par = par.at[24:32, 1].set(mlstm_f_bias[l].reshape(-1))
        gtok, grow, ghead = _gate_layouts(_gate_prep(gates[:, 0:32].T, par), n_chunks)

        a_ctx = _diff_attention(lambda_qk[l], qk, vt, None, None, lam_init, 0, batch, seq, seq, seq)
        a_lat = _diff_attention(lambda_qk[l], qk, vt, cache_attn_k[:, l].reshape(dec_batch, past_len, 512),
                                cache_attn_v[:, l].reshape(dec_batch, past_len, 512), lam_init, t_ctx,
                                dec_batch, dec_seq, min(1024, dec_seq), 512)

        dqkv = _delta_conv(bqkv, delta_conv_w[l], n_ctx_tiles, tiles_per_seq)
        od_f, od_b, s_fin = _delta_scan(dqkv, gtok, grow, ghead, _block_diag(state_delta[:, l]),
                                        n_ctx_tiles, tiles_per_seq)
        sd_l.append(_block_diag_inv(s_fin))

        cn_lat = jnp.concatenate([_block_diag(state_mlstm_C[:, l]), _norm_block(state_mlstm_n[:, l])], axis=-1)
        m_lat = jnp.repeat(state_mlstm_m[:, l], HEAD_DIM, axis=-1)
        hm_f, hm_b, cn_fin, m_fin = _mlstm_scan(bgc, gtok, grow, ghead, cn_lat, m_lat[..., None],
                                                m_lat[:, :, None, :], n_ctx_tiles, tiles_per_seq)
        cm_l.append(_block_diag_inv(cn_fin[:, :, :, 0:HW]))
        nm_l.append(_block_diag_inv(cn_fin[:, :, :, HW:2 * HW], 1)[..., 0])
        mm_l.append(m_fin[:, :, ::HEAD_DIM, 0])

        outs = _merge_ffn(x_src, mod[l], a_ctx, a_lat, (od_f, od_b), (hm_f, hm_b), bgc, attn_subln_g[l],
                          delta_norm_g[l], mlstm_norm_g[l], w_out[l].astype(BF16), lam_init, norm2_g[l],
                          w_gate_up[l].astype(BF16), w_down[l].astype(BF16), final_norm_g, l == DEPTH - 1,
                          n_ctx_tiles, tiles_per_seq)
        x_src = (outs[0], outs[0], t_ctx // DENSE_TILE, t_rows)

    y_prompt = outs[0].reshape(batch, seq, D_MODEL)
    y_sample = outs[1].reshape(dec_batch, dec_seq, D_MODEL)
    new_k = jnp.stack(ks_l, axis=1).reshape(batch, DEPTH, seq, HEADS, 2, A_QK_DIM)
    new_v = jnp.stack(vs_l, axis=1).reshape(batch, DEPTH, seq, HEADS, A_V_DIM)
    return (y_prompt, y_sample, new_k, new_v, jnp.stack(sd_l, axis=1),
            jnp.stack(cm_l, axis=1), jnp.stack(nm_l, axis=1), jnp.stack(mm_l, axis=1))
```

```python
import functools
import math

import jax
import jax.numpy as jnp
from jax import lax
from jax.experimental import pallas as pl
from jax.experimental.pallas import tpu as pltpu

F32 = jnp.float32
BF16 = jnp.bfloat16

D_MODEL = 1024
DEPTH = 2
GRID_W = 64
N_DIR = 2
CHUNK = 64
ROPE_BASE = 10000.0
EPS = 1e-6
HEADS = 4
A_QK_DIM = 64
A_V_DIM = 128
HEAD_DIM = 64
CONV_K = 5
FFN_HIDDEN = 2816
ROW_TILE = 256
CHUNKS_PER_TILE = ROW_TILE // CHUNK
HW = HEADS * HEAD_DIM
LANES = 128
HALO = 8
BF16_ROWS = 16
GATE_GROUP = 32

N_PROJ = 3712
VMEM_LIMIT = 56 * 1024 * 1024


DENSE_TILE = 512


def _dense_tiles(n_ctx_tiles, tiles_per_seq):
    ratio = DENSE_TILE // ROW_TILE
    assert n_ctx_tiles % ratio == 0 and tiles_per_seq % ratio == 0
    return n_ctx_tiles // ratio, tiles_per_seq // ratio


def _cparams(sem):
    return pltpu.CompilerParams(dimension_semantics=sem, vmem_limit_bytes=VMEM_LIMIT)


def _bdot(a, b):
    return jnp.dot(a.astype(BF16), b.astype(BF16), preferred_element_type=F32)


def _bdot_nt(a, b):
    return lax.dot_general(a.astype(BF16), b.astype(BF16), (((1,), (1,)), ((), ())),
                           preferred_element_type=F32)


def _split3(x):
    hi = x.astype(BF16).astype(F32)
    r1 = x - hi
    mid = r1.astype(BF16).astype(F32)
    lo = (r1 - mid).astype(BF16).astype(F32)
    return hi, mid, lo


def _dot01(x, m01):
    d = functools.partial(jnp.dot, preferred_element_type=F32)
    hi, mid, lo = _split3(x)
    return d(hi.astype(BF16), m01) + d(mid.astype(BF16), m01) + d(lo.astype(BF16), m01)


def _group_sumsq(x, seg):
    sq = x * x
    out = []
    for t in range(x.shape[1] // LANES):
        tile = sq[:, t * LANES:(t + 1) * LANES]
        if seg == LANES:
            out.append(jnp.broadcast_to(jnp.sum(tile, axis=-1, keepdims=True), tile.shape))
        else:
            low = lax.broadcasted_iota(jnp.int32, tile.shape, 1) < seg
            s_lo = jnp.sum(jnp.where(low, tile, 0.0), axis=-1, keepdims=True)
            s_hi = jnp.sum(jnp.where(low, 0.0, tile), axis=-1, keepdims=True)
            out.append(jnp.where(low, s_lo, s_hi))
    return jnp.concatenate(out, axis=1)


def _sigmoid(x):
    return 1.0 / (1.0 + jnp.exp(-x))


def _softplus(x):
    return jnp.maximum(x, 0.0) + jnp.log1p(jnp.exp(-jnp.abs(x)))


def _ada_kernel(c_ref, w_ref, b_ref, o_ref):
    c = c_ref[...]
    s = c * _sigmoid(c)
    o_ref[...] = _bdot(s, w_ref[...]) + b_ref[...]


def _ada_mod(cvecs, w_mod, b_mod):
    n_out = w_mod.shape[-1]
    tn = 1024
    return pl.pallas_call(
        _ada_kernel,
        grid=(DEPTH, n_out // tn),
        in_specs=[pl.BlockSpec((8, D_MODEL), lambda l, j: (0, 0)),
                  pl.BlockSpec((None, D_MODEL, tn), lambda l, j: (l, 0, j)),
                  pl.BlockSpec((None, 1, tn), lambda l, j: (l, 0, j))],
        out_specs=pl.BlockSpec((None, 8, tn), lambda l, j: (l, 0, j)),
        out_shape=jax.ShapeDtypeStruct((DEPTH, 8, n_out), F32),
        compiler_params=_cparams(("parallel", "parallel")),
        name="ada_mod",
    )(cvecs, w_mod, b_mod.reshape(DEPTH, 1, n_out))


def _proj_kernel(xc_ref, xl_ref, mod_ref, g_ref, w_ref, b_ref, cos_ref, sin_ref,
                 qk_ref, vt_ref, kv32_ref, bqkv_ref, bgc_ref, gates_ref, *, n_ctx_tiles):
    x = jnp.where(pl.program_id(0) < n_ctx_tiles, xc_ref[...], xl_ref[...])
    y = x * lax.rsqrt(jnp.mean(x * x, axis=-1, keepdims=True) + EPS) * g_ref[...]
    u = y * (1.0 + mod_ref[:, D_MODEL:2 * D_MODEL]) + mod_ref[:, 0:D_MODEL]
    acc = _bdot(u, w_ref[...]) + b_ref[...]
    kv32_ref[...] = acc[:, 512:1536]
    vt_ref[...] = acc[:, 1024:1536].T.astype(BF16)
    bqkv_ref[...] = acc[:, 1536:2304]
    bgc_ref[...] = acc[:, 2304:3584]
    gates_ref[...] = acc[:, 3584:3712]
    cos = cos_ref[...]
    sin = sin_ref[...]
    even = (lax.broadcasted_iota(jnp.int32, cos.shape, 1) % 2) == 0
    for j in range(8):
        xj = acc[:, j * LANES:(j + 1) * LANES]
        swapped = jnp.where(even, pltpu.roll(xj, LANES - 1, 1), pltpu.roll(xj, 1, 1))
        r = xj * cos + swapped * sin
        if j < 4:
            r = r * (A_QK_DIM ** -0.5)
        qk_ref[:, j * LANES:(j + 1) * LANES] = r.astype(BF16)


def _x_specs(n_ctx_tiles, lat_tile0):
    return [pl.BlockSpec((DENSE_TILE, D_MODEL), lambda i: (jnp.minimum(i, n_ctx_tiles - 1), 0)),
            pl.BlockSpec((DENSE_TILE, D_MODEL), lambda i: (jnp.maximum(i - n_ctx_tiles, 0) + lat_tile0, 0))]


def _in_proj(x_ctx, x_lat, lat_tile0, t_rows, mod_l, g1, w_p, b_p, cos_t, sin_t, n_ctx_tiles, tiles_per_seq):
    n_tiles = t_rows // DENSE_TILE
    n_ctx_tiles, tiles_per_seq = _dense_tiles(n_ctx_tiles, tiles_per_seq)

    def mod_idx(i):
        return (jnp.where(i < n_ctx_tiles, 0, 1 + (i - n_ctx_tiles) // tiles_per_seq), 0, 0)

    def rope_idx(i):
        return (jnp.where(i < n_ctx_tiles, 0, 1 + (i - n_ctx_tiles) % tiles_per_seq), 0)

    row = lambda w: pl.BlockSpec((DENSE_TILE, w), lambda i: (i, 0))
    outs = [(1024, BF16, False), (512, BF16, True), (1024, F32, False), (768, F32, False), (1280, F32, False),
            (LANES, F32, False)]
    return pl.pallas_call(
        functools.partial(_proj_kernel, n_ctx_tiles=n_ctx_tiles),
        grid=(n_tiles,),
        in_specs=_x_specs(n_ctx_tiles, lat_tile0) + [
                  pl.BlockSpec((None, 1, 6 * D_MODEL), mod_idx),
                  pl.BlockSpec((1, D_MODEL), lambda i: (0, 0)),
                  pl.BlockSpec((D_MODEL, N_PROJ), lambda i: (0, 0)),
                  pl.BlockSpec((1, N_PROJ), lambda i: (0, 0)),
                  pl.BlockSpec((DENSE_TILE, LANES), rope_idx),
                  pl.BlockSpec((DENSE_TILE, LANES), rope_idx)],
        out_specs=[pl.BlockSpec((w, DENSE_TILE), lambda i: (0, i)) if tr else row(w) for w, _, tr in outs],
        out_shape=[jax.ShapeDtypeStruct((w, t_rows) if tr else (t_rows, w), dt) for w, dt, tr in outs],
        compiler_params=_cparams(("parallel",)),
        name="in_proj",
    )(x_ctx, x_lat, mod_l.reshape(8, 1, 6 * D_MODEL), g1.reshape(1, D_MODEL), w_p, b_p.reshape(1, N_PROJ),
      cos_t, sin_t)


GATE_ROWS = 80


def _gates_kernel(g_ref, par_ref, o_ref):
    x = g_ref[...]
    alog = par_ref[0:8, 0:1]
    dtb = par_ref[0:8, 1:2]
    fb = par_ref[24:32, 1:2]
    g = -jnp.exp(alog) * _softplus(x[0:8] + dtb)
    beta = _sigmoid(x[8:16])
    ig = x[16:24]
    lf = -_softplus(-(x[24:32] + fb))
    tl = x.shape[1]
    r = lax.broadcasted_iota(jnp.int32, (LANES, LANES), 0)
    c = lax.broadcasted_iota(jnp.int32, (LANES, LANES), 1)
    same = (r // CHUNK) == (c // CHUNK)
    pre = jnp.where(same & (r <= c), 1.0, 0.0).astype(BF16)
    suf = jnp.where(same & (r >= c), 1.0, 0.0).astype(BF16)
    tot = jnp.where(same, 1.0, 0.0).astype(BF16)
    m01 = jnp.concatenate([pre, suf, tot], axis=1)
    fwd16 = (lax.broadcasted_iota(jnp.int32, (16, LANES), 0) % 8) < 4
    fwd8 = lax.broadcasted_iota(jnp.int32, (8, LANES), 0) < 4
    pos = lax.broadcasted_iota(jnp.int32, (8, LANES), 1) % CHUNK

    def running_max(v):
        pf = sf = v
        s = 1
        while s < CHUNK:
            pf = jnp.maximum(pf, jnp.where(pos >= s, pltpu.roll(pf, s, 1), -jnp.inf))
            sf = jnp.maximum(sf, jnp.where(pos < CHUNK - s, pltpu.roll(sf, LANES - s, 1), -jnp.inf))
            s *= 2
        return pf, sf

    o_ref[16:24, :] = beta
    for j in range(tl // LANES):
        sl = slice(j * LANES, (j + 1) * LANES)
        xs = jnp.concatenate([g[:, sl], lf[:, sl]], axis=0)
        cs = _dot01(xs, m01)
        cum = jnp.where(fwd16, cs[:, 0:LANES], cs[:, LANES:2 * LANES])
        total = cs[:, 2 * LANES:3 * LANES]
        gc, gtot, b, btot = cum[0:8], total[0:8], cum[8:16], total[8:16]
        rr = ig[:, sl] - b
        kvl = btot - b + ig[:, sl]
        r_pf, r_sf = running_max(rr)
        k_pf, k_sf = running_max(kvl)
        o_ref[0:8, sl] = gc
        o_ref[8:16, sl] = gtot - gc
        o_ref[24:32, sl] = gtot
        o_ref[32:40, sl] = b
        o_ref[40:48, sl] = rr
        o_ref[48:56, sl] = jnp.where(fwd8, r_pf, r_sf)
        o_ref[56:64, sl] = kvl
        o_ref[64:72, sl] = btot
        o_ref[72:80, sl] = jnp.maximum(k_pf, k_sf)


def _gate_prep(gates_t, par):
    t_rows = gates_t.shape[1]
    tl = math.gcd(t_rows, 2048)
    return pl.pallas_call(
        _gates_kernel,
        grid=(t_rows // tl,),
        in_specs=[pl.BlockSpec((32, tl), lambda i: (0, i)),
                  pl.BlockSpec((32, LANES), lambda i: (0, 0))],
        out_specs=pl.BlockSpec((GATE_ROWS, tl), lambda i: (0, i)),
        out_shape=jax.ShapeDtypeStruct((GATE_ROWS, t_rows), F32),
        compiler_params=_cparams(("parallel",)),
        name="gate_prep",
    )(gates_t, par)


def _conv_kernel(x_ref, p_ref, n_ref, w_ref, o_ref, ext_s, *, n_ctx_tiles, tiles_per_seq):
    i = pl.program_id(0)
    j = (i - n_ctx_tiles) % tiles_per_seq
    is_ctx = i < n_ctx_tiles
    first = jnp.logical_or(is_ctx, j == 0)
    last = jnp.logical_or(is_ctx, j == tiles_per_seq - 1)
    ext_s[0:HALO, :] = jnp.where(first, 0.0, p_ref[...])
    ext_s[HALO:HALO + ROW_TILE, :] = x_ref[...]
    ext_s[HALO + ROW_TILE:, :] = jnp.where(last, 0.0, n_ref[...])
    w = w_ref[...]
    y = None
    for k in range(CONV_K):
        off = HALO - CONV_K // 2 + k
        term = ext_s[off:off + ROW_TILE, :] * w[k:k + 1, :]
        y = term if y is None else y + term
    y = y * _sigmoid(y)
    q = y[:, 0:HW]
    k_ = y[:, HW:2 * HW]
    o_ref[:, 0:HW] = q * lax.rsqrt(_group_sumsq(q, HEAD_DIM) + EPS) * (HEAD_DIM ** -0.5)
    o_ref[:, HW:2 * HW] = k_ * lax.rsqrt(_group_sumsq(k_, HEAD_DIM) + EPS)
    o_ref[:, 2 * HW:3 * HW] = y[:, 2 * HW:3 * HW]


def _delta_conv(bqkv, conv_w, n_ctx_tiles, tiles_per_seq):
    t_rows, width = bqkv.shape
    n_tiles = t_rows // ROW_TILE
    hb = ROW_TILE // HALO
    n_hb = t_rows // HALO
    return pl.pallas_call(
        functools.partial(_conv_kernel, n_ctx_tiles=n_ctx_tiles, tiles_per_seq=tiles_per_seq),
        grid=(n_tiles,),
        in_specs=[pl.BlockSpec((ROW_TILE, width), lambda i: (i, 0)),
                  pl.BlockSpec((HALO, width), lambda i: (jnp.maximum(i * hb - 1, 0), 0)),
                  pl.BlockSpec((HALO, width), lambda i: (jnp.minimum((i + 1) * hb, n_hb - 1), 0)),
                  pl.BlockSpec((CONV_K, width), lambda i: (0, 0))],
        out_specs=pl.BlockSpec((ROW_TILE, width), lambda i: (i, 0)),
        out_shape=jax.ShapeDtypeStruct((t_rows, width), F32),
        scratch_shapes=[pltpu.VMEM((ROW_TILE + 2 * HALO, width), F32)],
        compiler_params=_cparams(("parallel",)),
        name="delta_conv",
    )(bqkv, bqkv, bqkv, conv_w)


def _attn_kernel(lqk_ref, q_ref, *refs, lam_init, tq, tk):
    o_ref = refs[-1]
    lq = lqk_ref[...]
    lam = (jnp.exp(jnp.sum(lq[0:1] * lq[1:2], axis=-1, keepdims=True))
           - jnp.exp(jnp.sum(lq[2:3] * lq[3:4], axis=-1, keepdims=True)) + lam_init)
    tiles = [(k_ref, vt_ref, j) for k_ref, vt_ref in zip(refs[0:-1:2], refs[1:-1:2])
             for j in range(k_ref.shape[0] // tk)]
    n_kv = len(tiles)
    ones = jnp.ones((BF16_ROWS, tk), BF16)
    for hh in range(q_ref.shape[1] // LANES):
        hs = slice(hh * LANES, (hh + 1) * LANES)
        q = q_ref[:, hs]
        lane = lax.broadcasted_iota(jnp.int32, q.shape, 1)
        zero = jnp.zeros_like(q)
        q2 = jnp.concatenate([jnp.where(lane < A_QK_DIM, q, zero), jnp.where(lane >= A_QK_DIM, q, zero)],
                             axis=0)

        def scores(t):
            k_ref, _, j = tiles[t]
            return lax.dot_general(k_ref[j * tk:(j + 1) * tk, hs], q2, (((1,), (1,)), ((), ())),
                                   preferred_element_type=F32)

        m = acc = None
        st_next = scores(0)
        for j in range(n_kv):
            st = st_next
            if j + 1 < n_kv:
                st_next = scores(j + 1)
            m_new = jnp.max(st, axis=0, keepdims=True)
            if j > 0:
                m_new = jnp.maximum(m, m_new)
            p = jnp.exp(st - m_new)
            _, vt_ref, jt = tiles[j]
            vt = jnp.concatenate([vt_ref[hs, jt * tk:(jt + 1) * tk], ones], axis=0)
            pv = jnp.dot(vt, p.astype(BF16), preferred_element_type=F32)
            acc = pv if j == 0 else jnp.exp(m - m_new) * acc + pv
            m = m_new
        o = acc[0:A_V_DIM] / acc[A_V_DIM:A_V_DIM + 1]
        o_ref[:, hs] = (o[:, 0:tq] - lam * o[:, tq:2 * tq]).T


def _diff_attention(lqk, qk, vt, cache_k, cache_v, lam_init, q_row0, n_seq, lq, tq, tk):
    assert q_row0 % lq == 0
    qb0 = q_row0 // tq
    sb0 = q_row0 // lq
    nq = lq // tq
    hb = HEADS if lq <= ROW_TILE else 1
    n_hblk = HEADS // hb
    in_specs = [pl.BlockSpec((4, A_QK_DIM), lambda b, h, i: (0, 0)),
                pl.BlockSpec((tq, hb * LANES), lambda b, h, i: (qb0 + b * nq + i, h)),
                pl.BlockSpec((lq, hb * LANES), lambda b, h, i: (sb0 + b, n_hblk + h)),
                pl.BlockSpec((hb * A_V_DIM, lq), lambda b, h, i: (h, sb0 + b))]
    args = [lqk, qk, qk, vt]
    if cache_k is not None:
        past = cache_k.shape[1]
        in_specs += [pl.BlockSpec((None, past, hb * LANES), lambda b, h, i: (b, 0, h)),
                     pl.BlockSpec((None, hb * A_V_DIM, past), lambda b, h, i: (b, h, 0))]
        args += [cache_k.astype(BF16), cache_v.astype(BF16).transpose(0, 2, 1)]
    return pl.pallas_call(
        functools.partial(_attn_kernel, lam_init=lam_init, tq=tq, tk=tk),
        grid=(n_seq, n_hblk, nq),
        in_specs=in_specs,
        out_specs=pl.BlockSpec((tq, hb * LANES), lambda b, h, i: (b * nq + i, h)),
        out_shape=jax.ShapeDtypeStruct((n_seq * lq, HEADS * A_V_DIM), F32),
        compiler_params=_cparams(("parallel", "parallel", "parallel")),
        name="diff_attention",
    )(*args)


def _same_head(n_cols=HW):
    r = lax.broadcasted_iota(jnp.int32, (HW, n_cols), 0) // HEAD_DIM
    c = (lax.broadcasted_iota(jnp.int32, (HW, n_cols), 1) % HW) // HEAD_DIM
    return r == c


def _bd(x, same):
    return jnp.where(same, jnp.concatenate([x, x, x, x], axis=0), 0.0)


def _cat_index():
    i = lax.broadcasted_iota(jnp.int32, (CHUNK, HW), 0)
    j = lax.broadcasted_iota(jnp.int32, (CHUNK, HW), 1) % CHUNK
    return i, j


def _scan_masks(direction):
    i, j = _cat_index()
    diff = (i - j) * (1 - 2 * direction)
    return diff >= 0, diff > 0


def _expand_matrix(first_lane, n_q):
    r = lax.broadcasted_iota(jnp.int32, (LANES, n_q * HW), 0)
    c = lax.broadcasted_iota(jnp.int32, (LANES, n_q * HW), 1)
    s = r % GATE_GROUP - first_lane
    hit = (r < 3 * GATE_GROUP) & (s >= 0) & (s < 4 * n_q) & (s // HEADS == c // HW) \
        & (s % HEADS == (c % HW) // HEAD_DIM)
    return jnp.where(hit, 1.0, 0.0).astype(BF16)


def _expand(tok, e3):
    hi, mid, lo = _split3(tok)
    x3 = hi + pltpu.roll(mid, GATE_GROUP, 1) + pltpu.roll(lo, 2 * GATE_GROUP, 1)
    return jnp.dot(x3.astype(BF16), e3, preferred_element_type=F32)


def _head_rows(hd, lane):
    return jnp.concatenate([jnp.broadcast_to(hd[h:h + 1, lane:lane + 1], (HEAD_DIM, 1)) for h in range(HEADS)],
                           axis=0)


def _pad_rows(x):
    return jnp.concatenate([x, jnp.zeros_like(x)], axis=0)


def _unit_tri_inverse(mats, same):
    i, j = _cat_index()
    b16 = (i // 16) == (j // 16)
    b32 = (i // 32) == (j // 32)
    eye = jnp.where(i == j, 1.0, 0.0)
    xs = [jnp.where(b16, a, 0.0) for a in mats]
    ts = [eye - x for x in xs]
    xbd = [_bd(x, same) for x in xs]
    for _ in range(3):
        xs = [_bdot(x, b) for x, b in zip(xs, xbd)]
        xbd = [_bd(x, same) for x in xs]
        ts = [t + _bdot(t, b) for t, b in zip(ts, xbd)]
    for inside in (jnp.logical_and(b32, jnp.logical_not(b16)), jnp.logical_not(b32)):
        mids = [_bdot(t, _bd(jnp.where(inside, a, 0.0), same)) for t, a in zip(ts, mats)]
        ts = [t - _bdot(m, _bd(t, same)) for t, m in zip(ts, mids)]
    return ts


def _scan_job(n_ctx_tiles, tiles_per_seq):
    j = pl.program_id(0)
    is_ctx = j < n_ctx_tiles
    return jnp.logical_or(is_ctx, (j - n_ctx_tiles) % tiles_per_seq == 0), is_ctx


def _chunks_of(refs_f, refs_b):
    out = []
    for d, (q, k, v, tok, rows, heads) in enumerate((refs_f, refs_b)):
        for c in range(CHUNKS_PER_TILE):
            sl = slice(c * CHUNK, (c + 1) * CHUNK)
            out.append((q[sl, :], k[sl, :], v[sl, :], tok[sl, :], rows[c], heads[c], d))
    return out


def _delta_prep(chunks):
    same = _same_head()
    masks = [_scan_masks(0), _scan_masks(1)]
    e3 = _expand_matrix(0, 3)
    n = len(chunks)
    ex = [_expand(ch[3], e3) for ch in chunks]
    gc = [e[:, 0:HW] for e in ex]
    kb = [chunks[i][1] * ex[i][:, HW:2 * HW] for i in range(n)]
    kk = [_bdot_nt(jnp.concatenate([kb[i], chunks[i][0]], axis=0), _bd(chunks[i][1], same)) for i in range(n)]
    e_incl = [jnp.exp(jnp.where(masks[chunks[i][6]][0], gc[i] - chunks[i][4][0:1, :], -jnp.inf)) for i in range(n)]
    a = [jnp.where(masks[chunks[i][6]][1], kk[i][0:CHUNK] * e_incl[i], 0.0) for i in range(n)]
    aqk = [kk[i][CHUNK:2 * CHUNK] * e_incl[i] for i in range(n)]
    tinv = _unit_tri_inverse(a, same)
    egc = [jnp.exp(g) for g in gc]
    rhs = [jnp.concatenate([_bd(chunks[i][2] * ex[i][:, HW:2 * HW], same), _bd(kb[i] * egc[i], same)], axis=1)
           for i in range(n)]
    uw = [_bdot(tinv[i], rhs[i]) for i in range(n)]
    kd_t = [_pad_rows(chunks[i][1] * jnp.exp(ex[i][:, 2 * HW:3 * HW])).T for i in range(n)]
    gl = [jnp.exp(_head_rows(chunks[i][5][0:HEADS], CHUNK)) for i in range(n)]
    return [(uw[i][:, 0:HW], uw[i][:, HW:2 * HW], chunks[i][0] * egc[i], kd_t[i], aqk[i], gl[i])
            for i in range(n)]


def _delta_step(pre, s_ref, o_ref, c, same):
    u, w, qd, kd_t, aqk, gl = pre
    s = s_ref[...]
    wq = _bdot(jnp.concatenate([w, qd], axis=0), s)
    v_new = u - wq[0:CHUNK]
    o_ref[c * CHUNK:(c + 1) * CHUNK, :] = wq[CHUNK:2 * CHUNK] + _bdot(aqk, _bd(v_new, same))
    s_ref[...] = s * gl + jnp.where(same, _bdot(kd_t, _pad_rows(v_new)), 0.0)


def _delta_kernel(qf, kf, vf, qb, kb, vb, tokf, rowf, headf, tokb, rowb, headb, s0_ref,
                  of_ref, ob_ref, sout_ref, sf_s, sb_s, *, n_ctx_tiles, tiles_per_seq):
    first, is_ctx = _scan_job(n_ctx_tiles, tiles_per_seq)

    @pl.when(first)
    def _():
        sf_s[...] = jnp.where(is_ctx, 0.0, s0_ref[0])
        sb_s[...] = jnp.where(is_ctx, 0.0, s0_ref[1])

    pre = _delta_prep(_chunks_of((qf, kf, vf, tokf, rowf, headf), (qb, kb, vb, tokb, rowb, headb)))
    same = _same_head()
    for i in range(CHUNKS_PER_TILE):
        cb = CHUNKS_PER_TILE - 1 - i
        _delta_step(pre[i], sf_s, of_ref, i, same)
        _delta_step(pre[CHUNKS_PER_TILE + cb], sb_s, ob_ref, cb, same)

    @pl.when(is_ctx)
    def _():
        sout_ref[0] = sf_s[...]
        sout_ref[1] = sb_s[...]


def _mlstm_prep(chunks):
    same = _same_head()
    masks = [_scan_masks(0), _scan_masks(1)]
    e3 = _expand_matrix(12, 2)
    n = len(chunks)
    ex = [_expand(ch[3], e3) for ch in chunks]
    b = [e[:, 0:HW] for e in ex]
    ks = [ch[1] * (HEAD_DIM ** -0.5) for ch in chunks]
    qk = [_bdot_nt(chunks[i][0], _bd(ks[i], same)) for i in range(n)]
    dmat = [jnp.where(masks[chunks[i][6]][0], b[i] + chunks[i][4][1:2, :], -jnp.inf) for i in range(n)]
    dmax = [ex[i][:, 0:HW] + ex[i][:, HW:2 * HW] for i in range(n)]
    k_t = [_pad_rows(k).T for k in ks]
    ones = jnp.ones((CHUNK, HW), F32)
    vaug = [_pad_rows(jnp.concatenate([ch[2], ones], axis=1)) for ch in chunks]
    vbd = [jnp.concatenate([_bd(ch[2], same), jnp.where(same, 1.0, 0.0)], axis=1) for ch in chunks]
    kvl = [jnp.concatenate([jnp.broadcast_to(ch[5][HEADS + h:HEADS + h + 1, :], (HEAD_DIM, LANES))
                            for h in range(HEADS)], axis=0) for ch in chunks]
    btot_col = [_head_rows(ch[5][HEADS:2 * HEADS], CHUNK) for ch in chunks]
    mkv_col = [_head_rows(ch[5][HEADS:2 * HEADS], CHUNK + 1) for ch in chunks]
    return [(chunks[i][0], k_t[i], vaug[i], vbd[i], b[i], dmat[i], dmax[i], qk[i], kvl[i], btot_col[i],
             mkv_col[i], chunks[i][4][2:3, :], chunks[i][4][3:4, :]) for i in range(n)]


def _mlstm_step(pre, cn_ref, mcol_ref, mrow_ref, o_ref, c, same2):
    q, k_t, vaug, vbd, b, dmat, dmax, qk, kvl, btot_col, mkv_col, btot_row, mkv_row = pre
    cn = cn_ref[...]
    ms_col = mcol_ref[...]
    ms_row = mrow_ref[...]
    inter = b + ms_row
    m_t = jnp.maximum(inter, dmax)
    s = qk * jnp.exp(dmat - m_t)
    w_inter = jnp.exp(inter - m_t)
    nd = jnp.concatenate([w_inter, w_inter], axis=1) * _bdot(q, cn) + _bdot(s, vbd)
    o_ref[c * CHUNK:(c + 1) * CHUNK, :] = nd[:, 0:HW] / jnp.maximum(jnp.abs(nd[:, HW:2 * HW]), jnp.exp(-m_t))
    m_new = jnp.maximum(btot_col + ms_col, mkv_col)
    lane = lax.broadcasted_iota(jnp.int32, kvl.shape, 1)
    wk = jnp.where(lane < CHUNK, jnp.exp(kvl - m_new), 0.0)
    cn_ref[...] = jnp.exp(btot_col + ms_col - m_new) * cn + jnp.where(same2, _bdot(k_t * wk, vaug), 0.0)
    mcol_ref[...] = m_new
    mrow_ref[...] = jnp.maximum(btot_row + ms_row, mkv_row)


def _mlstm_kernel(qf, kf, vf, qb, kb, vb, tokf, rowf, headf, tokb, rowb, headb, cn0_ref, mc0_ref, mr0_ref,
                  of_ref, ob_ref, cnout_ref, mout_ref, cnf_s, cnb_s, mcf_s, mcb_s, mrf_s, mrb_s,
                  *, n_ctx_tiles, tiles_per_seq):
    first, is_ctx = _scan_job(n_ctx_tiles, tiles_per_seq)

    @pl.when(first)
    def _():
        cnf_s[...] = jnp.where(is_ctx, 0.0, cn0_ref[0])
        cnb_s[...] = jnp.where(is_ctx, 0.0, cn0_ref[1])
        mcf_s[...] = jnp.where(is_ctx, 0.0, mc0_ref[0])
        mcb_s[...] = jnp.where(is_ctx, 0.0, mc0_ref[1])
        mrf_s[...] = jnp.where(is_ctx, 0.0, mr0_ref[0])
        mrb_s[...] = jnp.where(is_ctx, 0.0, mr0_ref[1])

    pre = _mlstm_prep(_chunks_of((qf, kf, vf, tokf, rowf, headf), (qb, kb, vb, tokb, rowb, headb)))
    same2 = _same_head(2 * HW)
    for i in range(CHUNKS_PER_TILE):
        cb = CHUNKS_PER_TILE - 1 - i
        _mlstm_step(pre[i], cnf_s, mcf_s, mrf_s, of_ref, i, same2)
        _mlstm_step(pre[CHUNKS_PER_TILE + cb], cnb_s, mcb_s, mrb_s, ob_ref, cb, same2)

    @pl.when(is_ctx)
    def _():
        cnout_ref[0] = cnf_s[...]
        cnout_ref[1] = cnb_s[...]
        mout_ref[0] = jnp.broadcast_to(mcf_s[...], (HW, LANES))
        mout_ref[1] = jnp.broadcast_to(mcb_s[...], (HW, LANES))


def _scan_call(kernel_fn, name, src, col_blocks, gtok, grow, ghead, states, state_out_shapes, scratch,
               n_ctx_tiles, tiles_per_seq):
    t_rows = src.shape[0]
    n_tiles = t_rows // ROW_TILE

    def seq_of(j):
        return jnp.where(j < n_ctx_tiles, j, n_ctx_tiles + (j - n_ctx_tiles) // tiles_per_seq)

    def back(j):
        jj = j - n_ctx_tiles
        mirrored = n_ctx_tiles + (jj // tiles_per_seq) * tiles_per_seq + tiles_per_seq - 1 - jj % tiles_per_seq
        return jnp.where(j < n_ctx_tiles, j, mirrored)

    def tile_spec(cb, bwd):
        return pl.BlockSpec((ROW_TILE, HW), (lambda j: (back(j), cb)) if bwd else (lambda j: (j, cb)))

    def gate_specs(d):
        tile = (lambda j: back(j)) if d else (lambda j: j)
        return [pl.BlockSpec((None, ROW_TILE, LANES), lambda j: (d, tile(j), 0)),
                pl.BlockSpec((None, CHUNKS_PER_TILE, 8, HW), lambda j: (d, tile(j), 0, 0)),
                pl.BlockSpec((None, CHUNKS_PER_TILE, 8, LANES), lambda j: (d, tile(j), 0, 0))]

    def state_in_spec(shape):
        return pl.BlockSpec((None,) + tuple(shape),
                            lambda j: (jnp.maximum(seq_of(j) - n_ctx_tiles, 0),) + (0,) * len(shape))

    def state_out_spec(shape):
        return pl.BlockSpec((None,) + tuple(shape),
                            lambda j: (jnp.minimum(j, n_ctx_tiles - 1),) + (0,) * len(shape))

    in_specs = ([tile_spec(cb, False) for cb in col_blocks] + [tile_spec(cb, True) for cb in col_blocks]
                + gate_specs(0) + gate_specs(1) + [state_in_spec(s.shape[1:]) for s in states])
    out_specs = ([pl.BlockSpec((ROW_TILE, HW), lambda j: (j, 0)),
                  pl.BlockSpec((ROW_TILE, HW), lambda j: (back(j), 0))]
                 + [state_out_spec(s[1:]) for s in state_out_shapes])
    out_shape = ([jax.ShapeDtypeStruct((t_rows, HW), F32)] * 2
                 + [jax.ShapeDtypeStruct(tuple(s), F32) for s in state_out_shapes])
    return pl.pallas_call(
        functools.partial(kernel_fn, n_ctx_tiles=n_ctx_tiles, tiles_per_seq=tiles_per_seq),
        grid=(n_tiles,),
        in_specs=in_specs, out_specs=out_specs, out_shape=out_shape,
        scratch_shapes=scratch,
        compiler_params=_cparams(("arbitrary",)),
        name=name,
    )(*([src] * (2 * len(col_blocks))), gtok, grow, ghead, gtok, grow, ghead, *states)


def _delta_scan(dqkv, gtok, grow, ghead, s0, n_ctx_tiles, tiles_per_seq):
    return _scan_call(_delta_kernel, "delta_scan", dqkv, (0, 1, 2), gtok, grow, ghead, [s0],
                      [(n_ctx_tiles,) + s0.shape[1:]],
                      [pltpu.VMEM((HW, HW), F32)] * 2, n_ctx_tiles, tiles_per_seq)


def _mlstm_scan(bgc, gtok, grow, ghead, cn0, mc0, mr0, n_ctx_tiles, tiles_per_seq):
    return _scan_call(_mlstm_kernel, "mlstm_scan", bgc, (1, 2, 3), gtok, grow, ghead, [cn0, mc0, mr0],
                      [(n_ctx_tiles,) + cn0.shape[1:], (n_ctx_tiles, N_DIR, HW, LANES)],
                      [pltpu.VMEM((HW, 2 * HW), F32)] * 2 + [pltpu.VMEM((HW, 1), F32)] * 2
                      + [pltpu.VMEM((1, HW), F32)] * 2, n_ctx_tiles, tiles_per_seq)


def _merge_ffn_kernel(xc_ref, xl_ref, mod_ref, actx_ref, alat_ref, odf_ref, odb_ref, hmf_ref, hmb_ref, bg_ref,
                      co_ref, ag_ref, dg_ref, mg_ref, w_ref, g2_ref, wgu_ref, wdn_ref, fg_ref, *o_refs,
                      lam_init, n_ctx_tiles, final):
    is_ctx = pl.program_id(0) < n_ctx_tiles
    a = jnp.where(is_ctx, actx_ref[...], alat_ref[...])
    a = a * lax.rsqrt(_group_sumsq(a, A_V_DIM) * (1.0 / A_V_DIM) + EPS)
    a = a * ag_ref[...] * (1.0 - lam_init)
    od = odf_ref[...] + odb_ref[...]
    bg = bg_ref[...]
    d = od * lax.rsqrt(_group_sumsq(od, HEAD_DIM) * (1.0 / HEAD_DIM) + EPS) * dg_ref[...] * (bg * _sigmoid(bg))
    hm = hmf_ref[...] + hmb_ref[...]
    m = _sigmoid(co_ref[...]) * (hm * lax.rsqrt(_group_sumsq(hm, HEAD_DIM) * (1.0 / HEAD_DIM) + EPS) * mg_ref[...])
    cat = jnp.concatenate([a.astype(BF16), d.astype(BF16), m.astype(BF16)], axis=1)
    mix = jnp.dot(cat, w_ref[...], preferred_element_type=F32)
    x = jnp.where(is_ctx, xc_ref[...], xl_ref[...]) + mod_ref[:, 2 * D_MODEL:3 * D_MODEL] * mix
    y = x * lax.rsqrt(jnp.mean(x * x, axis=-1, keepdims=True) + EPS) * g2_ref[...]
    u = y * (1.0 + mod_ref[:, 4 * D_MODEL:5 * D_MODEL]) + mod_ref[:, 3 * D_MODEL:4 * D_MODEL]
    h = jnp.dot(u.astype(BF16), wgu_ref[...], preferred_element_type=F32)
    gate = h[:, 0:FFN_HIDDEN]
    act = (gate * _sigmoid(gate)) * h[:, FFN_HIDDEN:2 * FFN_HIDDEN]
    out = x + mod_ref[:, 5 * D_MODEL:6 * D_MODEL] * jnp.dot(act.astype(BF16), wdn_ref[...],
                                                             preferred_element_type=F32)
    if not final:
        o_refs[0][...] = out
        return
    out = out * lax.rsqrt(jnp.mean(out * out, axis=-1, keepdims=True) + EPS) * fg_ref[...]
    yc_ref, yl_ref = o_refs
    yl_ref[...] = out

    @pl.when(is_ctx)
    def _():
        yc_ref[...] = out


def _merge_ffn(x_all, mod_l, a_ctx, a_lat, od, hm, bgc, attn_g, delta_g, mlstm_g, w_out, lam_init,
               g2, w_gu, w_dn, final_g, final, n_ctx_tiles, tiles_per_seq):
    x_ctx, x_lat, lat_tile0, t_rows = x_all
    n_ctx_tiles, tiles_per_seq = _dense_tiles(n_ctx_tiles, tiles_per_seq)

    def mod_idx(i):
        return (jnp.where(i < n_ctx_tiles, 0, 1 + (i - n_ctx_tiles) // tiles_per_seq), 0, 0)

    full = lambda r, c: pl.BlockSpec((r, c), lambda i: (0, 0), pipeline_mode=pl.Buffered(1))
    row = lambda w, cb=0: pl.BlockSpec((DENSE_TILE, w), lambda i: (i, cb))
    a_w = HEADS * A_V_DIM
    if final:
        t_ctx = n_ctx_tiles * DENSE_TILE
        out_specs = [pl.BlockSpec((DENSE_TILE, D_MODEL), lambda i: (jnp.minimum(i, n_ctx_tiles - 1), 0)),
                     pl.BlockSpec((DENSE_TILE, D_MODEL), lambda i: (jnp.maximum(i - n_ctx_tiles, 0), 0))]
        out_shape = [jax.ShapeDtypeStruct((t_ctx, D_MODEL), F32),
                     jax.ShapeDtypeStruct((t_rows - t_ctx, D_MODEL), F32)]
    else:
        out_specs = [row(D_MODEL)]
        out_shape = [jax.ShapeDtypeStruct((t_rows, D_MODEL), F32)]
    return pl.pallas_call(
        functools.partial(_merge_ffn_kernel, lam_init=lam_init, n_ctx_tiles=n_ctx_tiles, final=final),
        grid=(t_rows // DENSE_TILE,),
        in_specs=_x_specs(n_ctx_tiles, lat_tile0) + [
                  pl.BlockSpec((None, 1, 6 * D_MODEL), mod_idx),
                  pl.BlockSpec((DENSE_TILE, a_w), lambda i: (jnp.minimum(i, n_ctx_tiles - 1), 0)),
                  pl.BlockSpec((DENSE_TILE, a_w), lambda i: (jnp.maximum(i - n_ctx_tiles, 0), 0)),
                  row(HW), row(HW), row(HW), row(HW), row(HW), row(HW, 4),
                  full(1, a_w), full(1, HW), full(1, HW), full(D_MODEL, D_MODEL),
                  full(1, D_MODEL), full(D_MODEL, 2 * FFN_HIDDEN), full(FFN_HIDDEN, D_MODEL),
                  full(1, D_MODEL)],
        out_specs=out_specs,
        out_shape=out_shape,
        compiler_params=_cparams(("arbitrary",)),
        name="merge_ffn",
    )(x_ctx, x_lat, mod_l.reshape(8, 1, 6 * D_MODEL), a_ctx, a_lat, od[0], od[1], hm[0], hm[1], bgc, bgc,
      jnp.tile(attn_g, HEADS).reshape(1, -1), jnp.tile(delta_g, HEADS).reshape(1, -1),
      jnp.tile(mlstm_g, HEADS).reshape(1, -1), w_out, g2.reshape(1, D_MODEL), w_gu, w_dn,
      final_g.reshape(1, D_MODEL))


def _rope_tables(dec_seq):
    n_rows = dec_seq // GRID_W
    rows = jnp.repeat(jnp.arange(n_rows, dtype=F32), GRID_W)
    cols = jnp.tile(jnp.arange(GRID_W, dtype=F32), n_rows)
    n_freq = A_QK_DIM // 4
    inv_freq = ROPE_BASE ** (-jnp.arange(n_freq, dtype=F32) / n_freq)
    ang = jnp.concatenate([rows[:, None] * inv_freq, cols[:, None] * inv_freq], axis=-1)
    cos = jnp.repeat(jnp.cos(ang), 2, axis=-1)
    sin = jnp.repeat(jnp.sin(ang), 2, axis=-1) * jnp.tile(jnp.array([-1.0, 1.0], F32), A_QK_DIM // 2)
    cos = jnp.concatenate([jnp.ones((DENSE_TILE, A_QK_DIM), F32), cos], axis=0)
    sin = jnp.concatenate([jnp.zeros((DENSE_TILE, A_QK_DIM), F32), sin], axis=0)
    return jnp.tile(cos, (1, 2)), jnp.tile(sin, (1, 2))


def _permute_proj(w_in_l, b_in_l):
    def perm(a):
        head, ba_bb, tail, ci_cf = a[..., 0:2560], a[..., 2560:2576], a[..., 2576:3600], a[..., 3600:3616]
        pad = jnp.zeros(a.shape[:-1] + (N_PROJ - 3616,), a.dtype)
        return jnp.concatenate([head, tail, ba_bb, ci_cf, pad], axis=-1)
    return perm(w_in_l).astype(BF16), perm(b_in_l)


def _gate_layouts(grow80, n_chunks):
    r = grow80.reshape(10, N_DIR, HEADS, n_chunks, CHUNK)
    tok = r[jnp.array([0, 2, 1, 4, 6])].transpose(1, 3, 4, 0, 2).reshape(N_DIR, n_chunks * CHUNK, 5 * HEADS)
    gtok = jnp.pad(tok, ((0, 0), (0, 0), (0, LANES - 5 * HEADS)))
    rows = r[jnp.array([0, 5, 8, 9])].transpose(1, 3, 0, 2, 4).reshape(N_DIR, n_chunks, 4, HW)
    grow = jnp.pad(rows, ((0, 0), (0, 0), (0, 4), (0, 0)))
    per_head = lambda q: r[q, :, :, :, 0].transpose(0, 2, 1)[..., None]
    zeros = lambda w: jnp.zeros((N_DIR, n_chunks, HEADS, w), F32)
    delta_rows = jnp.concatenate([zeros(CHUNK), per_head(3), zeros(LANES - CHUNK - 1)], axis=-1)
    mlstm_rows = jnp.concatenate([r[7].transpose(0, 2, 1, 3), per_head(8), per_head(9),
                                  zeros(LANES - CHUNK - 2)], axis=-1)
    return gtok, grow, jnp.concatenate([delta_rows, mlstm_rows], axis=2)


def _block_diag(s):
    eye = jnp.eye(HEADS, dtype=s.dtype)
    out = s[..., :, :, None, :] * eye[:, None, :, None]
    return out.reshape(s.shape[:-3] + (HW, HW))


def _block_diag_inv(s_bd, n_cols=HEAD_DIM):
    blk = lambda h: s_bd[..., h * HEAD_DIM:(h + 1) * HEAD_DIM, h * HEAD_DIM:h * HEAD_DIM + n_cols]
    return jnp.stack([blk(h) for h in range(HEADS)], axis=-3)


def _norm_block(n):
    return _block_diag(jnp.broadcast_to(n[..., None], n.shape + (HEAD_DIM,)))


def kernel(x_prompt, x_sample, cache_attn_k, cache_attn_v, state_delta, state_mlstm_C, state_mlstm_n,
           state_mlstm_m, c, c_ctx, norm1_g, norm2_g, w_mod, b_mod, w_in, b_in, w_out, lambda_qk,
           attn_subln_g, delta_conv_w, delta_A_log, delta_dt_bias, delta_norm_g, mlstm_f_bias,
           mlstm_norm_g, w_gate_up, w_down, final_norm_g):
    batch, seq, _ = x_prompt.shape
    dec_batch, dec_seq, _ = x_sample.shape
    past_len = cache_attn_k.shape[2]
    assert seq == ROW_TILE and dec_seq % ROW_TILE == 0 and dec_batch + 1 <= 8
    t_ctx = batch * seq
    t_lat = dec_batch * dec_seq
    t_rows = t_ctx + t_lat
    n_ctx_tiles = t_ctx // ROW_TILE
    tiles_per_seq = dec_seq // ROW_TILE
    n_chunks = t_rows // CHUNK

    x_src = (x_prompt.reshape(t_ctx, D_MODEL), x_sample.reshape(t_lat, D_MODEL), 0, t_rows)
    cvecs = jnp.zeros((8, D_MODEL), F32).at[0].set(c_ctx).at[1:1 + dec_batch].set(c)
    mod = _ada_mod(cvecs, w_mod, b_mod)
    cos_t, sin_t = _rope_tables(dec_seq)

    ks_l, vs_l, sd_l, cm_l, nm_l, mm_l = [], [], [], [], [], []
    for l in range(DEPTH):
        lam_init = 0.8 - 0.6 * math.exp(-0.3 * l)
        w_p, b_p = _permute_proj(w_in[l], b_in[l])
        qk, vt, kv32, bqkv, bgc, gates = _in_proj(*x_src, mod[l], norm1_g[l], w_p, b_p, cos_t, sin_t,
                                                 n_ctx_tiles, tiles_per_seq)
        ks_l.append(kv32[:t_ctx, 0:512].reshape(batch, seq, 512))
        vs_l.append(kv32[:t_ctx, 512:1024].reshape(batch, seq, 512))

        par = jnp.zeros((32, LANES), F32)
        par = par.at[0:8, 0].set(delta_A_log[l].reshape(-1)).at[0:8, 1].set(delta_dt_bias[l].reshape(-1))
        par = par.at[24:32, 1].set(mlstm_f_bias[l].reshape(-1))
        gtok, grow, ghead = _gate_layouts(_gate_prep(gates[:, 0:32].T, par), n_chunks)

        a_ctx = _diff_attention(lambda_qk[l], qk, vt, None, None, lam_init, 0, batch, seq, seq, seq)
        a_lat = _diff_attention(lambda_qk[l], qk, vt, cache_attn_k[:, l].reshape(dec_batch, past_len, 512),
                                cache_attn_v[:, l].reshape(dec_batch, past_len, 512), lam_init, t_ctx,
                                dec_batch, dec_seq, min(2048, dec_seq), 512)

        dqkv = _delta_conv(bqkv, delta_conv_w[l], n_ctx_tiles, tiles_per_seq)
        od_f, od_b, s_fin = _delta_scan(dqkv, gtok, grow, ghead, _block_diag(state_delta[:, l]),
                                        n_ctx_tiles, tiles_per_seq)
        sd_l.append(_block_diag_inv(s_fin))

        cn_lat = jnp.concatenate([_block_diag(state_mlstm_C[:, l]), _norm_block(state_mlstm_n[:, l])], axis=-1)
        m_lat = jnp.repeat(state_mlstm_m[:, l], HEAD_DIM, axis=-1)
        hm_f, hm_b, cn_fin, m_fin = _mlstm_scan(bgc, gtok, grow, ghead, cn_lat, m_lat[..., None],
                                                m_lat[:, :, None, :], n_ctx_tiles, tiles_per_seq)
        cm_l.append(_block_diag_inv(cn_fin[:, :, :, 0:HW]))
        nm_l.append(_block_diag_inv(cn_fin[:, :, :, HW:2 * HW], 1)[..., 0])
        mm_l.append(m_fin[:, :, ::HEAD_DIM, 0])

        outs = _merge_ffn(x_src, mod[l], a_ctx, a_lat, (od_f, od_b), (hm_f, hm_b), bgc, attn_subln_g[l],
                          delta_norm_g[l], mlstm_norm_g[l], w_out[l].astype(BF16), lam_init, norm2_g[l],
                          w_gate_up[l].astype(BF16), w_down[l].astype(BF16), final_norm_g, l == DEPTH - 1,
                          n_ctx_tiles, tiles_per_seq)
        x_src = (outs[0], outs[0], t_ctx // DENSE_TILE, t_rows)

    y_prompt = outs[0].reshape(batch, seq, D_MODEL)
    y_sample = outs[1].reshape(dec_batch, dec_seq, D_MODEL)
    new_k = jnp.stack(ks_l, axis=1).reshape(batch, DEPTH, seq, HEADS, 2, A_QK_DIM)
    new_v = jnp.stack(vs_l, axis=1).reshape(batch, DEPTH, seq, HEADS, A_V_DIM)
    return (y_prompt, y_sample, new_k, new_v, jnp.stack(sd_l, axis=1),
            jnp.stack(cm_l, axis=1), jnp.stack(nm_l, axis=1), jnp.stack(mm_l, axis=1))
```
